```python
import jax, jax.numpy as jnp
from jax import lax
import numpy as np

D_MODEL = 1024
BATCH = 2
SEQ = 8192
DEPTH = 1

GRID_W = 64
CTX_LEN = 256
CONV_W = 1024
CONV_K = 3
N_HEADS = 16
HEAD_DIM = 64
ATTN_W = N_HEADS * HEAD_DIM
WIN_ROWS_MAX = 8
WIN_COLS = 16
ROPE_BASE = 10000.0
EPS = 1e-6
N_BRANCH = 2
IN_SIZES = [CONV_W] * 4 + [ATTN_W] * 4 + [D_MODEL] * N_BRANCH
IN_COLS = sum(IN_SIZES)
K_OFF = 4 * CONV_W + ATTN_W
V_END = 4 * CONV_W + 3 * ATTN_W

kernel_name = "hybrid_gated_conv_neighbourhood_attn_block"


def _rmsnorm(u, g):
    u32 = u.astype(jnp.float32)
    return (u32 * lax.rsqrt(jnp.mean(u32 * u32, axis=-1, keepdims=True) + EPS)).astype(u.dtype) * g


def _modulation(cond, w_mod, b_mod):
    m = jax.nn.silu(cond) @ w_mod + b_mod
    return jnp.split(m, 3, axis=-1)


def _split_in(p):
    idx = np.cumsum(IN_SIZES)[:-1].tolist()
    return jnp.split(p, idx, axis=-1)


def _dwconv3(u, w, b):
    L = u.shape[1]
    up = jnp.pad(u, ((0, 0), (1, 1), (0, 0)))
    return up[:, :L] * w[0] + up[:, 1:L + 1] * w[1] + up[:, 2:] * w[2] + b


def _rope_2d_tables(L, dtype):
    t = jnp.arange(L, dtype=jnp.int32)
    row = (t // GRID_W).astype(jnp.float32)
    col = (t % GRID_W).astype(jnp.float32)
    half = HEAD_DIM // 2
    inv = ROPE_BASE ** (-jnp.arange(0, half, 2, dtype=jnp.float32) / half)
    ang_r = row[:, None] * inv
    ang_c = col[:, None] * inv
    ang = jnp.concatenate([ang_r, ang_r, ang_c, ang_c], axis=-1)
    return jnp.cos(ang).astype(dtype), jnp.sin(ang).astype(dtype)


def _rot_half_axial(u):
    u1, u2, u3, u4 = jnp.split(u, 4, axis=-1)
    return jnp.concatenate([-u2, u1, -u4, u3], axis=-1)


def _apply_rope(u, cos, sin):
    return u * cos[None, :, None, :] + _rot_half_axial(u) * sin[None, :, None, :]


def _heads(u):
    return u.reshape(u.shape[0], u.shape[1], N_HEADS, HEAD_DIM)


def _neighbourhood_attention(q, k, v, k_ctx, v_ctx, rpb):
    bsz, L, H, Dh = q.shape
    rows = L // GRID_W
    wr = min(WIN_ROWS_MAX, rows)
    n_nb = wr * WIN_COLS
    scale = HEAD_DIM ** -0.5
    qg = q.reshape(bsz, rows, GRID_W, H, Dh)
    kg = k.reshape(bsz, rows, GRID_W, H, Dh)
    vg = v.reshape(bsz, rows, GRID_W, H, Dh)
    col = jnp.arange(GRID_W, dtype=jnp.int32)
    col_start = jnp.clip(col - WIN_COLS // 2, 0, GRID_W - WIN_COLS)
    col_idx = col_start[:, None] + jnp.arange(WIN_COLS, dtype=jnp.int32)
    dc_idx = col_idx - col[:, None] + (WIN_COLS - 1)

    def row_block(r):
        rs = jnp.clip(r - wr // 2, 0, rows - wr)
        q_r = lax.dynamic_index_in_dim(qg, r, axis=1, keepdims=False)
        k_slab = lax.dynamic_slice_in_dim(kg, rs, wr, axis=1)
        v_slab = lax.dynamic_slice_in_dim(vg, rs, wr, axis=1)
        k_win = k_slab[:, :, col_idx]
        v_win = v_slab[:, :, col_idx]
        dr_idx = rs + jnp.arange(wr, dtype=jnp.int32) - r + (WIN_ROWS_MAX - 1)
        bias = rpb[:, dr_idx[None, :, None], dc_idx[:, None, :]]
        s_nb = (jnp.einsum('bqhd,bpqjhd->bqhpj', q_r, k_win).astype(jnp.float32) * scale
                + jnp.transpose(bias, (1, 0, 2, 3))[None].astype(jnp.float32))
        s_nb = s_nb.reshape(bsz, GRID_W, H, n_nb)
        s_ctx = jnp.einsum('bqhd,bchd->bqhc', q_r, k_ctx).astype(jnp.float32) * scale
        p = jax.nn.softmax(jnp.concatenate([s_nb, s_ctx], axis=-1), axis=-1).astype(v.dtype)
        p_nb = p[..., :n_nb].reshape(bsz, GRID_W, H, wr, WIN_COLS)
        p_ctx = p[..., n_nb:]
        return (jnp.einsum('bqhpj,bpqjhd->bqhd', p_nb, v_win)
                + jnp.einsum('bqhc,bchd->bqhd', p_ctx, v_ctx))

    out = lax.map(row_block, jnp.arange(rows, dtype=jnp.int32))
    return jnp.transpose(out, (1, 0, 2, 3, 4)).reshape(bsz, L, H * Dh)


def _ctx_attention(q, k, v):
    scale = HEAD_DIM ** -0.5
    s = jnp.einsum('bqhd,bkhd->bhqk', q, k).astype(jnp.float32) * scale
    p = jax.nn.softmax(s, axis=-1).astype(v.dtype)
    o = jnp.einsum('bhqk,bkhd->bqhd', p, v)
    return o.reshape(o.shape[0], o.shape[1], ATTN_W)


def _mixer_out(parts, attn_o, conv_w, conv_b, w_out_conv, w_out_attn, w_o):
    b_gate, c_gate, x_in, z_a, _, _, _, z_b, g_a, g_b = parts
    y_a = (jax.nn.silu(z_a) * b_gate * _dwconv3(c_gate * x_in, conv_w, conv_b)) @ w_out_conv
    y_b = (jax.nn.silu(z_b) * attn_o) @ w_out_attn
    merged = jax.nn.sigmoid(g_a) * y_a + jax.nn.sigmoid(g_b) * y_b
    return merged @ w_o


def setup_inputs(seed: int = 0) -> dict:
    key = jax.random.key(seed)
    ks = jax.random.split(key, 16)
    f32 = jnp.float32
    nrm = lambda k, shape, s: jax.random.normal(k, shape, f32) * s
    return {
        "x": nrm(ks[0], (BATCH, SEQ, D_MODEL), 1.0),
        "c": nrm(ks[1], (BATCH, D_MODEL), 1.0),
        "ctx": nrm(ks[2], (BATCH, CTX_LEN, D_MODEL), 1.0),
        "c_ctx": nrm(ks[3], (D_MODEL,), 1.0),
        "w_mod": nrm(ks[4], (DEPTH, D_MODEL, 3 * D_MODEL), D_MODEL ** -0.5),
        "b_mod": nrm(ks[5], (DEPTH, 3 * D_MODEL), 0.02),
        "pre_g": 1.0 + nrm(ks[6], (DEPTH, D_MODEL), 0.05),
        "post_g": 1.0 + nrm(ks[7], (DEPTH, D_MODEL), 0.05),
        "w_in": nrm(ks[8], (DEPTH, D_MODEL, IN_COLS), D_MODEL ** -0.5),
        "conv_w": nrm(ks[9], (DEPTH, CONV_K, CONV_W), CONV_K ** -0.5),
        "conv_b": nrm(ks[10], (DEPTH, CONV_W), 0.02),
        "rpb": nrm(ks[11], (DEPTH, N_HEADS, 2 * WIN_ROWS_MAX - 1, 2 * WIN_COLS - 1), 0.1),
        "w_out_conv": nrm(ks[12], (DEPTH, CONV_W, D_MODEL), CONV_W ** -0.5),
        "w_out_attn": nrm(ks[13], (DEPTH, ATTN_W, D_MODEL), ATTN_W ** -0.5),
        "w_o": nrm(ks[14], (DEPTH, D_MODEL, D_MODEL), D_MODEL ** -0.5),
    }


def reference(x, c, ctx, c_ctx, w_mod, b_mod, pre_g, post_g, w_in, conv_w, conv_b, rpb,
              w_out_conv, w_out_attn, w_o):
    L = x.shape[1]
    cos, sin = _rope_2d_tables(L, x.dtype)
    for i in range(DEPTH):
        last = i == DEPTH - 1
        sh, sc, gt = _modulation(c, w_mod[i], b_mod[i])
        sh_c, sc_c, gt_c = _modulation(c_ctx, w_mod[i], b_mod[i])
        h = _rmsnorm(x, pre_g[i]) * (1.0 + sc[:, None, :]) + sh[:, None, :]
        hc = _rmsnorm(ctx, pre_g[i]) * (1.0 + sc_c) + sh_c
        parts = _split_in(h @ w_in[i])
        q = _apply_rope(_heads(parts[4]), cos, sin)
        k = _apply_rope(_heads(parts[5]), cos, sin)
        v = _heads(parts[6])
        if last:
            kv_c = hc @ w_in[i][:, K_OFF:V_END]
            k_c, v_c = jnp.split(kv_c, 2, axis=-1)
            k_c, v_c = _heads(k_c), _heads(v_c)
        else:
            parts_c = _split_in(hc @ w_in[i])
            k_c, v_c = _heads(parts_c[5]), _heads(parts_c[6])
        attn = _neighbourhood_attention(q, k, v, k_c, v_c, rpb[i])
        y = _mixer_out(parts, attn, conv_w[i], conv_b[i], w_out_conv[i], w_out_attn[i], w_o[i])
        if not last:
            attn_c = _ctx_attention(_heads(parts_c[4]), k_c, v_c)
            y_c = _mixer_out(parts_c, attn_c, conv_w[i], conv_b[i], w_out_conv[i], w_out_attn[i], w_o[i])
            ctx = ctx + gt_c * _rmsnorm(y_c, post_g[i])
        x = x + gt[:, None, :] * _rmsnorm(y, post_g[i])
    return x
```

```python
import functools

import numpy as np
import jax
import jax.numpy as jnp
from jax import lax
from jax.experimental import pallas as pl
from jax.experimental.pallas import tpu as pltpu

D_MODEL = 1024
GRID_W = 64
N_HEADS = 16
HEAD_DIM = 64
WIN_ROWS = 8
WIN_COLS = 16
ROPE_BASE = 10000.0
EPS = 1e-6
PART_W = 1024
N_PARTS_IN = 10
PART_Q, PART_K, PART_V = 4, 5, 6
OUT_BG, OUT_CG, OUT_XI, OUT_ZA, OUT_Q, OUT_V, OUT_ZB, OUT_GA, OUT_GB = range(9)
LANES = 128
N_PAIRS = N_HEADS // 2
WIN_PAIRS = 5
NEG = -1e30
BF16 = jnp.bfloat16
F32 = jnp.float32


def _modulated_norm(x, g, sc, sh):
    ms = jnp.mean(x * x, axis=-1, keepdims=True)
    return x * lax.rsqrt(ms + EPS) * (g * (1.0 + sc)) + sh


def _modulation_kernel(c_ref, w_ref, b_ref, o_ref):
    c = c_ref[...]
    o_ref[...] = jnp.dot(c * jax.nn.sigmoid(c), w_ref[...], preferred_element_type=F32,
                         precision=lax.Precision.HIGHEST) + b_ref[...]


def _modulation(cond8, w_mod, b_mod):
    n = w_mod.shape[1]
    tn = 768
    return pl.pallas_call(
        _modulation_kernel,
        grid=(n // tn,),
        in_specs=[pl.BlockSpec((8, D_MODEL), lambda j: (0, 0)),
                  pl.BlockSpec((D_MODEL, tn), lambda j: (0, j)),
                  pl.BlockSpec((1, tn), lambda j: (0, j))],
        out_specs=pl.BlockSpec((8, tn), lambda j: (0, j)),
        out_shape=jax.ShapeDtypeStruct((8, n), F32),
        name="modulation",
    )(cond8, w_mod, b_mod)


def _bias_consts():
    q = np.arange(GRID_W)[:, None]
    c = np.arange(GRID_W)[None, :]
    cs = np.clip(q - WIN_COLS // 2, 0, GRID_W - WIN_COLS)
    inwin = (c >= cs) & (c < cs + WIN_COLS)
    dc = c - q + (WIN_COLS - 1)
    sel = np.zeros((32, GRID_W, GRID_W), np.float32)
    qq, cc = np.nonzero(inwin)
    sel[dc[qq, cc], qq, cc] = 1.0
    negm = np.where(inwin, 0.0, NEG).astype(np.float32)
    return sel.reshape(32, GRID_W * GRID_W), negm.reshape(1, GRID_W * GRID_W)


def _bias_kernel(rpb_ref, sel_ref, neg_ref, o_ref):
    o_ref[...] = jnp.dot(rpb_ref[...], sel_ref[...], preferred_element_type=F32,
                         precision=lax.Precision.HIGHEST) + neg_ref[...]


def _row_window(r, rows):
    if isinstance(r, int):
        rs = min(max(r - WIN_ROWS // 2, 0), rows - WIN_ROWS)
        sp = min(rs // 2, rows // 2 - WIN_PAIRS)
        var = r if r <= 4 else (r - (rows - 4) + 7 if r >= rows - 4 else 5 + (r & 1))
        return rs, sp, var
    rs = jnp.clip(r - WIN_ROWS // 2, 0, rows - WIN_ROWS)
    sp = jnp.minimum(lax.shift_right_logical(rs, 1), rows // 2 - WIN_PAIRS)
    var = jnp.where(r <= 4, r, jnp.where(r >= rows - 4, r - (rows - 4) + 7, 5 + (r & 1)))
    return rs, sp, var


N_BIAS_VAR = 11
def _variant_rows(rows):
    return [0, 1, 2, 3, 4, 6, 5, rows - 4, rows - 3, rows - 2, rows - 1]


def _bias_tables(rpb, rows):
    sel, negm = _bias_consts()
    n_dr = 2 * WIN_ROWS - 1
    rpb2 = jnp.pad(rpb.reshape(N_HEADS * n_dr, 2 * WIN_COLS - 1), ((0, 0), (0, 1)))
    bcol = pl.pallas_call(
        _bias_kernel,
        out_shape=jax.ShapeDtypeStruct((N_HEADS * n_dr, GRID_W * GRID_W), F32),
        name="bias_table",
    )(rpb2, jnp.asarray(sel), jnp.asarray(negm))
    bcol = bcol.reshape(N_HEADS, n_dr, GRID_W, GRID_W)
    masked = jnp.full((N_HEADS, GRID_W, GRID_W), NEG, F32)
    variants = []
    for r in _variant_rows(rows):
        rs, sp, _ = _row_window(r, rows)
        blocks = []
        for w in range(2 * WIN_PAIRS):
            row = 2 * sp + w
            blocks.append(bcol[:, row - r + WIN_ROWS - 1] if rs <= row < rs + WIN_ROWS else masked)
        variants.append(jnp.concatenate(blocks, axis=-1))
    t = jnp.stack(variants, axis=1)
    return t.reshape(N_PAIRS, 2, N_BIAS_VAR, GRID_W, 2 * WIN_PAIRS * GRID_W).transpose(
        0, 2, 1, 3, 4).reshape(N_PAIRS, N_BIAS_VAR, 2 * GRID_W, 2 * WIN_PAIRS * GRID_W)


def _ctx_kv_kernel(x_ref, sc_ref, sh_ref, g_ref, w_ref, o_ref):
    h = _modulated_norm(x_ref[...], g_ref[...], sc_ref[...], sh_ref[...]).astype(BF16)
    o_ref[...] = jnp.dot(h, w_ref[...], preferred_element_type=F32).astype(BF16)


def _ctx_kv(ctx2, sc_c, sh_c, pre_g, w_in_bf):
    m = ctx2.shape[0]
    return pl.pallas_call(
        _ctx_kv_kernel,
        grid=(2,),
        in_specs=[pl.BlockSpec((m, D_MODEL), lambda j: (0, 0)),
                  pl.BlockSpec((1, D_MODEL), lambda j: (0, 0)),
                  pl.BlockSpec((1, D_MODEL), lambda j: (0, 0)),
                  pl.BlockSpec((1, D_MODEL), lambda j: (0, 0)),
                  pl.BlockSpec((D_MODEL, PART_W), lambda j: (0, PART_K + j))],
        out_specs=pl.BlockSpec((m, PART_W), lambda j: (0, j)),
        out_shape=jax.ShapeDtypeStruct((m, 2 * PART_W), BF16),
        name="ctx_kv",
    )(ctx2, sc_c, sh_c, pre_g, w_in_bf)


def _inproj_kernel(x_ref, sc_ref, sh_ref, g_ref, w_ref, cos_ref, sina_ref, sinb_ref,
                   p_ref, kt_ref, h_ref):
    n = pl.program_id(1)
    tm = x_ref.shape[0]

    @pl.when(n == 0)
    def _():
        h_ref[...] = _modulated_norm(x_ref[...], g_ref[...], sc_ref[0], sh_ref[0]).astype(BF16)

    r = jnp.dot(h_ref[...], w_ref[...], preferred_element_type=F32)

    def rope_slab(s):
        u = r[:, s * LANES:(s + 1) * LANES]
        return (u * cos_ref[...] + pltpu.roll(u, 16, 1) * sina_ref[...]
                + pltpu.roll(u, LANES - 16, 1) * sinb_ref[...])

    @pl.when(n == PART_Q)
    def _():
        for s in range(PART_W // LANES):
            p_ref[:, s * LANES:(s + 1) * LANES] = (rope_slab(s) * (HEAD_DIM ** -0.5)).astype(BF16)

    @pl.when(n == PART_K)
    def _():
        for s in range(PART_W // LANES):
            kt = rope_slab(s).T
            for j in range(tm // LANES):
                kt_ref[0, j, s * LANES:(s + 1) * LANES, :] = kt[:, j * LANES:(j + 1) * LANES].astype(BF16)

    @pl.when((n != PART_Q) & (n != PART_K))
    def _():
        p_ref[...] = r.astype(BF16)


def _inproj(x2, sc, sh, pre_g, w_in_bf, cos, sina, sinb, batch, seq, tm):
    m = x2.shape[0]
    tpb = seq // tm
    return pl.pallas_call(
        _inproj_kernel,
        grid=(m // tm, N_PARTS_IN),
        in_specs=[pl.BlockSpec((tm, D_MODEL), lambda i, n: (i, 0)),
                  pl.BlockSpec((1, 1, D_MODEL), lambda i, n: (i // tpb, 0, 0)),
                  pl.BlockSpec((1, 1, D_MODEL), lambda i, n: (i // tpb, 0, 0)),
                  pl.BlockSpec((1, D_MODEL), lambda i, n: (0, 0)),
                  pl.BlockSpec((D_MODEL, PART_W), lambda i, n: (0, n)),
                  pl.BlockSpec((tm, LANES), lambda i, n: (i % tpb, 0)),
                  pl.BlockSpec((tm, LANES), lambda i, n: (i % tpb, 0)),
                  pl.BlockSpec((tm, LANES), lambda i, n: (i % tpb, 0))],
        out_specs=[pl.BlockSpec((tm, PART_W), lambda i, n: (i, jnp.where(n <= PART_Q, n, n - 1))),
                   pl.BlockSpec((1, tm // LANES, PART_W, LANES), lambda i, n: (i // tpb, i % tpb, 0, 0))],
        out_shape=[jax.ShapeDtypeStruct((m, (N_PARTS_IN - 1) * PART_W), BF16),
                   jax.ShapeDtypeStruct((batch, seq // LANES, PART_W, LANES), BF16)],
        scratch_shapes=[pltpu.VMEM((tm, D_MODEL), BF16)],
        compiler_params=pltpu.CompilerParams(dimension_semantics=("arbitrary", "arbitrary")),
        name="in_projection",
    )(x2, sc, sh, pre_g, w_in_bf, cos, sina, sinb)


def _attn_kernel(q_ref, kt_ref, v_ref, kct_ref, vc_ref, bias_ref, o_ref, *, rows):
    first = lax.broadcasted_iota(jnp.int32, (GRID_W, LANES), 1) < HEAD_DIM
    kct = kct_ref[0]
    vc = vc_ref[0]
    nb = 2 * WIN_PAIRS * GRID_W

    def body(r, carry):
        _, sp, var = _row_window(r, rows)
        q = q_ref[pl.ds(pl.multiple_of(r * GRID_W, GRID_W), GRID_W), :]
        zero = jnp.zeros_like(q)
        lhs = jnp.concatenate([jnp.where(first, q, zero), jnp.where(first, zero, q)], axis=0)
        kcat = jnp.concatenate([kt_ref[0, sp + g] for g in range(WIN_PAIRS)] + [kct], axis=1)
        s = jnp.dot(lhs, kcat, preferred_element_type=F32)
        s_nb = s[:, :nb] + bias_ref[0, var]
        s_cx = s[:, nb:]
        m = jnp.maximum(jnp.max(s_nb, axis=-1, keepdims=True), jnp.max(s_cx, axis=-1, keepdims=True))
        p_nb = jnp.exp(s_nb - m)
        p_cx = jnp.exp(s_cx - m)
        l = jnp.sum(p_nb, axis=-1, keepdims=True) + jnp.sum(p_cx, axis=-1, keepdims=True)
        vwin = v_ref[pl.ds(pl.multiple_of(sp * LANES, LANES), nb), :]
        o = (jnp.dot(p_nb.astype(BF16), vwin, preferred_element_type=F32)
             + jnp.dot(p_cx.astype(BF16), vc, preferred_element_type=F32)) / l
        out = jnp.where(first, o[:GRID_W], o[GRID_W:])
        o_ref[pl.ds(pl.multiple_of(r * GRID_W, GRID_W), GRID_W), :] = out.astype(BF16)
        return carry

    lax.fori_loop(0, rows, body, 0)


def _attention(p9, kt, kct, vc, bias, batch, seq):
    rows = seq // GRID_W
    n_ctx = vc.shape[1]
    ppp = PART_W // LANES
    return pl.pallas_call(
        functools.partial(_attn_kernel, rows=rows),
        grid=(N_PAIRS, batch),
        in_specs=[pl.BlockSpec((seq, LANES), lambda hp, b: (b, OUT_Q * ppp + hp)),
                  pl.BlockSpec((1, seq // LANES, LANES, LANES), lambda hp, b: (b, 0, hp, 0)),
                  pl.BlockSpec((seq, LANES), lambda hp, b: (b, OUT_V * ppp + hp)),
                  pl.BlockSpec((1, LANES, n_ctx), lambda hp, b: (b, hp, 0)),
                  pl.BlockSpec((1, n_ctx, LANES), lambda hp, b: (b, 0, hp)),
                  pl.BlockSpec((1, N_BIAS_VAR, 2 * GRID_W, 2 * WIN_PAIRS * GRID_W),
                               lambda hp, b: (hp, 0, 0, 0))],
        out_specs=pl.BlockSpec((seq, LANES), lambda hp, b: (b, hp)),
        out_shape=jax.ShapeDtypeStruct((batch * seq, N_HEADS * HEAD_DIM), BF16),
        compiler_params=pltpu.CompilerParams(dimension_semantics=("arbitrary", "arbitrary")),
        name="attention",
    )(p9, kt, p9, kct, vc, bias)


def _mixer_kernel(bg_ref, cg_ref, xi_ref, za_ref, zb_ref, ga_ref, gb_ref, at_ref,
                  cgp_ref, xip_ref, cgn_ref, xin_ref, x_ref, gt_ref, cw_ref, cb_ref, pg_ref,
                  woc_ref, woa_ref, wo_ref, o_ref, *, tiles_per_batch, halo):
    t = pl.program_id(0) % tiles_per_batch
    u = cg_ref[...].astype(F32) * xi_ref[...].astype(F32)
    tm = u.shape[0]
    has_prev = jnp.where(t > 0, 1.0, 0.0)
    has_next = jnp.where(t < tiles_per_batch - 1, 1.0, 0.0)
    prev_row = cgp_ref[halo - 1:halo, :].astype(F32) * xip_ref[halo - 1:halo, :].astype(F32) * has_prev
    next_row = cgn_ref[0:1, :].astype(F32) * xin_ref[0:1, :].astype(F32) * has_next
    row = lax.broadcasted_iota(jnp.int32, u.shape, 0)
    u_prev = jnp.where(row == 0, prev_row, pltpu.roll(u, 1, 0))
    u_next = jnp.where(row == tm - 1, next_row, pltpu.roll(u, tm - 1, 0))
    conv = u_prev * cw_ref[0:1, :] + u * cw_ref[1:2, :] + u_next * cw_ref[2:3, :] + cb_ref[...]
    za = za_ref[...].astype(F32)
    a = (za * jax.nn.sigmoid(za)) * bg_ref[...].astype(F32) * conv
    y_a = jnp.dot(a.astype(BF16), woc_ref[...], preferred_element_type=F32)
    zb = zb_ref[...].astype(F32)
    bb = (zb * jax.nn.sigmoid(zb)) * at_ref[...].astype(F32)
    y_b = jnp.dot(bb.astype(BF16), woa_ref[...], preferred_element_type=F32)
    merged = (jax.nn.sigmoid(ga_ref[...].astype(F32)) * y_a
              + jax.nn.sigmoid(gb_ref[...].astype(F32)) * y_b)
    y = jnp.dot(merged.astype(BF16), wo_ref[...], preferred_element_type=F32)
    ms = jnp.mean(y * y, axis=-1, keepdims=True)
    o_ref[...] = x_ref[...] + gt_ref[0] * ((y * lax.rsqrt(ms + EPS)) * pg_ref[...])


def _mixer(p9, attn, x2, gt, conv_w, conv_b, post_g, woc, woa, wo, seq, tm):
    m = x2.shape[0]
    tpb = seq // tm
    halo = 16
    hb = tm // halo
    nhb = m // halo

    def part(k):
        return pl.BlockSpec((tm, PART_W), lambda i: (i, k))

    def prev(k):
        return pl.BlockSpec((halo, PART_W), lambda i: (jnp.maximum(i * hb - 1, 0), k))

    def nxt(k):
        return pl.BlockSpec((halo, PART_W), lambda i: (jnp.minimum((i + 1) * hb, nhb - 1), k))

    def full(shape):
        return pl.BlockSpec(shape, lambda i: (0,) * len(shape))

    return pl.pallas_call(
        functools.partial(_mixer_kernel, tiles_per_batch=tpb, halo=halo),
        grid=(m // tm,),
        in_specs=[part(OUT_BG), part(OUT_CG), part(OUT_XI), part(OUT_ZA), part(OUT_ZB),
                  part(OUT_GA), part(OUT_GB),
                  pl.BlockSpec((tm, PART_W), lambda i: (i, 0)),
                  prev(OUT_CG), prev(OUT_XI), nxt(OUT_CG), nxt(OUT_XI),
                  pl.BlockSpec((tm, D_MODEL), lambda i: (i, 0)),
                  pl.BlockSpec((1, 1, D_MODEL), lambda i: (i // tpb, 0, 0)),
                  full((3, PART_W)), full((1, PART_W)), full((1, D_MODEL)),
                  full((PART_W, D_MODEL)), full((PART_W, D_MODEL)), full((D_MODEL, D_MODEL))],
        out_specs=pl.BlockSpec((tm, D_MODEL), lambda i: (i, 0)),
        out_shape=jax.ShapeDtypeStruct((m, D_MODEL), F32),
        compiler_params=pltpu.CompilerParams(dimension_semantics=("arbitrary",)),
        name="mixer_out",
    )(p9, p9, p9, p9, p9, p9, p9, attn, p9, p9, p9, p9, x2, gt,
      conv_w, conv_b, post_g, woc, woa, wo)


def _rope_tables(seq):
    t = jnp.arange(seq, dtype=jnp.int32)
    row = (t // GRID_W).astype(F32)
    col = (t % GRID_W).astype(F32)
    half = HEAD_DIM // 2
    inv = ROPE_BASE ** (-jnp.arange(0, half, 2, dtype=F32) / half)
    ang_r = row[:, None] * inv
    ang_c = col[:, None] * inv
    ang = jnp.concatenate([ang_r, ang_r, ang_c, ang_c], axis=-1)
    cos = jnp.tile(jnp.cos(ang), (1, LANES // HEAD_DIM))
    sin = jnp.tile(jnp.sin(ang), (1, LANES // HEAD_DIM))
    odd_chunk = ((jnp.arange(LANES) // (HEAD_DIM // 4)) % 2 == 1)[None, :]
    sina = jnp.where(odd_chunk, sin, 0.0)
    sinb = jnp.where(odd_chunk, 0.0, -sin)
    return cos, sina, sinb


def _layer(x, c, ctx, c_ctx, w_mod, b_mod, pre_g, post_g, w_in, conv_w, conv_b, rpb,
           w_out_conv, w_out_attn, w_o):
    batch, seq, _ = x.shape
    n_ctx = ctx.shape[1]
    rows = seq // GRID_W
    assert rows >= 2 * WIN_PAIRS and seq % LANES == 0 and batch <= 7

    cond8 = jnp.zeros((8, D_MODEL), F32).at[:batch].set(c).at[batch].set(c_ctx)
    mod = _modulation(cond8, w_mod, b_mod.reshape(1, -1))
    sh, sc, gt = (mod[:batch, k * D_MODEL:(k + 1) * D_MODEL].reshape(batch, 1, D_MODEL) for k in range(3))
    sh_c, sc_c = (mod[batch:batch + 1, k * D_MODEL:(k + 1) * D_MODEL] for k in range(2))

    pre_g2 = pre_g.reshape(1, D_MODEL)
    w_in_bf = w_in.astype(BF16)
    kvc = _ctx_kv(ctx.reshape(batch * n_ctx, D_MODEL), sc_c, sh_c, pre_g2, w_in_bf)
    kct = jnp.transpose(kvc[:, :PART_W].reshape(batch, n_ctx, PART_W), (0, 2, 1))
    vc = kvc[:, PART_W:].reshape(batch, n_ctx, PART_W)

    x2 = x.reshape(batch * seq, D_MODEL)
    cos, sina, sinb = _rope_tables(seq)
    tm_in = min(1024, seq)
    p9, kt = _inproj(x2, sc, sh, pre_g2, w_in_bf, cos, sina, sinb, batch, seq, tm_in)

    bias = _bias_tables(rpb, rows)
    attn = _attention(p9, kt, kct, vc, bias, batch, seq)

    tm_out = min(512, seq)
    out = _mixer(p9, attn, x2, gt, conv_w, conv_b.reshape(1, -1), post_g.reshape(1, -1),
                 w_out_conv.astype(BF16), w_out_attn.astype(BF16), w_o.astype(BF16), seq, tm_out)
    return out.reshape(batch, seq, D_MODEL)


def kernel(x, c, ctx, c_ctx, w_mod, b_mod, pre_g, post_g, w_in, conv_w, conv_b, rpb,
           w_out_conv, w_out_attn, w_o):
    depth = w_mod.shape[0]
    assert depth == 1, "context stream update between layers is not implemented"
    return _layer(x, c, ctx, c_ctx, w_mod[0], b_mod[0], pre_g[0], post_g[0], w_in[0], conv_w[0],
                  conv_b[0], rpb[0], w_out_conv[0], w_out_attn[0], w_o[0])
```

```python
import functools

import numpy as np
import jax
import jax.numpy as jnp
from jax import lax
from jax.experimental import pallas as pl
from jax.experimental.pallas import tpu as pltpu

D_MODEL = 1024
GRID_W = 64
N_HEADS = 16
HEAD_DIM = 64
WIN_ROWS = 8
WIN_COLS = 16
ROPE_BASE = 10000.0
EPS = 1e-6
PART_W = 1024
N_PARTS_IN = 10
PART_Q, PART_K, PART_V = 4, 5, 6
OUT_BG, OUT_CG, OUT_XI, OUT_ZA, OUT_K, OUT_ZB, OUT_GA, OUT_GB = range(8)
LANES = 128
N_PAIRS = N_HEADS // 2
WIN_PAIRS = 5
NEG = -1e30
BF16 = jnp.bfloat16
F32 = jnp.float32


def _modulated_norm(x, g, sc, sh):
    ms = jnp.mean(x * x, axis=-1, keepdims=True)
    return x * lax.rsqrt(ms + EPS) * (g * (1.0 + sc)) + sh


def _modulation_kernel(c_ref, w_ref, b_ref, o_ref):
    c = c_ref[...]
    o_ref[...] = jnp.dot(c * jax.nn.sigmoid(c), w_ref[...], preferred_element_type=F32,
                         precision=lax.Precision.HIGHEST) + b_ref[...]


def _modulation(cond8, w_mod, b_mod):
    n = w_mod.shape[1]
    tn = 768
    return pl.pallas_call(
        _modulation_kernel,
        grid=(n // tn,),
        in_specs=[pl.BlockSpec((8, D_MODEL), lambda j: (0, 0)),
                  pl.BlockSpec((D_MODEL, tn), lambda j: (0, j)),
                  pl.BlockSpec((1, tn), lambda j: (0, j))],
        out_specs=pl.BlockSpec((8, tn), lambda j: (0, j)),
        out_shape=jax.ShapeDtypeStruct((8, n), F32),
        name="modulation",
    )(cond8, w_mod, b_mod)


def _bias_consts():
    q = np.arange(GRID_W)[None, :]
    c = np.arange(GRID_W)[:, None]
    cs = np.clip(q - WIN_COLS // 2, 0, GRID_W - WIN_COLS)
    inwin = (c >= cs) & (c < cs + WIN_COLS)
    dc = c - q + (WIN_COLS - 1)
    sel = np.zeros((32, GRID_W, GRID_W), np.float32)
    cc, qq = np.nonzero(inwin)
    sel[dc[cc, qq], cc, qq] = 1.0
    negm = np.where(inwin, 0.0, NEG).astype(np.float32)
    return sel.reshape(32, GRID_W * GRID_W), negm.reshape(1, GRID_W * GRID_W)


def _bias_kernel(rpb_ref, sel_ref, neg_ref, o_ref):
    o_ref[...] = jnp.dot(rpb_ref[...], sel_ref[...], preferred_element_type=F32,
                         precision=lax.Precision.HIGHEST) + neg_ref[...]


def _pair_window(t, n_row_pairs):
    if isinstance(t, int):
        sp = min(max(t - 2, 0), n_row_pairs - WIN_PAIRS)
        var = t if t < 2 else (t - (n_row_pairs - 2) + 3 if t >= n_row_pairs - 2 else 2)
        return sp, var
    sp = jnp.clip(t - 2, 0, n_row_pairs - WIN_PAIRS)
    var = jnp.where(t < 2, t, jnp.where(t >= n_row_pairs - 2, t - (n_row_pairs - 2) + 3, 2))
    return sp, var


N_BIAS_VAR = 5


def _bias_tables(rpb, rows):
    sel, negm = _bias_consts()
    n_dr = 2 * WIN_ROWS - 1
    n_rp = rows // 2
    rpb2 = jnp.pad(rpb.reshape(N_HEADS * n_dr, 2 * WIN_COLS - 1), ((0, 0), (0, 1)))
    bcol = pl.pallas_call(
        _bias_kernel,
        out_shape=jax.ShapeDtypeStruct((N_HEADS * n_dr, GRID_W * GRID_W), F32),
        name="bias_table",
    )(rpb2, jnp.asarray(sel), jnp.asarray(negm))
    bcol = bcol.reshape(N_PAIRS, 2, n_dr, GRID_W, GRID_W)
    masked = jnp.full((N_PAIRS, GRID_W, GRID_W), NEG, F32)
    variants = []
    for t in [0, 1, 2, n_rp - 2, n_rp - 1]:
        sp, _ = _pair_window(t, n_rp)
        win_rows = []
        for w in range(2 * WIN_PAIRS):
            krow = 2 * sp + w
            lanes = []
            for hh in range(2):
                for rr in range(2):
                    r = 2 * t + rr
                    rs = min(max(r - WIN_ROWS // 2, 0), rows - WIN_ROWS)
                    ok = rs <= krow < rs + WIN_ROWS
                    lanes.append(bcol[:, hh, krow - r + WIN_ROWS - 1] if ok else masked)
            win_rows.append(jnp.concatenate(lanes, axis=-1))
        variants.append(jnp.concatenate(win_rows, axis=-2))
    return jnp.stack(variants, axis=1)


def _ctx_kv_kernel(x_ref, sc_ref, sh_ref, g_ref, w_ref, o_ref):
    h = _modulated_norm(x_ref[...], g_ref[...], sc_ref[...], sh_ref[...]).astype(BF16)
    o_ref[...] = jnp.dot(h, w_ref[...], preferred_element_type=F32).astype(BF16)


def _ctx_kv(ctx2, sc_c, sh_c, pre_g, w_in_bf):
    m = ctx2.shape[0]
    return pl.pallas_call(
        _ctx_kv_kernel,
        grid=(2,),
        in_specs=[pl.BlockSpec((m, D_MODEL), lambda j: (0, 0)),
                  pl.BlockSpec((1, D_MODEL), lambda j: (0, 0)),
                  pl.BlockSpec((1, D_MODEL), lambda j: (0, 0)),
                  pl.BlockSpec((1, D_MODEL), lambda j: (0, 0)),
                  pl.BlockSpec((D_MODEL, PART_W), lambda j: (0, PART_K + j))],
        out_specs=pl.BlockSpec((m, PART_W), lambda j: (0, j)),
        out_shape=jax.ShapeDtypeStruct((m, 2 * PART_W), BF16),
        name="ctx_kv",
    )(ctx2, sc_c, sh_c, pre_g, w_in_bf)


def _inproj_kernel(x_ref, sc_ref, sh_ref, g_ref, w_ref, cos_ref, sina_ref, sinb_ref,
                   p_ref, qt_ref, vt_ref, h_ref):
    n = pl.program_id(1)
    tm = x_ref.shape[0]
    n_slabs = PART_W // LANES

    @pl.when(n == 0)
    def _():
        h_ref[...] = _modulated_norm(x_ref[...], g_ref[...], sc_ref[0], sh_ref[0]).astype(BF16)

    r = jnp.dot(h_ref[...], w_ref[...], preferred_element_type=F32)

    def slab(s):
        return r[:, s * LANES:(s + 1) * LANES]

    def rope(u):
        return (u * cos_ref[...] + pltpu.roll(u, 16, 1) * sina_ref[...]
                + pltpu.roll(u, LANES - 16, 1) * sinb_ref[...])

    def store_transposed(t_ref, s, u):
        ut = u.T
        for j in range(tm // LANES):
            t_ref[0, j, s * LANES:(s + 1) * LANES, :] = ut[:, j * LANES:(j + 1) * LANES].astype(BF16)

    @pl.when(n == PART_Q)
    def _():
        for s in range(n_slabs):
            store_transposed(qt_ref, s, rope(slab(s)) * (HEAD_DIM ** -0.5))

    @pl.when(n == PART_K)
    def _():
        for s in range(n_slabs):
            p_ref[:, s * LANES:(s + 1) * LANES] = rope(slab(s)).astype(BF16)

    @pl.when(n == PART_V)
    def _():
        for s in range(n_slabs):
            store_transposed(vt_ref, s, slab(s))

    @pl.when((n < PART_Q) | (n > PART_V))
    def _():
        p_ref[...] = r.astype(BF16)


def _inproj(x2, sc, sh, pre_g, w_in_bf, cos, sina, sinb, batch, seq, tm):
    m = x2.shape[0]
    tpb = seq // tm
    t_spec = pl.BlockSpec((1, tm // LANES, PART_W, LANES), lambda i, n: (i // tpb, i % tpb, 0, 0))
    t_shape = jax.ShapeDtypeStruct((batch, seq // LANES, PART_W, LANES), BF16)

    def p_col(n):
        return n - (n >= PART_Q).astype(jnp.int32) - (n >= PART_V).astype(jnp.int32)

    return pl.pallas_call(
        _inproj_kernel,
        grid=(m // tm, N_PARTS_IN),
        in_specs=[pl.BlockSpec((tm, D_MODEL), lambda i, n: (i, 0)),
                  pl.BlockSpec((1, 1, D_MODEL), lambda i, n: (i // tpb, 0, 0)),
                  pl.BlockSpec((1, 1, D_MODEL), lambda i, n: (i // tpb, 0, 0)),
                  pl.BlockSpec((1, D_MODEL), lambda i, n: (0, 0)),
                  pl.BlockSpec((D_MODEL, PART_W), lambda i, n: (0, n)),
                  pl.BlockSpec((tm, LANES), lambda i, n: (i % tpb, 0)),
                  pl.BlockSpec((tm, LANES), lambda i, n: (i % tpb, 0)),
                  pl.BlockSpec((tm, LANES), lambda i, n: (i % tpb, 0))],
        out_specs=[pl.BlockSpec((tm, PART_W), lambda i, n: (i, p_col(n))), t_spec, t_spec],
        out_shape=[jax.ShapeDtypeStruct((m, (N_PARTS_IN - 2) * PART_W), BF16), t_shape, t_shape],
        scratch_shapes=[pltpu.VMEM((tm, D_MODEL), BF16)],
        compiler_params=pltpu.CompilerParams(dimension_semantics=("arbitrary", "arbitrary")),
        name="in_projection",
    )(x2, sc, sh, pre_g, w_in_bf, cos, sina, sinb)


def _attn_kernel(qt_ref, k_ref, vt_ref, kc_ref, vct_ref, bias_ref, o_ref, s_ref, p_ref, linv_ref,
                 *, n_row_pairs):
    low = lax.broadcasted_iota(jnp.int32, (LANES, LANES), 0) < HEAD_DIM
    nb = WIN_PAIRS * LANES
    last = n_row_pairs - 1

    def scores(t, slot):
        sp, var = _pair_window(t, n_row_pairs)
        qt = qt_ref[0, t]
        zero = jnp.zeros_like(qt)
        rhs = jnp.concatenate([jnp.where(low, qt, zero), jnp.where(low, zero, qt)], axis=1)
        kwin = k_ref[pl.ds(pl.multiple_of(sp * LANES, LANES), nb), :]
        s_ref[slot, :nb, :] = jnp.dot(kwin, rhs, preferred_element_type=F32) + bias_ref[0, var]
        s_ref[slot, nb:, :] = jnp.dot(kc_ref[0], rhs, preferred_element_type=F32)

    def softmax(slot):
        s = s_ref[slot]
        p = jnp.exp(s - jnp.max(s, axis=0, keepdims=True))
        linv_ref[slot] = 1.0 / jnp.sum(p, axis=0, keepdims=True)
        p_ref[slot] = p.astype(BF16)

    def values(t, slot):
        sp, _ = _pair_window(t, n_row_pairs)
        vfull = jnp.concatenate([vt_ref[0, sp + g] for g in range(WIN_PAIRS)] + [vct_ref[0]], axis=1)
        ot = jnp.dot(vfull, p_ref[slot], preferred_element_type=F32) * linv_ref[slot]
        out = jnp.where(low, ot[:, :LANES], ot[:, LANES:]).T
        o_ref[pl.ds(pl.multiple_of(t * LANES, LANES), LANES), :] = out.astype(BF16)

    scores(0, 0)
    p_ref[1] = jnp.zeros(p_ref.shape[1:], BF16)
    linv_ref[1] = jnp.zeros(linv_ref.shape[1:], F32)

    def body(i, carry):
        t = 2 * i
        scores(t + 1, 1)
        softmax(0)
        values(jnp.maximum(t - 1, 0), 1)
        scores(jnp.minimum(t + 2, last), 0)
        softmax(1)
        values(t, 0)
        return carry

    lax.fori_loop(0, n_row_pairs // 2, body, 0)
    values(last, 1)


def _attention(p8, qt, vt, kc, vct, bias, batch, seq):
    n_ctx = kc.shape[1]
    n_keys = WIN_PAIRS * LANES + n_ctx
    assert (seq // LANES) % 2 == 0
    ppp = PART_W // LANES
    t_spec = pl.BlockSpec((1, seq // LANES, LANES, LANES), lambda hp, b: (b, 0, hp, 0))
    return pl.pallas_call(
        functools.partial(_attn_kernel, n_row_pairs=seq // LANES),
        grid=(N_PAIRS, batch),
        in_specs=[t_spec,
                  pl.BlockSpec((seq, LANES), lambda hp, b: (b, OUT_K * ppp + hp)),
                  t_spec,
                  pl.BlockSpec((1, n_ctx, LANES), lambda hp, b: (b, 0, hp)),
                  pl.BlockSpec((1, LANES, n_ctx), lambda hp, b: (b, hp, 0)),
                  pl.BlockSpec((1, N_BIAS_VAR, WIN_PAIRS * LANES, 2 * LANES),
                               lambda hp, b: (hp, 0, 0, 0))],
        out_specs=pl.BlockSpec((seq, LANES), lambda hp, b: (b, hp)),
        out_shape=jax.ShapeDtypeStruct((batch * seq, N_HEADS * HEAD_DIM), BF16),
        scratch_shapes=[pltpu.VMEM((2, n_keys, 2 * LANES), F32),
                        pltpu.VMEM((2, n_keys, 2 * LANES), BF16),
                        pltpu.VMEM((2, 1, 2 * LANES), F32)],
        compiler_params=pltpu.CompilerParams(dimension_semantics=("arbitrary", "arbitrary")),
        name="attention",
    )(qt, p8, vt, kc, vct, bias)


def _mixer_kernel(bg_ref, cg_ref, xi_ref, za_ref, zb_ref, ga_ref, gb_ref, at_ref,
                  cgp_ref, xip_ref, cgn_ref, xin_ref, x_ref, gt_ref, cw_ref, cb_ref, pg_ref,
                  woc_ref, woa_ref, wo_ref, o_ref, *, tiles_per_batch, halo):
    t = pl.program_id(0) % tiles_per_batch
    u = cg_ref[...].astype(F32) * xi_ref[...].astype(F32)
    tm = u.shape[0]
    has_prev = jnp.where(t > 0, 1.0, 0.0)
    has_next = jnp.where(t < tiles_per_batch - 1, 1.0, 0.0)
    prev_row = cgp_ref[halo - 1:halo, :].astype(F32) * xip_ref[halo - 1:halo, :].astype(F32) * has_prev
    next_row = cgn_ref[0:1, :].astype(F32) * xin_ref[0:1, :].astype(F32) * has_next
    row = lax.broadcasted_iota(jnp.int32, u.shape, 0)
    u_prev = jnp.where(row == 0, prev_row, pltpu.roll(u, 1, 0))
    u_next = jnp.where(row == tm - 1, next_row, pltpu.roll(u, tm - 1, 0))
    conv = u_prev * cw_ref[0:1, :] + u * cw_ref[1:2, :] + u_next * cw_ref[2:3, :] + cb_ref[...]
    za = za_ref[...].astype(F32)
    a = (za * jax.nn.sigmoid(za)) * bg_ref[...].astype(F32) * conv
    y_a = jnp.dot(a.astype(BF16), woc_ref[...], preferred_element_type=F32)
    zb = zb_ref[...].astype(F32)
    bb = (zb * jax.nn.sigmoid(zb)) * at_ref[...].astype(F32)
    y_b = jnp.dot(bb.astype(BF16), woa_ref[...], preferred_element_type=F32)
    merged = (jax.nn.sigmoid(ga_ref[...].astype(F32)) * y_a
              + jax.nn.sigmoid(gb_ref[...].astype(F32)) * y_b)
    y = jnp.dot(merged.astype(BF16), wo_ref[...], preferred_element_type=F32)
    ms = jnp.mean(y * y, axis=-1, keepdims=True)
    o_ref[...] = x_ref[...] + gt_ref[0] * ((y * lax.rsqrt(ms + EPS)) * pg_ref[...])


def _mixer(p8, attn, x2, gt, conv_w, conv_b, post_g, woc, woa, wo, seq, tm):
    m = x2.shape[0]
    tpb = seq // tm
    halo = 16
    hb = tm // halo
    nhb = m // halo

    def part(k):
        return pl.BlockSpec((tm, PART_W), lambda i: (i, k))

    def prev(k):
        return pl.BlockSpec((halo, PART_W), lambda i: (jnp.maximum(i * hb - 1, 0), k))

    def nxt(k):
        return pl.BlockSpec((halo, PART_W), lambda i: (jnp.minimum((i + 1) * hb, nhb - 1), k))

    def full(shape):
        return pl.BlockSpec(shape, lambda i: (0,) * len(shape))

    return pl.pallas_call(
        functools.partial(_mixer_kernel, tiles_per_batch=tpb, halo=halo),
        grid=(m // tm,),
        in_specs=[part(OUT_BG), part(OUT_CG), part(OUT_XI), part(OUT_ZA), part(OUT_ZB),
                  part(OUT_GA), part(OUT_GB),
                  pl.BlockSpec((tm, PART_W), lambda i: (i, 0)),
                  prev(OUT_CG), prev(OUT_XI), nxt(OUT_CG), nxt(OUT_XI),
                  pl.BlockSpec((tm, D_MODEL), lambda i: (i, 0)),
                  pl.BlockSpec((1, 1, D_MODEL), lambda i: (i // tpb, 0, 0)),
                  full((3, PART_W)), full((1, PART_W)), full((1, D_MODEL)),
                  full((PART_W, D_MODEL)), full((PART_W, D_MODEL)), full((D_MODEL, D_MODEL))],
        out_specs=pl.BlockSpec((tm, D_MODEL), lambda i: (i, 0)),
        out_shape=jax.ShapeDtypeStruct((m, D_MODEL), F32),
        compiler_params=pltpu.CompilerParams(dimension_semantics=("arbitrary",)),
        name="mixer_out",
    )(p8, p8, p8, p8, p8, p8, p8, attn, p8, p8, p8, p8, x2, gt,
      conv_w, conv_b, post_g, woc, woa, wo)


def _rope_tables(seq):
    t = jnp.arange(seq, dtype=jnp.int32)
    row = (t // GRID_W).astype(F32)
    col = (t % GRID_W).astype(F32)
    half = HEAD_DIM // 2
    inv = ROPE_BASE ** (-jnp.arange(0, half, 2, dtype=F32) / half)
    ang_r = row[:, None] * inv
    ang_c = col[:, None] * inv
    ang = jnp.concatenate([ang_r, ang_r, ang_c, ang_c], axis=-1)
    cos = jnp.tile(jnp.cos(ang), (1, LANES // HEAD_DIM))
    sin = jnp.tile(jnp.sin(ang), (1, LANES // HEAD_DIM))
    odd_chunk = ((jnp.arange(LANES) // (HEAD_DIM // 4)) % 2 == 1)[None, :]
    sina = jnp.where(odd_chunk, sin, 0.0)
    sinb = jnp.where(odd_chunk, 0.0, -sin)
    return cos, sina, sinb


def _layer(x, c, ctx, c_ctx, w_mod, b_mod, pre_g, post_g, w_in, conv_w, conv_b, rpb,
           w_out_conv, w_out_attn, w_o):
    batch, seq, _ = x.shape
    n_ctx = ctx.shape[1]
    rows = seq // GRID_W
    assert rows >= 2 * WIN_PAIRS and seq % LANES == 0 and batch <= 7

    cond8 = jnp.zeros((8, D_MODEL), F32).at[:batch].set(c).at[batch].set(c_ctx)
    mod = _modulation(cond8, w_mod, b_mod.reshape(1, -1))
    sh, sc, gt = (mod[:batch, k * D_MODEL:(k + 1) * D_MODEL].reshape(batch, 1, D_MODEL) for k in range(3))
    sh_c, sc_c = (mod[batch:batch + 1, k * D_MODEL:(k + 1) * D_MODEL] for k in range(2))

    pre_g2 = pre_g.reshape(1, D_MODEL)
    w_in_bf = w_in.astype(BF16)
    kvc = _ctx_kv(ctx.reshape(batch * n_ctx, D_MODEL), sc_c, sh_c, pre_g2, w_in_bf)
    kc = kvc[:, :PART_W].reshape(batch, n_ctx, PART_W)
    vct = jnp.transpose(kvc[:, PART_W:].reshape(batch, n_ctx, PART_W), (0, 2, 1))

    x2 = x.reshape(batch * seq, D_MODEL)
    cos, sina, sinb = _rope_tables(seq)
    tm_in = min(1024, seq)
    p8, qt, vt = _inproj(x2, sc, sh, pre_g2, w_in_bf, cos, sina, sinb, batch, seq, tm_in)

    bias = _bias_tables(rpb, rows)
    attn = _attention(p8, qt, vt, kc, vct, bias, batch, seq)

    tm_out = min(512, seq)
    out = _mixer(p8, attn, x2, gt, conv_w, conv_b.reshape(1, -1), post_g.reshape(1, -1),
                 w_out_conv.astype(BF16), w_out_attn.astype(BF16), w_o.astype(BF16), seq, tm_out)
    return out.reshape(batch, seq, D_MODEL)


def kernel(x, c, ctx, c_ctx, w_mod, b_mod, pre_g, post_g, w_in, conv_w, conv_b, rpb,
           w_out_conv, w_out_attn, w_o):
    depth = w_mod.shape[0]
    assert depth == 1, "context stream update between layers is not implemented"
    return _layer(x, c, ctx, c_ctx, w_mod[0], b_mod[0], pre_g[0], post_g[0], w_in[0], conv_w[0],
                  conv_b[0], rpb[0], w_out_conv[0], w_out_attn[0], w_o[0])
```

```python
import functools

import numpy as np
import jax
import jax.numpy as jnp
from jax import lax
from jax.experimental import pallas as pl
from jax.experimental.pallas import tpu as pltpu

D_MODEL = 1024
GRID_W = 64
N_HEADS = 16
HEAD_DIM = 64
WIN_ROWS = 8
WIN_COLS = 16
ROPE_BASE = 10000.0
EPS = 1e-6
PART_W = 1024
N_PARTS_IN = 10
PART_Q, PART_K, PART_V = 4, 5, 6
OUT_BG, OUT_CG, OUT_XI, OUT_ZA, OUT_K, OUT_ZB, OUT_GA, OUT_GB = range(8)
LANES = 128
N_PAIRS = N_HEADS // 2
WIN_PAIRS = 5
NEG = -1e30
BF16 = jnp.bfloat16
F32 = jnp.float32


def _modulated_norm(x, g, sc, sh):
    ms = jnp.mean(x * x, axis=-1, keepdims=True)
    return x * lax.rsqrt(ms + EPS) * (g * (1.0 + sc)) + sh


def _modulation_kernel(c_ref, w_ref, b_ref, o_ref):
    c = c_ref[...]
    o_ref[...] = jnp.dot(c * jax.nn.sigmoid(c), w_ref[...], preferred_element_type=F32,
                         precision=lax.Precision.HIGHEST) + b_ref[...]


def _modulation(cond8, w_mod, b_mod):
    n = w_mod.shape[1]
    tn = 768
    return pl.pallas_call(
        _modulation_kernel,
        grid=(n // tn,),
        in_specs=[pl.BlockSpec((8, D_MODEL), lambda j: (0, 0)),
                  pl.BlockSpec((D_MODEL, tn), lambda j: (0, j)),
                  pl.BlockSpec((1, tn), lambda j: (0, j))],
        out_specs=pl.BlockSpec((8, tn), lambda j: (0, j)),
        out_shape=jax.ShapeDtypeStruct((8, n), F32),
        name="modulation",
    )(cond8, w_mod, b_mod)


def _bias_consts():
    q = np.arange(GRID_W)[None, :]
    c = np.arange(GRID_W)[:, None]
    cs = np.clip(q - WIN_COLS // 2, 0, GRID_W - WIN_COLS)
    inwin = (c >= cs) & (c < cs + WIN_COLS)
    dc = c - q + (WIN_COLS - 1)
    sel = np.zeros((32, GRID_W, GRID_W), np.float32)
    cc, qq = np.nonzero(inwin)
    sel[dc[cc, qq], cc, qq] = 1.0
    negm = np.where(inwin, 0.0, NEG).astype(np.float32)
    return sel.reshape(32, GRID_W * GRID_W), negm.reshape(1, GRID_W * GRID_W)


def _bias_kernel(rpb_ref, sel_ref, neg_ref, o_ref):
    o_ref[...] = jnp.dot(rpb_ref[...], sel_ref[...], preferred_element_type=F32,
                         precision=lax.Precision.HIGHEST) + neg_ref[...]


def _pair_window(t, n_row_pairs):
    if isinstance(t, int):
        sp = min(max(t - 2, 0), n_row_pairs - WIN_PAIRS)
        var = t if t < 2 else (t - (n_row_pairs - 2) + 3 if t >= n_row_pairs - 2 else 2)
        return sp, var
    sp = jnp.clip(t - 2, 0, n_row_pairs - WIN_PAIRS)
    var = jnp.where(t < 2, t, jnp.where(t >= n_row_pairs - 2, t - (n_row_pairs - 2) + 3, 2))
    return sp, var


N_BIAS_VAR = 5


def _bias_tables(rpb, rows):
    sel, negm = _bias_consts()
    n_dr = 2 * WIN_ROWS - 1
    n_rp = rows // 2
    rpb2 = jnp.pad(rpb.reshape(N_HEADS * n_dr, 2 * WIN_COLS - 1), ((0, 0), (0, 1)))
    bcol = pl.pallas_call(
        _bias_kernel,
        out_shape=jax.ShapeDtypeStruct((N_HEADS * n_dr, GRID_W * GRID_W), F32),
        name="bias_table",
    )(rpb2, jnp.asarray(sel), jnp.asarray(negm))
    bcol = bcol.reshape(N_PAIRS, 2, n_dr, GRID_W, GRID_W)
    masked = jnp.full((N_PAIRS, GRID_W, GRID_W), NEG, F32)
    variants = []
    for t in [0, 1, 2, n_rp - 2, n_rp - 1]:
        sp, _ = _pair_window(t, n_rp)
        win_rows = []
        for w in range(2 * WIN_PAIRS):
            krow = 2 * sp + w
            lanes = []
            for hh in range(2):
                for rr in range(2):
                    r = 2 * t + rr
                    rs = min(max(r - WIN_ROWS // 2, 0), rows - WIN_ROWS)
                    ok = rs <= krow < rs + WIN_ROWS
                    lanes.append(bcol[:, hh, krow - r + WIN_ROWS - 1] if ok else masked)
            win_rows.append(jnp.concatenate(lanes, axis=-1))
        variants.append(jnp.concatenate(win_rows, axis=-2))
    return jnp.stack(variants, axis=1)


def _ctx_kv_kernel(x_ref, sc_ref, sh_ref, g_ref, w_ref, o_ref):
    h = _modulated_norm(x_ref[...], g_ref[...], sc_ref[...], sh_ref[...]).astype(BF16)
    o_ref[...] = jnp.dot(h, w_ref[...], preferred_element_type=F32).astype(BF16)


def _ctx_kv(ctx2, sc_c, sh_c, pre_g, w_in_bf):
    m = ctx2.shape[0]
    return pl.pallas_call(
        _ctx_kv_kernel,
        grid=(2,),
        in_specs=[pl.BlockSpec((m, D_MODEL), lambda j: (0, 0)),
                  pl.BlockSpec((1, D_MODEL), lambda j: (0, 0)),
                  pl.BlockSpec((1, D_MODEL), lambda j: (0, 0)),
                  pl.BlockSpec((1, D_MODEL), lambda j: (0, 0)),
                  pl.BlockSpec((D_MODEL, PART_W), lambda j: (0, PART_K + j))],
        out_specs=pl.BlockSpec((m, PART_W), lambda j: (0, j)),
        out_shape=jax.ShapeDtypeStruct((m, 2 * PART_W), BF16),
        name="ctx_kv",
    )(ctx2, sc_c, sh_c, pre_g, w_in_bf)


def _split_halves_layout(w_part):
    k = w_part.shape[0]
    q4 = HEAD_DIM // 4
    return w_part.reshape(k, N_HEADS, 2, 2, q4).transpose(0, 3, 1, 2, 4).reshape(k, PART_W)


def _inproj_kernel(x_ref, sc_ref, sh_ref, g_ref, w_ref, cos_ref, sin_ref, p_ref, qt_ref, vt_ref, h_ref):
    tm = x_ref.shape[0]
    n_half = PART_W // (2 * LANES)
    h_ref[...] = _modulated_norm(x_ref[...], g_ref[...], sc_ref[0], sh_ref[0]).astype(BF16)

    def slab(r, s):
        return r[:, s * LANES:(s + 1) * LANES]

    def rope(r):
        firsts, seconds = [], []
        for s in range(n_half):
            a, b = slab(r, s), slab(r, n_half + s)
            firsts.append(a * cos_ref[...] - b * sin_ref[...])
            seconds.append(b * cos_ref[...] + a * sin_ref[...])
        return firsts + seconds

    def store_transposed(t_ref, s, u):
        ut = u.astype(BF16).T
        for j in range(tm // LANES):
            t_ref[0, j, s * LANES:(s + 1) * LANES, :] = ut[:, j * LANES:(j + 1) * LANES]

    out_col = 0
    for n in range(N_PARTS_IN):
        r = jnp.dot(h_ref[...], w_ref[:, n * PART_W:(n + 1) * PART_W], preferred_element_type=F32)
        if n == PART_Q:
            for s, u in enumerate(rope(r)):
                store_transposed(qt_ref, s, u * (HEAD_DIM ** -0.5))
            continue
        if n == PART_V:
            for s in range(PART_W // LANES):
                store_transposed(vt_ref, s, slab(r, s))
            continue
        if n == PART_K:
            for s, u in enumerate(rope(r)):
                p_ref[:, out_col * PART_W + s * LANES:out_col * PART_W + (s + 1) * LANES] = u.astype(BF16)
        else:
            p_ref[:, out_col * PART_W:(out_col + 1) * PART_W] = r.astype(BF16)
        out_col += 1


def _inproj(x2, sc, sh, pre_g, w_in_bf, cos, sin, batch, seq, tm):
    m = x2.shape[0]
    tpb = seq // tm
    n_out = N_PARTS_IN - 2
    t_spec = pl.BlockSpec((1, tm // LANES, PART_W, LANES), lambda i: (i // tpb, i % tpb, 0, 0))
    t_shape = jax.ShapeDtypeStruct((batch, seq // LANES, PART_W, LANES), BF16)
    return pl.pallas_call(
        _inproj_kernel,
        grid=(m // tm,),
        in_specs=[pl.BlockSpec((tm, D_MODEL), lambda i: (i, 0)),
                  pl.BlockSpec((1, 1, D_MODEL), lambda i: (i // tpb, 0, 0)),
                  pl.BlockSpec((1, 1, D_MODEL), lambda i: (i // tpb, 0, 0)),
                  pl.BlockSpec((1, D_MODEL), lambda i: (0, 0)),
                  pl.BlockSpec((D_MODEL, N_PARTS_IN * PART_W), lambda i: (0, 0),
                               pipeline_mode=pl.Buffered(1)),
                  pl.BlockSpec((tm, LANES), lambda i: (i % tpb, 0)),
                  pl.BlockSpec((tm, LANES), lambda i: (i % tpb, 0))],
        out_specs=[pl.BlockSpec((tm, n_out * PART_W), lambda i: (i, 0)), t_spec, t_spec],
        out_shape=[jax.ShapeDtypeStruct((m, n_out * PART_W), BF16), t_shape, t_shape],
        scratch_shapes=[pltpu.VMEM((tm, D_MODEL), BF16)],
        compiler_params=pltpu.CompilerParams(dimension_semantics=("arbitrary",)),
        name="in_projection",
    )(x2, sc, sh, pre_g, w_in_bf, cos, sin)


def _attn_kernel(qta_ref, qtb_ref, ka_ref, kb_ref, vt_ref, kca_ref, kcb_ref, vct_ref, bias_ref,
                 o_ref, s_ref, p_ref, linv_ref, *, n_row_pairs):
    low = lax.broadcasted_iota(jnp.int32, (LANES, LANES), 0) < HEAD_DIM
    nb = WIN_PAIRS * LANES
    last = n_row_pairs - 1
    head_of_dim = (lax.broadcasted_iota(jnp.int32, (2 * LANES, LANES), 0) % LANES) // (HEAD_DIM // 2)
    first_head = 2 * (pl.program_id(0) % 2)
    sel0 = head_of_dim == first_head
    sel1 = head_of_dim == first_head + 1

    def scores(t, slot):
        sp, var = _pair_window(t, n_row_pairs)
        qt = jnp.concatenate([qta_ref[0, t], qtb_ref[0, t]], axis=0)
        zero = jnp.zeros_like(qt)
        rhs = jnp.concatenate([jnp.where(sel0, qt, zero), jnp.where(sel1, qt, zero)], axis=1)
        rows = pl.ds(pl.multiple_of(sp * LANES, LANES), nb)
        kwin = jnp.concatenate([ka_ref[rows, :], kb_ref[rows, :]], axis=1)
        kc = jnp.concatenate([kca_ref[0], kcb_ref[0]], axis=1)
        s_ref[slot, :nb, :] = jnp.dot(kwin, rhs, preferred_element_type=F32) + bias_ref[0, var]
        s_ref[slot, nb:, :] = jnp.dot(kc, rhs, preferred_element_type=F32)

    def softmax(slot):
        s = s_ref[slot]
        p = jnp.exp(s - jnp.max(s, axis=0, keepdims=True))
        linv_ref[slot] = 1.0 / jnp.sum(p, axis=0, keepdims=True)
        p_ref[slot] = p.astype(BF16)

    def values(t, slot):
        sp, _ = _pair_window(t, n_row_pairs)
        vfull = jnp.concatenate([vt_ref[0, sp + g] for g in range(WIN_PAIRS)] + [vct_ref[0]], axis=1)
        ot = jnp.dot(vfull, p_ref[slot], preferred_element_type=F32) * linv_ref[slot]
        out = jnp.where(low, ot[:, :LANES], ot[:, LANES:]).T
        o_ref[pl.ds(pl.multiple_of(t * LANES, LANES), LANES), :] = out.astype(BF16)

    scores(0, 0)
    p_ref[1] = jnp.zeros(p_ref.shape[1:], BF16)
    linv_ref[1] = jnp.zeros(linv_ref.shape[1:], F32)

    def body(i, carry):
        t = 2 * i
        scores(t + 1, 1)
        softmax(0)
        values(jnp.maximum(t - 1, 0), 1)
        scores(jnp.minimum(t + 2, last), 0)
        softmax(1)
        values(t, 0)
        return carry

    lax.fori_loop(0, n_row_pairs // 2, body, 0)
    values(last, 1)


def _attention(p8, qt, vt, kc, vct, bias, batch, seq):
    n_ctx = kc.shape[1]
    n_keys = WIN_PAIRS * LANES + n_ctx
    assert (seq // LANES) % 2 == 0
    spp = PART_W // LANES
    half = spp // 2

    def t_spec(slab):
        return pl.BlockSpec((1, seq // LANES, LANES, LANES), lambda hp, b: (b, 0, slab(hp), 0))

    return pl.pallas_call(
        functools.partial(_attn_kernel, n_row_pairs=seq // LANES),
        grid=(N_PAIRS, batch),
        in_specs=[t_spec(lambda hp: hp // 2), t_spec(lambda hp: half + hp // 2),
                  pl.BlockSpec((seq, LANES), lambda hp, b: (b, OUT_K * spp + hp // 2)),
                  pl.BlockSpec((seq, LANES), lambda hp, b: (b, OUT_K * spp + half + hp // 2)),
                  t_spec(lambda hp: hp),
                  pl.BlockSpec((1, n_ctx, LANES), lambda hp, b: (b, 0, hp // 2)),
                  pl.BlockSpec((1, n_ctx, LANES), lambda hp, b: (b, 0, half + hp // 2)),
                  pl.BlockSpec((1, LANES, n_ctx), lambda hp, b: (b, hp, 0)),
                  pl.BlockSpec((1, N_BIAS_VAR, WIN_PAIRS * LANES, 2 * LANES),
                               lambda hp, b: (hp, 0, 0, 0))],
        out_specs=pl.BlockSpec((seq, LANES), lambda hp, b: (b, hp)),
        out_shape=jax.ShapeDtypeStruct((batch * seq, N_HEADS * HEAD_DIM), BF16),
        scratch_shapes=[pltpu.VMEM((2, n_keys, 2 * LANES), F32),
                        pltpu.VMEM((2, n_keys, 2 * LANES), BF16),
                        pltpu.VMEM((2, 1, 2 * LANES), F32)],
        compiler_params=pltpu.CompilerParams(dimension_semantics=("arbitrary", "arbitrary")),
        name="attention",
    )(qt, qt, p8, p8, vt, kc, kc, vct, bias)


def _mixer_kernel(bg_ref, cg_ref, xi_ref, za_ref, zb_ref, ga_ref, gb_ref, at_ref,
                  cgp_ref, xip_ref, cgn_ref, xin_ref, x_ref, gt_ref, cw_ref, cb_ref, pg_ref,
                  woc_ref, woa_ref, wo_ref, o_ref, *, tiles_per_batch, halo):
    t = pl.program_id(0) % tiles_per_batch
    u = cg_ref[...].astype(F32) * xi_ref[...].astype(F32)
    tm = u.shape[0]
    has_prev = jnp.where(t > 0, 1.0, 0.0)
    has_next = jnp.where(t < tiles_per_batch - 1, 1.0, 0.0)
    prev_row = cgp_ref[halo - 1:halo, :].astype(F32) * xip_ref[halo - 1:halo, :].astype(F32) * has_prev
    next_row = cgn_ref[0:1, :].astype(F32) * xin_ref[0:1, :].astype(F32) * has_next
    row = lax.broadcasted_iota(jnp.int32, u.shape, 0)
    u_prev = jnp.where(row == 0, prev_row, pltpu.roll(u, 1, 0))
    u_next = jnp.where(row == tm - 1, next_row, pltpu.roll(u, tm - 1, 0))
    conv = u_prev * cw_ref[0:1, :] + u * cw_ref[1:2, :] + u_next * cw_ref[2:3, :] + cb_ref[...]
    za = za_ref[...].astype(F32)
    a = (za * jax.nn.sigmoid(za)) * bg_ref[...].astype(F32) * conv
    y_a = jnp.dot(a.astype(BF16), woc_ref[...], preferred_element_type=F32)
    zb = zb_ref[...].astype(F32)
    bb = (zb * jax.nn.sigmoid(zb)) * at_ref[...].astype(F32)
    y_b = jnp.dot(bb.astype(BF16), woa_ref[...], preferred_element_type=F32)
    merged = (jax.nn.sigmoid(ga_ref[...].astype(F32)) * y_a
              + jax.nn.sigmoid(gb_ref[...].astype(F32)) * y_b)
    y = jnp.dot(merged.astype(BF16), wo_ref[...], preferred_element_type=F32)
    ms = jnp.mean(y * y, axis=-1, keepdims=True)
    o_ref[...] = x_ref[...] + gt_ref[0] * ((y * lax.rsqrt(ms + EPS)) * pg_ref[...])


def _mixer(p8, attn, x2, gt, conv_w, conv_b, post_g, woc, woa, wo, seq, tm):
    m = x2.shape[0]
    tpb = seq // tm
    halo = 16
    hb = tm // halo
    nhb = m // halo

    def part(k):
        return pl.BlockSpec((tm, PART_W), lambda i: (i, k))

    def prev(k):
        return pl.BlockSpec((halo, PART_W), lambda i: (jnp.maximum(i * hb - 1, 0), k))

    def nxt(k):
        return pl.BlockSpec((halo, PART_W), lambda i: (jnp.minimum((i + 1) * hb, nhb - 1), k))

    def full(shape):
        return pl.BlockSpec(shape, lambda i: (0,) * len(shape))

    return pl.pallas_call(
        functools.partial(_mixer_kernel, tiles_per_batch=tpb, halo=halo),
        grid=(m // tm,),
        in_specs=[part(OUT_BG), part(OUT_CG), part(OUT_XI), part(OUT_ZA), part(OUT_ZB),
                  part(OUT_GA), part(OUT_GB),
                  pl.BlockSpec((tm, PART_W), lambda i: (i, 0)),
                  prev(OUT_CG), prev(OUT_XI), nxt(OUT_CG), nxt(OUT_XI),
                  pl.BlockSpec((tm, D_MODEL), lambda i: (i, 0)),
                  pl.BlockSpec((1, 1, D_MODEL), lambda i: (i // tpb, 0, 0)),
                  full((3, PART_W)), full((1, PART_W)), full((1, D_MODEL)),
                  full((PART_W, D_MODEL)), full((PART_W, D_MODEL)), full((D_MODEL, D_MODEL))],
        out_specs=pl.BlockSpec((tm, D_MODEL), lambda i: (i, 0)),
        out_shape=jax.ShapeDtypeStruct((m, D_MODEL), F32),
        compiler_params=pltpu.CompilerParams(dimension_semantics=("arbitrary",)),
        name="mixer_out",
    )(p8, p8, p8, p8, p8, p8, p8, attn, p8, p8, p8, p8, x2, gt,
      conv_w, conv_b, post_g, woc, woa, wo)


def _rope_tables(seq):
    t = jnp.arange(seq, dtype=jnp.int32)
    row = (t // GRID_W).astype(F32)
    col = (t % GRID_W).astype(F32)
    half = HEAD_DIM // 2
    inv = ROPE_BASE ** (-jnp.arange(0, half, 2, dtype=F32) / half)
    ang_r = row[:, None] * inv
    ang_c = col[:, None] * inv
    ang = jnp.tile(jnp.concatenate([ang_r, ang_c], axis=-1), (1, LANES // (HEAD_DIM // 2)))
    return jnp.cos(ang), jnp.sin(ang)


def _layer(x, c, ctx, c_ctx, w_mod, b_mod, pre_g, post_g, w_in, conv_w, conv_b, rpb,
           w_out_conv, w_out_attn, w_o):
    batch, seq, _ = x.shape
    n_ctx = ctx.shape[1]
    rows = seq // GRID_W
    assert rows >= 2 * WIN_PAIRS and seq % LANES == 0 and batch <= 7

    cond8 = jnp.zeros((8, D_MODEL), F32).at[:batch].set(c).at[batch].set(c_ctx)
    mod = _modulation(cond8, w_mod, b_mod.reshape(1, -1))
    sh, sc, gt = (mod[:batch, k * D_MODEL:(k + 1) * D_MODEL].reshape(batch, 1, D_MODEL) for k in range(3))
    sh_c, sc_c = (mod[batch:batch + 1, k * D_MODEL:(k + 1) * D_MODEL] for k in range(2))

    pre_g2 = pre_g.reshape(1, D_MODEL)
    def part(n):
        return w_in[:, n * PART_W:(n + 1) * PART_W]

    w_in_bf = jnp.concatenate(
        [w_in[:, :PART_Q * PART_W], _split_halves_layout(part(PART_Q)), _split_halves_layout(part(PART_K)),
         w_in[:, PART_V * PART_W:]], axis=1).astype(BF16)
    kvc = _ctx_kv(ctx.reshape(batch * n_ctx, D_MODEL), sc_c, sh_c, pre_g2, w_in_bf)
    kc = kvc[:, :PART_W].reshape(batch, n_ctx, PART_W)
    vct = jnp.transpose(kvc[:, PART_W:].reshape(batch, n_ctx, PART_W), (0, 2, 1))

    x2 = x.reshape(batch * seq, D_MODEL)
    cos, sin = _rope_tables(seq)
    tm_in = min(512, seq)
    p8, qt, vt = _inproj(x2, sc, sh, pre_g2, w_in_bf, cos, sin, batch, seq, tm_in)

    bias = _bias_tables(rpb, rows)
    attn = _attention(p8, qt, vt, kc, vct, bias, batch, seq)

    tm_out = min(512, seq)
    out = _mixer(p8, attn, x2, gt, conv_w, conv_b.reshape(1, -1), post_g.reshape(1, -1),
                 w_out_conv.astype(BF16), w_out_attn.astype(BF16), w_o.astype(BF16), seq, tm_out)
    return out.reshape(batch, seq, D_MODEL)


def kernel(x, c, ctx, c_ctx, w_mod, b_mod, pre_g, post_g, w_in, conv_w, conv_b, rpb,
           w_out_conv, w_out_attn, w_o):
    depth = w_mod.shape[0]
    assert depth == 1, "context stream update between layers is not implemented"
    return _layer(x, c, ctx, c_ctx, w_mod[0], b_mod[0], pre_g[0], post_g[0], w_in[0], conv_w[0],
                  conv_b[0], rpb[0], w_out_conv[0], w_out_attn[0], w_o[0])
```

```python
import functools

import numpy as np
import jax
import jax.numpy as jnp
from jax import lax
from jax.experimental import pallas as pl
from jax.experimental.pallas import tpu as pltpu

D_MODEL = 1024
GRID_W = 64
N_HEADS = 16
HEAD_DIM = 64
WIN_ROWS = 8
WIN_COLS = 16
ROPE_BASE = 10000.0
EPS = 1e-6
PART_W = 1024
N_PARTS_IN = 10
PART_Q, PART_K, PART_V = 4, 5, 6
OUT_BG, OUT_CG, OUT_XI, OUT_ZA, OUT_K, OUT_ZB, OUT_GA, OUT_GB = range(8)
LANES = 128
N_PAIRS = N_HEADS // 2
WIN_PAIRS = 5
NEG = -1e30
RED_ROWS = 64
ATTN_GROUP = 4
BF16 = jnp.bfloat16
F32 = jnp.float32


def _modulated_norm(x, g, sc, sh):
    ms = jnp.mean(x * x, axis=-1, keepdims=True)
    return x * lax.rsqrt(ms + EPS) * (g * (1.0 + sc)) + sh


def _modulation_kernel(c_ref, w_ref, b_ref, o_ref):
    c = c_ref[...]
    o_ref[...] = jnp.dot(c * jax.nn.sigmoid(c), w_ref[...], preferred_element_type=F32,
                         precision=lax.Precision.HIGHEST) + b_ref[...]


def _modulation(cond8, w_mod, b_mod):
    n = w_mod.shape[1]
    tn = 768
    return pl.pallas_call(
        _modulation_kernel,
        grid=(n // tn,),
        in_specs=[pl.BlockSpec((8, D_MODEL), lambda j: (0, 0)),
                  pl.BlockSpec((D_MODEL, tn), lambda j: (0, j)),
                  pl.BlockSpec((1, tn), lambda j: (0, j))],
        out_specs=pl.BlockSpec((8, tn), lambda j: (0, j)),
        out_shape=jax.ShapeDtypeStruct((8, n), F32),
        name="modulation",
    )(cond8, w_mod, b_mod)


def _bias_consts():
    q = np.arange(GRID_W)[None, :]
    c = np.arange(GRID_W)[:, None]
    cs = np.clip(q - WIN_COLS // 2, 0, GRID_W - WIN_COLS)
    inwin = (c >= cs) & (c < cs + WIN_COLS)
    dc = c - q + (WIN_COLS - 1)
    sel = np.zeros((32, GRID_W, GRID_W), np.float32)
    cc, qq = np.nonzero(inwin)
    sel[dc[cc, qq], cc, qq] = 1.0
    negm = np.where(inwin, 0.0, NEG).astype(np.float32)
    return sel.reshape(32, GRID_W * GRID_W), negm.reshape(1, GRID_W * GRID_W)


def _bias_kernel(rpb_ref, sel_ref, neg_ref, o_ref):
    o_ref[...] = jnp.dot(rpb_ref[...], sel_ref[...], preferred_element_type=F32,
                         precision=lax.Precision.HIGHEST) + neg_ref[...]


def _pair_window(t, n_row_pairs):
    if isinstance(t, int):
        sp = min(max(t - 2, 0), n_row_pairs - WIN_PAIRS)
        var = t if t < 2 else (t - (n_row_pairs - 2) + 3 if t >= n_row_pairs - 2 else 2)
        return sp, var
    sp = jnp.clip(t - 2, 0, n_row_pairs - WIN_PAIRS)
    var = jnp.where(t < 2, t, jnp.where(t >= n_row_pairs - 2, t - (n_row_pairs - 2) + 3, 2))
    return sp, var


N_BIAS_VAR = 5


def _bias_tables(rpb, rows):
    sel, negm = _bias_consts()
    n_dr = 2 * WIN_ROWS - 1
    n_rp = rows // 2
    rpb2 = jnp.pad(rpb.reshape(N_HEADS * n_dr, 2 * WIN_COLS - 1), ((0, 0), (0, 1)))
    bcol = pl.pallas_call(
        _bias_kernel,
        out_shape=jax.ShapeDtypeStruct((N_HEADS * n_dr, GRID_W * GRID_W), F32),
        name="bias_table",
    )(rpb2, jnp.asarray(sel), jnp.asarray(negm))
    bcol = bcol.reshape(N_PAIRS, 2, n_dr, GRID_W, GRID_W)
    masked = jnp.full((N_PAIRS, GRID_W, GRID_W), NEG, F32)
    variants = []
    for t in [0, 1, 2, n_rp - 2, n_rp - 1]:
        sp, _ = _pair_window(t, n_rp)
        win_rows = []
        for w in range(2 * WIN_PAIRS):
            krow = 2 * sp + w
            lanes = []
            for hh in range(2):
                for rr in range(2):
                    r = 2 * t + rr
                    rs = min(max(r - WIN_ROWS // 2, 0), rows - WIN_ROWS)
                    ok = rs <= krow < rs + WIN_ROWS
                    lanes.append(bcol[:, hh, krow - r + WIN_ROWS - 1] if ok else masked)
            win_rows.append(jnp.concatenate(lanes, axis=-1))
        variants.append(jnp.concatenate(win_rows, axis=-2))
    return jnp.stack(variants, axis=1)


def _ctx_kv_kernel(x_ref, sc_ref, sh_ref, g_ref, w_ref, o_ref):
    h = _modulated_norm(x_ref[...], g_ref[...], sc_ref[...], sh_ref[...]).astype(BF16)
    o_ref[...] = jnp.dot(h, w_ref[...], preferred_element_type=F32).astype(BF16)


def _ctx_kv(ctx2, sc_c, sh_c, pre_g, w_in_bf):
    m = ctx2.shape[0]
    return pl.pallas_call(
        _ctx_kv_kernel,
        grid=(2,),
        in_specs=[pl.BlockSpec((m, D_MODEL), lambda j: (0, 0)),
                  pl.BlockSpec((1, D_MODEL), lambda j: (0, 0)),
                  pl.BlockSpec((1, D_MODEL), lambda j: (0, 0)),
                  pl.BlockSpec((1, D_MODEL), lambda j: (0, 0)),
                  pl.BlockSpec((D_MODEL, PART_W), lambda j: (0, PART_K + j))],
        out_specs=pl.BlockSpec((m, PART_W), lambda j: (0, j)),
        out_shape=jax.ShapeDtypeStruct((m, 2 * PART_W), BF16),
        name="ctx_kv",
    )(ctx2, sc_c, sh_c, pre_g, w_in_bf)


def _split_halves_layout(w_part):
    k = w_part.shape[0]
    q4 = HEAD_DIM // 4
    return w_part.reshape(k, N_HEADS, 2, 2, q4).transpose(0, 3, 1, 2, 4).reshape(k, PART_W)


def _inproj_kernel(x_ref, sc_ref, sh_ref, g_ref, w_ref, cos_ref, sin_ref, p_ref, qt_ref, vt_ref, h_ref):
    tm = x_ref.shape[0]
    n_half = PART_W // (2 * LANES)
    h_ref[...] = _modulated_norm(x_ref[...], g_ref[...], sc_ref[0], sh_ref[0]).astype(BF16)

    def slab(r, s):
        return r[:, s * LANES:(s + 1) * LANES]

    def rope(r):
        firsts, seconds = [], []
        for s in range(n_half):
            a, b = slab(r, s), slab(r, n_half + s)
            firsts.append(a * cos_ref[...] - b * sin_ref[...])
            seconds.append(b * cos_ref[...] + a * sin_ref[...])
        return firsts + seconds

    def store_transposed(t_ref, s, u):
        ut = u.astype(BF16).T
        for j in range(tm // LANES):
            t_ref[0, j, s * LANES:(s + 1) * LANES, :] = ut[:, j * LANES:(j + 1) * LANES]

    out_col = 0
    for n in range(N_PARTS_IN):
        r = jnp.dot(h_ref[...], w_ref[:, n * PART_W:(n + 1) * PART_W], preferred_element_type=F32)
        if n == PART_Q:
            for s, u in enumerate(rope(r)):
                store_transposed(qt_ref, s, u * (HEAD_DIM ** -0.5))
            continue
        if n == PART_V:
            for s in range(PART_W // LANES):
                store_transposed(vt_ref, s, slab(r, s))
            continue
        if n == PART_K:
            for s, u in enumerate(rope(r)):
                p_ref[:, out_col * PART_W + s * LANES:out_col * PART_W + (s + 1) * LANES] = u.astype(BF16)
        else:
            p_ref[:, out_col * PART_W:(out_col + 1) * PART_W] = r.astype(BF16)
        out_col += 1


def _inproj(x2, sc, sh, pre_g, w_in_bf, cos, sin, batch, seq, tm):
    m = x2.shape[0]
    tpb = seq // tm
    n_out = N_PARTS_IN - 2
    t_spec = pl.BlockSpec((1, tm // LANES, PART_W, LANES), lambda i: (i // tpb, i % tpb, 0, 0))
    t_shape = jax.ShapeDtypeStruct((batch, seq // LANES, PART_W, LANES), BF16)
    return pl.pallas_call(
        _inproj_kernel,
        grid=(m // tm,),
        in_specs=[pl.BlockSpec((tm, D_MODEL), lambda i: (i, 0)),
                  pl.BlockSpec((1, 1, D_MODEL), lambda i: (i // tpb, 0, 0)),
                  pl.BlockSpec((1, 1, D_MODEL), lambda i: (i // tpb, 0, 0)),
                  pl.BlockSpec((1, D_MODEL), lambda i: (0, 0)),
                  pl.BlockSpec((D_MODEL, N_PARTS_IN * PART_W), lambda i: (0, 0),
                               pipeline_mode=pl.Buffered(1)),
                  pl.BlockSpec((tm, LANES), lambda i: (i % tpb, 0)),
                  pl.BlockSpec((tm, LANES), lambda i: (i % tpb, 0))],
        out_specs=[pl.BlockSpec((tm, n_out * PART_W), lambda i: (i, 0)), t_spec, t_spec],
        out_shape=[jax.ShapeDtypeStruct((m, n_out * PART_W), BF16), t_shape, t_shape],
        scratch_shapes=[pltpu.VMEM((tm, D_MODEL), BF16)],
        compiler_params=pltpu.CompilerParams(dimension_semantics=("arbitrary",)),
        name="in_projection",
    )(x2, sc, sh, pre_g, w_in_bf, cos, sin)


def _attn_kernel(qta_ref, qtb_ref, ka_ref, kb_ref, vt_ref, kca_ref, kcb_ref, vct_ref, bias_ref,
                 o_ref, *stage_refs, n_row_pairs):
    low = lax.broadcasted_iota(jnp.int32, (LANES, LANES), 0) < HEAD_DIM
    nb = WIN_PAIRS * LANES
    last = n_row_pairs - 1
    head_of_dim = (lax.broadcasted_iota(jnp.int32, (2 * LANES, LANES), 0) % LANES) // (HEAD_DIM // 2)
    first_head = 2 * (pl.program_id(0) % 2)
    sel0 = head_of_dim == first_head
    sel1 = head_of_dim == first_head + 1

    def scores(t, slot):
        sp, var = _pair_window(t, n_row_pairs)
        qt = jnp.concatenate([qta_ref[0, t], qtb_ref[0, t]], axis=0)
        zero = jnp.zeros_like(qt)
        rhs = jnp.concatenate([jnp.where(sel0, qt, zero), jnp.where(sel1, qt, zero)], axis=1)
        rows = pl.ds(pl.multiple_of(sp * LANES, LANES), nb)
        kwin = jnp.concatenate([ka_ref[rows, :], kb_ref[rows, :]], axis=1)
        s_refs[slot][:nb, :] = jnp.dot(kwin, rhs, preferred_element_type=F32) + bias_ref[0, var]
        yield
        kc = jnp.concatenate([kca_ref[0], kcb_ref[0]], axis=1)
        s_refs[slot][nb:, :] = jnp.dot(kc, rhs, preferred_element_type=F32)

    def softmax(slot):
        s_ref, p_ref = s_refs[slot], p_refs[slot]
        chunks = [slice(c * RED_ROWS, (c + 1) * RED_ROWS) for c in range(s_ref.shape[0] // RED_ROWS)]
        m = s_ref[chunks[0], :]
        for c in chunks[1:]:
            m = jnp.maximum(m, s_ref[c, :])
        m = jnp.max(m, axis=0, keepdims=True)
        yield
        acc = jnp.zeros((RED_ROWS, 2 * LANES), F32)
        for k, c in enumerate(chunks):
            p = jnp.exp(s_ref[c, :] - m)
            acc = acc + p
            p_ref[c, :] = p.astype(BF16)
            if k % 4 == 3:
                yield
        l_refs[slot][...] = 1.0 / jnp.sum(acc, axis=0, keepdims=True)

    def values(t, slot):
        sp, _ = _pair_window(t, n_row_pairs)
        vfull = jnp.concatenate([vt_ref[0, sp + g] for g in range(WIN_PAIRS)] + [vct_ref[0]], axis=1)
        ot = jnp.dot(vfull, p_refs[slot][...], preferred_element_type=F32)
        yield
        ot = ot * l_refs[slot][...]
        out = jnp.where(low, ot[:, :LANES], ot[:, LANES:]).T
        o_ref[pl.ds(pl.multiple_of(t * LANES, LANES), LANES), :] = out.astype(BF16)

    def run(*stages):
        for stage in stages:
            for _ in stage:
                pass

    def interleave(sm, va, sc):
        next(sm)
        next(va)
        next(sm)
        next(sc)
        next(sm)
        run(sc)
        next(sm)
        run(va, sm)

    g = ATTN_GROUP
    s_refs, p_refs, l_refs = (stage_refs[k * 2 * g:(k + 1) * 2 * g] for k in range(3))
    for j in range(g):
        run(scores(j, j))
    for slot in range(g, 2 * g):
        p_refs[slot][...] = jnp.zeros(p_refs[slot].shape, BF16)
        l_refs[slot][...] = jnp.zeros(l_refs[slot].shape, F32)

    def group(t, cur, nxt):
        for j in range(g):
            interleave(softmax(cur + j), values(jnp.maximum(t + j - g, 0), nxt + j),
                       scores(jnp.minimum(t + j + g, last), nxt + j))

    def body(i, carry):
        t = 2 * g * i
        group(t, 0, g)
        group(t + g, g, 0)
        return carry

    lax.fori_loop(0, n_row_pairs // (2 * g), body, 0)
    for j in range(g):
        run(values(n_row_pairs - g + j, g + j))


def _attention(p8, qt, vt, kc, vct, bias, batch, seq):
    n_ctx = kc.shape[1]
    n_keys = WIN_PAIRS * LANES + n_ctx
    n_slots = 2 * ATTN_GROUP
    assert (seq // LANES) % n_slots == 0
    spp = PART_W // LANES
    half = spp // 2

    def t_spec(slab):
        return pl.BlockSpec((1, seq // LANES, LANES, LANES), lambda hp, b: (b, 0, slab(hp), 0))

    return pl.pallas_call(
        functools.partial(_attn_kernel, n_row_pairs=seq // LANES),
        grid=(N_PAIRS, batch),
        in_specs=[t_spec(lambda hp: hp // 2), t_spec(lambda hp: half + hp // 2),
                  pl.BlockSpec((seq, LANES), lambda hp, b: (b, OUT_K * spp + hp // 2)),
                  pl.BlockSpec((seq, LANES), lambda hp, b: (b, OUT_K * spp + half + hp // 2)),
                  t_spec(lambda hp: hp),
                  pl.BlockSpec((1, n_ctx, LANES), lambda hp, b: (b, 0, hp // 2)),
                  pl.BlockSpec((1, n_ctx, LANES), lambda hp, b: (b, 0, half + hp // 2)),
                  pl.BlockSpec((1, LANES, n_ctx), lambda hp, b: (b, hp, 0)),
                  pl.BlockSpec((1, N_BIAS_VAR, WIN_PAIRS * LANES, 2 * LANES),
                               lambda hp, b: (hp, 0, 0, 0))],
        out_specs=pl.BlockSpec((seq, LANES), lambda hp, b: (b, hp)),
        out_shape=jax.ShapeDtypeStruct((batch * seq, N_HEADS * HEAD_DIM), BF16),
        scratch_shapes=[pltpu.VMEM((n_keys, 2 * LANES), F32)] * n_slots
        + [pltpu.VMEM((n_keys, 2 * LANES), BF16)] * n_slots + [pltpu.VMEM((1, 2 * LANES), F32)] * n_slots,
        compiler_params=pltpu.CompilerParams(dimension_semantics=("arbitrary", "arbitrary")),
        name="attention",
    )(qt, qt, p8, p8, vt, kc, kc, vct, bias)


def _mixer_kernel(bg_ref, cg_ref, xi_ref, za_ref, zb_ref, ga_ref, gb_ref, at_ref,
                  cgp_ref, xip_ref, cgn_ref, xin_ref, x_ref, gt_ref, cw_ref, cb_ref, pg_ref,
                  woc_ref, woa_ref, wo_ref, o_ref, *, tiles_per_batch, halo):
    t = pl.program_id(0) % tiles_per_batch
    u = cg_ref[...].astype(F32) * xi_ref[...].astype(F32)
    tm = u.shape[0]
    has_prev = jnp.where(t > 0, 1.0, 0.0)
    has_next = jnp.where(t < tiles_per_batch - 1, 1.0, 0.0)
    prev_row = cgp_ref[halo - 1:halo, :].astype(F32) * xip_ref[halo - 1:halo, :].astype(F32) * has_prev
    next_row = cgn_ref[0:1, :].astype(F32) * xin_ref[0:1, :].astype(F32) * has_next
    row = lax.broadcasted_iota(jnp.int32, u.shape, 0)
    u_prev = jnp.where(row == 0, prev_row, pltpu.roll(u, 1, 0))
    u_next = jnp.where(row == tm - 1, next_row, pltpu.roll(u, tm - 1, 0))
    conv = u_prev * cw_ref[0:1, :] + u * cw_ref[1:2, :] + u_next * cw_ref[2:3, :] + cb_ref[...]
    za = za_ref[...].astype(F32)
    a = (za * jax.nn.sigmoid(za)) * bg_ref[...].astype(F32) * conv
    y_a = jnp.dot(a.astype(BF16), woc_ref[...], preferred_element_type=F32)
    zb = zb_ref[...].astype(F32)
    bb = (zb * jax.nn.sigmoid(zb)) * at_ref[...].astype(F32)
    y_b = jnp.dot(bb.astype(BF16), woa_ref[...], preferred_element_type=F32)
    merged = (jax.nn.sigmoid(ga_ref[...].astype(F32)) * y_a
              + jax.nn.sigmoid(gb_ref[...].astype(F32)) * y_b)
    y = jnp.dot(merged.astype(BF16), wo_ref[...], preferred_element_type=F32)
    ms = jnp.mean(y * y, axis=-1, keepdims=True)
    o_ref[...] = x_ref[...] + gt_ref[0] * ((y * lax.rsqrt(ms + EPS)) * pg_ref[...])


def _mixer(p8, attn, x2, gt, conv_w, conv_b, post_g, woc, woa, wo, seq, tm):
    m = x2.shape[0]
    tpb = seq // tm
    halo = 16
    hb = tm // halo
    nhb = m // halo

    def part(k):
        return pl.BlockSpec((tm, PART_W), lambda i: (i, k))

    def prev(k):
        return pl.BlockSpec((halo, PART_W), lambda i: (jnp.maximum(i * hb - 1, 0), k))

    def nxt(k):
        return pl.BlockSpec((halo, PART_W), lambda i: (jnp.minimum((i + 1) * hb, nhb - 1), k))

    def full(shape):
        return pl.BlockSpec(shape, lambda i: (0,) * len(shape))

    return pl.pallas_call(
        functools.partial(_mixer_kernel, tiles_per_batch=tpb, halo=halo),
        grid=(m // tm,),
        in_specs=[part(OUT_BG), part(OUT_CG), part(OUT_XI), part(OUT_ZA), part(OUT_ZB),
                  part(OUT_GA), part(OUT_GB),
                  pl.BlockSpec((tm, PART_W), lambda i: (i, 0)),
                  prev(OUT_CG), prev(OUT_XI), nxt(OUT_CG), nxt(OUT_XI),
                  pl.BlockSpec((tm, D_MODEL), lambda i: (i, 0)),
                  pl.BlockSpec((1, 1, D_MODEL), lambda i: (i // tpb, 0, 0)),
                  full((3, PART_W)), full((1, PART_W)), full((1, D_MODEL)),
                  full((PART_W, D_MODEL)), full((PART_W, D_MODEL)), full((D_MODEL, D_MODEL))],
        out_specs=pl.BlockSpec((tm, D_MODEL), lambda i: (i, 0)),
        out_shape=jax.ShapeDtypeStruct((m, D_MODEL), F32),
        compiler_params=pltpu.CompilerParams(dimension_semantics=("arbitrary",)),
        name="mixer_out",
    )(p8, p8, p8, p8, p8, p8, p8, attn, p8, p8, p8, p8, x2, gt,
      conv_w, conv_b, post_g, woc, woa, wo)


def _rope_tables(seq):
    t = jnp.arange(seq, dtype=jnp.int32)
    row = (t // GRID_W).astype(F32)
    col = (t % GRID_W).astype(F32)
    half = HEAD_DIM // 2
    inv = ROPE_BASE ** (-jnp.arange(0, half, 2, dtype=F32) / half)
    ang_r = row[:, None] * inv
    ang_c = col[:, None] * inv
    ang = jnp.tile(jnp.concatenate([ang_r, ang_c], axis=-1), (1, LANES // (HEAD_DIM // 2)))
    return jnp.cos(ang), jnp.sin(ang)


def _layer(x, c, ctx, c_ctx, w_mod, b_mod, pre_g, post_g, w_in, conv_w, conv_b, rpb,
           w_out_conv, w_out_attn, w_o):
    batch, seq, _ = x.shape
    n_ctx = ctx.shape[1]
    rows = seq // GRID_W
    assert rows >= 2 * WIN_PAIRS and seq % LANES == 0 and batch <= 7

    cond8 = jnp.zeros((8, D_MODEL), F32).at[:batch].set(c).at[batch].set(c_ctx)
    mod = _modulation(cond8, w_mod, b_mod.reshape(1, -1))
    sh, sc, gt = (mod[:batch, k * D_MODEL:(k + 1) * D_MODEL].reshape(batch, 1, D_MODEL) for k in range(3))
    sh_c, sc_c = (mod[batch:batch + 1, k * D_MODEL:(k + 1) * D_MODEL] for k in range(2))

    pre_g2 = pre_g.reshape(1, D_MODEL)
    def part(n):
        return w_in[:, n * PART_W:(n + 1) * PART_W]

    w_in_bf = jnp.concatenate(
        [w_in[:, :PART_Q * PART_W], _split_halves_layout(part(PART_Q)), _split_halves_layout(part(PART_K)),
         w_in[:, PART_V * PART_W:]], axis=1).astype(BF16)
    kvc = _ctx_kv(ctx.reshape(batch * n_ctx, D_MODEL), sc_c, sh_c, pre_g2, w_in_bf)
    kc = kvc[:, :PART_W].reshape(batch, n_ctx, PART_W)
    vct = jnp.transpose(kvc[:, PART_W:].reshape(batch, n_ctx, PART_W), (0, 2, 1))

    x2 = x.reshape(batch * seq, D_MODEL)
    cos, sin = _rope_tables(seq)
    tm_in = min(512, seq)
    p8, qt, vt = _inproj(x2, sc, sh, pre_g2, w_in_bf, cos, sin, batch, seq, tm_in)

    bias = _bias_tables(rpb, rows)
    attn = _attention(p8, qt, vt, kc, vct, bias, batch, seq)

    tm_out = min(512, seq)
    out = _mixer(p8, attn, x2, gt, conv_w, conv_b.reshape(1, -1), post_g.reshape(1, -1),
                 w_out_conv.astype(BF16), w_out_attn.astype(BF16), w_o.astype(BF16), seq, tm_out)
    return out.reshape(batch, seq, D_MODEL)


def kernel(x, c, ctx, c_ctx, w_mod, b_mod, pre_g, post_g, w_in, conv_w, conv_b, rpb,
           w_out_conv, w_out_attn, w_o):
    depth = w_mod.shape[0]
    assert depth == 1, "context stream update between layers is not implemented"
    return _layer(x, c, ctx, c_ctx, w_mod[0], b_mod[0], pre_g[0], post_g[0], w_in[0], conv_w[0],
                  conv_b[0], rpb[0], w_out_conv[0], w_out_attn[0], w_o[0])
```

```python
import functools

import numpy as np
import jax
import jax.numpy as jnp
from jax import lax
from jax.experimental import pallas as pl
from jax.experimental.pallas import tpu as pltpu

D_MODEL = 1024
GRID_W = 64
N_HEADS = 16
HEAD_DIM = 64
WIN_ROWS = 8
WIN_COLS = 16
ROPE_BASE = 10000.0
EPS = 1e-6
PART_W = 1024
N_PARTS_IN = 10
PART_Q, PART_K, PART_V = 4, 5, 6
OUT_BG, OUT_CG, OUT_XI, OUT_ZA, OUT_K, OUT_ZB, OUT_GA, OUT_GB = range(8)
LANES = 128
N_PAIRS = N_HEADS // 2
WIN_PAIRS = 5
NEG = -1e30
RED_ROWS = 64
ATTN_GROUP = 4
SUM_ROWS = 16
LOG2E = 1.4426950408889634
BF16 = jnp.bfloat16
F32 = jnp.float32


def _modulated_norm(x, g, sc, sh):
    ms = jnp.mean(x * x, axis=-1, keepdims=True)
    return x * lax.rsqrt(ms + EPS) * (g * (1.0 + sc)) + sh


def _modulation_kernel(c_ref, w_ref, b_ref, o_ref):
    c = c_ref[...]
    o_ref[...] = jnp.dot(c * jax.nn.sigmoid(c), w_ref[...], preferred_element_type=F32,
                         precision=lax.Precision.HIGHEST) + b_ref[...]


def _modulation(cond8, w_mod, b_mod):
    n = w_mod.shape[1]
    tn = 768
    return pl.pallas_call(
        _modulation_kernel,
        grid=(n // tn,),
        in_specs=[pl.BlockSpec((8, D_MODEL), lambda j: (0, 0)),
                  pl.BlockSpec((D_MODEL, tn), lambda j: (0, j)),
                  pl.BlockSpec((1, tn), lambda j: (0, j))],
        out_specs=pl.BlockSpec((8, tn), lambda j: (0, j)),
        out_shape=jax.ShapeDtypeStruct((8, n), F32),
        name="modulation",
    )(cond8, w_mod, b_mod)


def _bias_consts():
    q = np.arange(GRID_W)[None, :]
    c = np.arange(GRID_W)[:, None]
    cs = np.clip(q - WIN_COLS // 2, 0, GRID_W - WIN_COLS)
    inwin = (c >= cs) & (c < cs + WIN_COLS)
    dc = c - q + (WIN_COLS - 1)
    sel = np.zeros((32, GRID_W, GRID_W), np.float32)
    cc, qq = np.nonzero(inwin)
    sel[dc[cc, qq], cc, qq] = 1.0
    negm = np.where(inwin, 0.0, NEG).astype(np.float32)
    return sel.reshape(32, GRID_W * GRID_W), negm.reshape(1, GRID_W * GRID_W)


def _bias_kernel(rpb_ref, sel_ref, neg_ref, o_ref):
    o_ref[...] = (jnp.dot(rpb_ref[...], sel_ref[...], preferred_element_type=F32,
                          precision=lax.Precision.HIGHEST) + neg_ref[...]) * LOG2E


def _pair_window(t, n_row_pairs):
    if isinstance(t, int):
        sp = min(max(t - 2, 0), n_row_pairs - WIN_PAIRS)
        var = t if t < 2 else (t - (n_row_pairs - 2) + 3 if t >= n_row_pairs - 2 else 2)
        return sp, var
    sp = jnp.clip(t - 2, 0, n_row_pairs - WIN_PAIRS)
    var = jnp.where(t < 2, t, jnp.where(t >= n_row_pairs - 2, t - (n_row_pairs - 2) + 3, 2))
    return sp, var


N_BIAS_VAR = 5


def _bias_tables(rpb, rows):
    sel, negm = _bias_consts()
    n_dr = 2 * WIN_ROWS - 1
    n_rp = rows // 2
    rpb2 = jnp.pad(rpb.reshape(N_HEADS * n_dr, 2 * WIN_COLS - 1), ((0, 0), (0, 1)))
    bcol = pl.pallas_call(
        _bias_kernel,
        out_shape=jax.ShapeDtypeStruct((N_HEADS * n_dr, GRID_W * GRID_W), F32),
        name="bias_table",
    )(rpb2, jnp.asarray(sel), jnp.asarray(negm))
    bcol = jnp.concatenate([bcol.reshape(N_PAIRS, 2 * n_dr, GRID_W, GRID_W),
                            jnp.full((N_PAIRS, 1, GRID_W, GRID_W), NEG, F32)], axis=1)
    idx = np.full((N_BIAS_VAR, 2 * WIN_PAIRS, 2, 2), 2 * n_dr, np.int32)
    for v, t in enumerate([0, 1, 2, n_rp - 2, n_rp - 1]):
        sp, _ = _pair_window(t, n_rp)
        for w in range(2 * WIN_PAIRS):
            krow = 2 * sp + w
            for rr in range(2):
                r = 2 * t + rr
                rs = min(max(r - WIN_ROWS // 2, 0), rows - WIN_ROWS)
                if rs <= krow < rs + WIN_ROWS:
                    idx[v, w, :, rr] = np.arange(2) * n_dr + (krow - r + WIN_ROWS - 1)
    t = jnp.take(bcol, jnp.asarray(idx.reshape(-1)), axis=1)
    t = t.reshape(N_PAIRS, N_BIAS_VAR, 2 * WIN_PAIRS, 2, 2, GRID_W, GRID_W)
    return t.transpose(0, 1, 2, 5, 3, 4, 6).reshape(
        N_PAIRS, N_BIAS_VAR, 2 * WIN_PAIRS * GRID_W, 4 * GRID_W)


def _ctx_kv_kernel(x_ref, sc_ref, sh_ref, g_ref, wk_ref, wv_ref, k_ref, v_ref):
    h = _modulated_norm(x_ref[...], g_ref[...], sc_ref[...], sh_ref[...]).astype(BF16)
    k_ref[...] = jnp.dot(h, wk_ref[...], preferred_element_type=F32).astype(BF16)
    v_ref[...] = jnp.dot(h, wv_ref[...], preferred_element_type=F32).astype(BF16)


def _ctx_kv(ctx2, sc_c, sh_c, pre_g, wk_bf, w_in_bf):
    m = ctx2.shape[0]

    def full(shape):
        return pl.BlockSpec(shape, lambda j: (0,) * len(shape))

    return pl.pallas_call(
        _ctx_kv_kernel,
        grid=(1,),
        in_specs=[full((m, D_MODEL)), full((1, D_MODEL)), full((1, D_MODEL)), full((1, D_MODEL)),
                  full((D_MODEL, PART_W)),
                  pl.BlockSpec((D_MODEL, PART_W), lambda j: (0, PART_V))],
        out_specs=[full((m, PART_W)), full((m, PART_W))],
        out_shape=[jax.ShapeDtypeStruct((m, PART_W), BF16)] * 2,
        name="ctx_kv",
    )(ctx2, sc_c, sh_c, pre_g, wk_bf, w_in_bf)


def _split_halves_layout(w_part):
    k = w_part.shape[0]
    q4 = HEAD_DIM // 4
    return w_part.reshape(k, N_HEADS, 2, 2, q4).transpose(0, 3, 1, 2, 4).reshape(k, PART_W)


def _inproj_kernel(x_ref, sc_ref, sh_ref, g_ref, w_ref, wq_ref, wk_ref, cos_ref, sin_ref,
                   p_ref, qt_ref, vt_ref, h_ref):
    tm = x_ref.shape[0]
    n_half = PART_W // (2 * LANES)
    h_ref[...] = _modulated_norm(x_ref[...], g_ref[...], sc_ref[0], sh_ref[0]).astype(BF16)

    def slab(r, s):
        return r[:, s * LANES:(s + 1) * LANES]

    def rope(r):
        firsts, seconds = [], []
        for s in range(n_half):
            a, b = slab(r, s), slab(r, n_half + s)
            firsts.append(a * cos_ref[...] - b * sin_ref[...])
            seconds.append(b * cos_ref[...] + a * sin_ref[...])
        return firsts + seconds

    def store_transposed(t_ref, s, u):
        ut = u.astype(BF16).T
        for j in range(tm // LANES):
            t_ref[0, j, s * LANES:(s + 1) * LANES, :] = ut[:, j * LANES:(j + 1) * LANES]

    out_col = 0
    for n in range(N_PARTS_IN):
        if n == PART_Q:
            w = wq_ref[...]
        elif n == PART_K:
            w = wk_ref[...]
        else:
            w = w_ref[:, n * PART_W:(n + 1) * PART_W]
        r = jnp.dot(h_ref[...], w, preferred_element_type=F32)
        if n == PART_Q:
            for s, u in enumerate(rope(r)):
                store_transposed(qt_ref, s, u * (HEAD_DIM ** -0.5 * LOG2E))
            continue
        if n == PART_V:
            for s in range(PART_W // LANES):
                store_transposed(vt_ref, s, slab(r, s))
            continue
        if n == PART_K:
            for s, u in enumerate(rope(r)):
                p_ref[:, out_col * PART_W + s * LANES:out_col * PART_W + (s + 1) * LANES] = u.astype(BF16)
        else:
            p_ref[:, out_col * PART_W:(out_col + 1) * PART_W] = r.astype(BF16)
        out_col += 1


def _inproj(x2, sc, sh, pre_g, w_in_bf, wq_bf, wk_bf, cos, sin, batch, seq, tm):
    m = x2.shape[0]
    tpb = seq // tm
    n_out = N_PARTS_IN - 2
    t_spec = pl.BlockSpec((1, tm // LANES, PART_W, LANES), lambda i: (i // tpb, i % tpb, 0, 0))
    t_shape = jax.ShapeDtypeStruct((batch, seq // LANES, PART_W, LANES), BF16)

    def resident(shape):
        return pl.BlockSpec(shape, lambda i: (0, 0), pipeline_mode=pl.Buffered(1))

    return pl.pallas_call(
        _inproj_kernel,
        grid=(m // tm,),
        in_specs=[pl.BlockSpec((tm, D_MODEL), lambda i: (i, 0)),
                  pl.BlockSpec((1, 1, D_MODEL), lambda i: (i // tpb, 0, 0)),
                  pl.BlockSpec((1, 1, D_MODEL), lambda i: (i // tpb, 0, 0)),
                  pl.BlockSpec((1, D_MODEL), lambda i: (0, 0)),
                  resident((D_MODEL, N_PARTS_IN * PART_W)),
                  resident((D_MODEL, PART_W)), resident((D_MODEL, PART_W)),
                  pl.BlockSpec((tm, LANES), lambda i: (i % tpb, 0)),
                  pl.BlockSpec((tm, LANES), lambda i: (i % tpb, 0))],
        out_specs=[pl.BlockSpec((tm, n_out * PART_W), lambda i: (i, 0)), t_spec, t_spec],
        out_shape=[jax.ShapeDtypeStruct((m, n_out * PART_W), BF16), t_shape, t_shape],
        scratch_shapes=[pltpu.VMEM((tm, D_MODEL), BF16)],
        compiler_params=pltpu.CompilerParams(dimension_semantics=("arbitrary",)),
        name="in_projection",
    )(x2, sc, sh, pre_g, w_in_bf, wq_bf, wk_bf, cos, sin)


def _attn_kernel(qta_ref, qtb_ref, ka_ref, kb_ref, vt_ref, kca_ref, kcb_ref, vct_ref, bias_ref,
                 o_ref, *stage_refs, n_row_pairs):
    low = lax.broadcasted_iota(jnp.int32, (LANES, LANES), 0) < HEAD_DIM
    nb = WIN_PAIRS * LANES
    last = n_row_pairs - 1
    head_of_dim = (lax.broadcasted_iota(jnp.int32, (2 * LANES, LANES), 0) % LANES) // (HEAD_DIM // 2)
    first_head = 2 * (pl.program_id(0) % 2)
    sel0 = head_of_dim == first_head
    sel1 = head_of_dim == first_head + 1

    def scores(t, slot):
        sp, var = _pair_window(t, n_row_pairs)
        qt = jnp.concatenate([qta_ref[0, t], qtb_ref[0, t]], axis=0)
        zero = jnp.zeros_like(qt)
        rhs = jnp.concatenate([jnp.where(sel0, qt, zero), jnp.where(sel1, qt, zero)], axis=1)
        rows = pl.ds(pl.multiple_of(sp * LANES, LANES), nb)
        kwin = jnp.concatenate([ka_ref[rows, :], kb_ref[rows, :]], axis=1)
        s_refs[slot][:nb, :] = jnp.dot(kwin, rhs, preferred_element_type=F32) + bias_ref[0, var]
        yield
        kc = jnp.concatenate([kca_ref[0], kcb_ref[0]], axis=1)
        s_refs[slot][nb:, :] = jnp.dot(kc, rhs, preferred_element_type=F32)

    def softmax(slot):
        s_ref, p_ref = s_refs[slot], p_refs[slot]
        chunks = [slice(c * RED_ROWS, (c + 1) * RED_ROWS) for c in range(s_ref.shape[0] // RED_ROWS)]
        m = s_ref[chunks[0], :]
        for c in chunks[1:]:
            m = jnp.maximum(m, s_ref[c, :])
        m = jnp.max(m, axis=0, keepdims=True)
        yield
        for k, c in enumerate(chunks):
            p_ref[c, :] = jnp.exp2(s_ref[c, :] - m).astype(BF16)
            if k % 4 == 3:
                yield

    def values(t, slot):
        sp, _ = _pair_window(t, n_row_pairs)
        p = p_refs[slot][...]
        vext = jnp.concatenate([vt_ref[0, sp + g] for g in range(WIN_PAIRS)] + [vct_ref[0]], axis=1)
        vext = jnp.concatenate([vext, jnp.ones((SUM_ROWS, p.shape[0]), BF16)], axis=0)
        ot = jnp.dot(vext, p, preferred_element_type=F32)
        yield
        ot = ot[:LANES] * (1.0 / ot[LANES:LANES + 1])
        out = jnp.where(low, ot[:, :LANES], ot[:, LANES:]).T
        o_ref[pl.ds(pl.multiple_of(t * LANES, LANES), LANES), :] = out.astype(BF16)

    def run(*stages):
        for stage in stages:
            for _ in stage:
                pass

    def interleave(sm, va, sc):
        next(sm)
        next(va)
        next(sm)
        next(sc)
        next(sm)
        run(sc)
        next(sm)
        run(va, sm)

    g = ATTN_GROUP
    s_refs, p_refs = stage_refs[:2 * g], stage_refs[2 * g:]
    for j in range(g):
        run(scores(j, j))
    for slot in range(g, 2 * g):
        p_refs[slot][...] = jnp.ones(p_refs[slot].shape, BF16)

    def group(t, cur, nxt):
        for j in range(g):
            interleave(softmax(cur + j), values(jnp.maximum(t + j - g, 0), nxt + j),
                       scores(jnp.minimum(t + j + g, last), nxt + j))

    def body(i, carry):
        t = 2 * g * i
        group(t, 0, g)
        group(t + g, g, 0)
        return carry

    lax.fori_loop(0, n_row_pairs // (2 * g), body, 0)
    for j in range(g):
        run(values(n_row_pairs - g + j, g + j))


def _attention(p8, qt, vt, kc, vct, bias, batch, seq):
    n_ctx = kc.shape[1]
    n_keys = WIN_PAIRS * LANES + n_ctx
    n_slots = 2 * ATTN_GROUP
    assert (seq // LANES) % n_slots == 0
    spp = PART_W // LANES
    half = spp // 2

    def t_spec(slab):
        return pl.BlockSpec((1, seq // LANES, LANES, LANES), lambda hp, b: (b, 0, slab(hp), 0))

    return pl.pallas_call(
        functools.partial(_attn_kernel, n_row_pairs=seq // LANES),
        grid=(N_PAIRS, batch),
        in_specs=[t_spec(lambda hp: hp // 2), t_spec(lambda hp: half + hp // 2),
                  pl.BlockSpec((seq, LANES), lambda hp, b: (b, OUT_K * spp + hp // 2)),
                  pl.BlockSpec((seq, LANES), lambda hp, b: (b, OUT_K * spp + half + hp // 2)),
                  t_spec(lambda hp: hp),
                  pl.BlockSpec((1, n_ctx, LANES), lambda hp, b: (b, 0, hp // 2)),
                  pl.BlockSpec((1, n_ctx, LANES), lambda hp, b: (b, 0, half + hp // 2)),
                  pl.BlockSpec((1, LANES, n_ctx), lambda hp, b: (b, hp, 0)),
                  pl.BlockSpec((1, N_BIAS_VAR, WIN_PAIRS * LANES, 2 * LANES),
                               lambda hp, b: (hp, 0, 0, 0))],
        out_specs=pl.BlockSpec((seq, LANES), lambda hp, b: (b, hp)),
        out_shape=jax.ShapeDtypeStruct((batch * seq, N_HEADS * HEAD_DIM), BF16),
        scratch_shapes=[pltpu.VMEM((n_keys, 2 * LANES), F32)] * n_slots
        + [pltpu.VMEM((n_keys, 2 * LANES), BF16)] * n_slots,
        compiler_params=pltpu.CompilerParams(dimension_semantics=("arbitrary", "arbitrary")),
        name="attention",
    )(qt, qt, p8, p8, vt, kc, kc, vct, bias)


def _mixer_kernel(bg_ref, cg_ref, xi_ref, za_ref, zb_ref, ga_ref, gb_ref, at_ref,
                  cgp_ref, xip_ref, cgn_ref, xin_ref, x_ref, gt_ref, cw_ref, cb_ref, pg_ref,
                  woc_ref, woa_ref, wo_ref, o_ref, *, tiles_per_batch, halo):
    t = pl.program_id(0) % tiles_per_batch
    u = cg_ref[...].astype(F32) * xi_ref[...].astype(F32)
    tm = u.shape[0]
    has_prev = jnp.where(t > 0, 1.0, 0.0)
    has_next = jnp.where(t < tiles_per_batch - 1, 1.0, 0.0)
    prev_row = cgp_ref[halo - 1:halo, :].astype(F32) * xip_ref[halo - 1:halo, :].astype(F32) * has_prev
    next_row = cgn_ref[0:1, :].astype(F32) * xin_ref[0:1, :].astype(F32) * has_next
    row = lax.broadcasted_iota(jnp.int32, u.shape, 0)
    u_prev = jnp.where(row == 0, prev_row, pltpu.roll(u, 1, 0))
    u_next = jnp.where(row == tm - 1, next_row, pltpu.roll(u, tm - 1, 0))
    conv = u_prev * cw_ref[0:1, :] + u * cw_ref[1:2, :] + u_next * cw_ref[2:3, :] + cb_ref[...]
    za = za_ref[...].astype(F32)
    a = (za * jax.nn.sigmoid(za)) * bg_ref[...].astype(F32) * conv
    y_a = jnp.dot(a.astype(BF16), woc_ref[...], preferred_element_type=F32)
    zb = zb_ref[...].astype(F32)
    bb = (zb * jax.nn.sigmoid(zb)) * at_ref[...].astype(F32)
    y_b = jnp.dot(bb.astype(BF16), woa_ref[...], preferred_element_type=F32)
    merged = (jax.nn.sigmoid(ga_ref[...].astype(F32)) * y_a
              + jax.nn.sigmoid(gb_ref[...].astype(F32)) * y_b)
    y = jnp.dot(merged.astype(BF16), wo_ref[...], preferred_element_type=F32)
    ms = jnp.mean(y * y, axis=-1, keepdims=True)
    o_ref[...] = x_ref[...] + gt_ref[0] * ((y * lax.rsqrt(ms + EPS)) * pg_ref[...])


def _mixer(p8, attn, x2, gt, conv_w, conv_b, post_g, woc, woa, wo, seq, tm):
    m = x2.shape[0]
    tpb = seq // tm
    halo = 16
    hb = tm // halo
    nhb = m // halo

    def part(k):
        return pl.BlockSpec((tm, PART_W), lambda i: (i, k))

    def prev(k):
        return pl.BlockSpec((halo, PART_W), lambda i: (jnp.maximum(i * hb - 1, 0), k))

    def nxt(k):
        return pl.BlockSpec((halo, PART_W), lambda i: (jnp.minimum((i + 1) * hb, nhb - 1), k))

    def full(shape):
        return pl.BlockSpec(shape, lambda i: (0,) * len(shape))

    return pl.pallas_call(
        functools.partial(_mixer_kernel, tiles_per_batch=tpb, halo=halo),
        grid=(m // tm,),
        in_specs=[part(OUT_BG), part(OUT_CG), part(OUT_XI), part(OUT_ZA), part(OUT_ZB),
                  part(OUT_GA), part(OUT_GB),
                  pl.BlockSpec((tm, PART_W), lambda i: (i, 0)),
                  prev(OUT_CG), prev(OUT_XI), nxt(OUT_CG), nxt(OUT_XI),
                  pl.BlockSpec((tm, D_MODEL), lambda i: (i, 0)),
                  pl.BlockSpec((1, 1, D_MODEL), lambda i: (i // tpb, 0, 0)),
                  full((3, PART_W)), full((1, PART_W)), full((1, D_MODEL)),
                  full((PART_W, D_MODEL)), full((PART_W, D_MODEL)), full((D_MODEL, D_MODEL))],
        out_specs=pl.BlockSpec((tm, D_MODEL), lambda i: (i, 0)),
        out_shape=jax.ShapeDtypeStruct((m, D_MODEL), F32),
        compiler_params=pltpu.CompilerParams(dimension_semantics=("arbitrary",)),
        name="mixer_out",
    )(p8, p8, p8, p8, p8, p8, p8, attn, p8, p8, p8, p8, x2, gt,
      conv_w, conv_b, post_g, woc, woa, wo)


def _rope_tables(seq):
    t = jnp.arange(seq, dtype=jnp.int32)
    row = (t // GRID_W).astype(F32)
    col = (t % GRID_W).astype(F32)
    half = HEAD_DIM // 2
    inv = ROPE_BASE ** (-jnp.arange(0, half, 2, dtype=F32) / half)
    ang_r = row[:, None] * inv
    ang_c = col[:, None] * inv
    ang = jnp.tile(jnp.concatenate([ang_r, ang_c], axis=-1), (1, LANES // (HEAD_DIM // 2)))
    return jnp.cos(ang), jnp.sin(ang)


def _layer(x, c, ctx, c_ctx, w_mod, b_mod, pre_g, post_g, w_in, conv_w, conv_b, rpb,
           w_out_conv, w_out_attn, w_o):
    batch, seq, _ = x.shape
    n_ctx = ctx.shape[1]
    rows = seq // GRID_W
    assert rows >= 2 * WIN_PAIRS and seq % LANES == 0 and batch <= 7

    cond8 = jnp.zeros((8, D_MODEL), F32).at[:batch].set(c).at[batch].set(c_ctx)
    mod = _modulation(cond8, w_mod, b_mod.reshape(1, -1))
    sh, sc, gt = (mod[:batch, k * D_MODEL:(k + 1) * D_MODEL].reshape(batch, 1, D_MODEL) for k in range(3))
    sh_c, sc_c = (mod[batch:batch + 1, k * D_MODEL:(k + 1) * D_MODEL] for k in range(2))

    pre_g2 = pre_g.reshape(1, D_MODEL)
    w_in_bf = w_in.astype(BF16)
    wq_bf = _split_halves_layout(w_in_bf[:, PART_Q * PART_W:(PART_Q + 1) * PART_W])
    wk_bf = _split_halves_layout(w_in_bf[:, PART_K * PART_W:(PART_K + 1) * PART_W])
    kc, vc = _ctx_kv(ctx.reshape(batch * n_ctx, D_MODEL), sc_c, sh_c, pre_g2, wk_bf, w_in_bf)
    kc = kc.reshape(batch, n_ctx, PART_W)
    vct = jnp.transpose(vc.reshape(batch, n_ctx, PART_W), (0, 2, 1))

    x2 = x.reshape(batch * seq, D_MODEL)
    cos, sin = _rope_tables(seq)
    tm_in = min(512, seq)
    p8, qt, vt = _inproj(x2, sc, sh, pre_g2, w_in_bf, wq_bf, wk_bf, cos, sin, batch, seq, tm_in)

    bias = _bias_tables(rpb, rows)
    attn = _attention(p8, qt, vt, kc, vct, bias, batch, seq)

    tm_out = min(512, seq)
    out = _mixer(p8, attn, x2, gt, conv_w, conv_b.reshape(1, -1), post_g.reshape(1, -1),
                 w_out_conv.astype(BF16), w_out_attn.astype(BF16), w_o.astype(BF16), seq, tm_out)
    return out.reshape(batch, seq, D_MODEL)


def kernel(x, c, ctx, c_ctx, w_mod, b_mod, pre_g, post_g, w_in, conv_w, conv_b, rpb,
           w_out_conv, w_out_attn, w_o):
    depth = w_mod.shape[0]
    assert depth == 1, "context stream update between layers is not implemented"
    return _layer(x, c, ctx, c_ctx, w_mod[0], b_mod[0], pre_g[0], post_g[0], w_in[0], conv_w[0],
                  conv_b[0], rpb[0], w_out_conv[0], w_out_attn[0], w_o[0])
```

```python
import functools

import numpy as np
import jax
import jax.numpy as jnp
from jax import lax
from jax.experimental import pallas as pl
from jax.experimental.pallas import tpu as pltpu

D_MODEL = 1024
GRID_W = 64
N_HEADS = 16
HEAD_DIM = 64
WIN_ROWS = 8
WIN_COLS = 16
ROPE_BASE = 10000.0
EPS = 1e-6
PART_W = 1024
N_PARTS_IN = 10
PART_Q, PART_K, PART_V = 4, 5, 6
OUT_BG, OUT_CG, OUT_XI, OUT_ZA, OUT_K, OUT_ZB, OUT_GA, OUT_GB = range(8)
LANES = 128
N_PAIRS = N_HEADS // 2
WIN_PAIRS = 5
NEG = -1e30
RED_ROWS = 64
ATTN_GROUP = 4
MIX_ROWS = 128
SUM_ROWS = 16
LOG2E = 1.4426950408889634
BF16 = jnp.bfloat16
F32 = jnp.float32


def _modulated_norm(x, g, sc, sh):
    ms = jnp.mean(x * x, axis=-1, keepdims=True)
    return x * lax.rsqrt(ms + EPS) * (g * (1.0 + sc)) + sh


def _modulation_kernel(c_ref, w_ref, b_ref, o_ref):
    c = c_ref[...]
    o_ref[...] = jnp.dot(c * jax.nn.sigmoid(c), w_ref[...], preferred_element_type=F32,
                         precision=lax.Precision.HIGHEST) + b_ref[...]


def _modulation(cond8, w_mod, b_mod):
    n = w_mod.shape[1]
    tn = 768
    return pl.pallas_call(
        _modulation_kernel,
        grid=(n // tn,),
        in_specs=[pl.BlockSpec((8, D_MODEL), lambda j: (0, 0)),
                  pl.BlockSpec((D_MODEL, tn), lambda j: (0, j)),
                  pl.BlockSpec((1, tn), lambda j: (0, j))],
        out_specs=pl.BlockSpec((8, tn), lambda j: (0, j)),
        out_shape=jax.ShapeDtypeStruct((8, n), F32),
        name="modulation",
    )(cond8, w_mod, b_mod)


def _bias_consts():
    q = np.arange(GRID_W)[None, :]
    c = np.arange(GRID_W)[:, None]
    cs = np.clip(q - WIN_COLS // 2, 0, GRID_W - WIN_COLS)
    inwin = (c >= cs) & (c < cs + WIN_COLS)
    dc = c - q + (WIN_COLS - 1)
    sel = np.zeros((32, GRID_W, GRID_W), np.float32)
    cc, qq = np.nonzero(inwin)
    sel[dc[cc, qq], cc, qq] = 1.0
    negm = np.where(inwin, 0.0, NEG).astype(np.float32)[None]
    sel, negm = (np.concatenate([a, a], axis=-1).reshape(a.shape[0], GRID_W * LANES) for a in (sel, negm))
    return sel, negm


def _bias_kernel(rpb_ref, sel_ref, neg_ref, o_ref):
    o_ref[...] = (jnp.dot(rpb_ref[...], sel_ref[...], preferred_element_type=F32,
                          precision=lax.Precision.HIGHEST) + neg_ref[...]) * LOG2E


def _pair_window(t, n_row_pairs):
    if isinstance(t, int):
        sp = min(max(t - 2, 0), n_row_pairs - WIN_PAIRS)
        var = t if t < 2 else (t - (n_row_pairs - 2) + 3 if t >= n_row_pairs - 2 else 2)
        return sp, var
    sp = jnp.clip(t - 2, 0, n_row_pairs - WIN_PAIRS)
    var = jnp.where(t < 2, t, jnp.where(t >= n_row_pairs - 2, t - (n_row_pairs - 2) + 3, 2))
    return sp, var


N_BIAS_VAR = 5


def _bias_tables(rpb, rows):
    sel, negm = _bias_consts()
    n_dr = 2 * WIN_ROWS - 1
    n_rp = rows // 2
    rpb2 = jnp.pad(rpb.reshape(N_HEADS * n_dr, 2 * WIN_COLS - 1), ((0, 0), (0, 1)))
    bcol = pl.pallas_call(
        _bias_kernel,
        out_shape=jax.ShapeDtypeStruct((N_HEADS * n_dr, GRID_W * LANES), F32),
        name="bias_table",
    )(rpb2, jnp.asarray(sel), jnp.asarray(negm))
    blocks = bcol.reshape(N_PAIRS, 2 * n_dr, GRID_W, LANES)
    idx = np.full((N_BIAS_VAR, 2 * WIN_PAIRS, 2, 2), -1, np.int32)
    for v, t in enumerate([0, 1, 2, n_rp - 2, n_rp - 1]):
        sp, _ = _pair_window(t, n_rp)
        for w in range(2 * WIN_PAIRS):
            krow = 2 * sp + w
            for rr in range(2):
                r = 2 * t + rr
                rs = min(max(r - WIN_ROWS // 2, 0), rows - WIN_ROWS)
                if rs <= krow < rs + WIN_ROWS:
                    idx[v, w, :, rr] = np.arange(2) * n_dr + (krow - r + WIN_ROWS - 1)
    return pl.pallas_call(
        functools.partial(_bias_assemble_kernel, idx=idx),
        grid=(N_PAIRS,),
        in_specs=[pl.BlockSpec((1, 2 * n_dr, GRID_W, LANES), lambda p: (p, 0, 0, 0))],
        out_specs=pl.BlockSpec((1, N_BIAS_VAR, 2 * WIN_PAIRS * GRID_W, 2 * LANES), lambda p: (p, 0, 0, 0)),
        out_shape=jax.ShapeDtypeStruct((N_PAIRS, N_BIAS_VAR, 2 * WIN_PAIRS * GRID_W, 2 * LANES), F32),
        name="bias_assemble",
    )(blocks)


def _bias_assemble_kernel(u_ref, o_ref, *, idx):
    left = lax.broadcasted_iota(jnp.int32, (GRID_W, LANES), 1) < GRID_W
    masked = jnp.full((GRID_W, LANES), NEG, F32)
    n_var, n_win, _, _ = idx.shape
    for v in range(n_var):
        for w in range(n_win):
            for hh in range(2):
                b0, b1 = (u_ref[0, int(i)] if i >= 0 else masked for i in idx[v, w, hh])
                o_ref[0, v, w * GRID_W:(w + 1) * GRID_W, hh * LANES:(hh + 1) * LANES] = jnp.where(left, b0, b1)


def _ctx_kv_kernel(x_ref, sc_ref, sh_ref, g_ref, wk_ref, wv_ref, k_ref, v_ref):
    h = _modulated_norm(x_ref[...], g_ref[...], sc_ref[...], sh_ref[...]).astype(BF16)
    k_ref[...] = jnp.dot(h, wk_ref[...], preferred_element_type=F32).astype(BF16)
    v_ref[...] = jnp.dot(h, wv_ref[...], preferred_element_type=F32).astype(BF16)


def _ctx_kv(ctx2, sc_c, sh_c, pre_g, wk_bf, w_in_bf):
    m = ctx2.shape[0]

    def full(shape):
        return pl.BlockSpec(shape, lambda j: (0,) * len(shape))

    return pl.pallas_call(
        _ctx_kv_kernel,
        grid=(1,),
        in_specs=[full((m, D_MODEL)), full((1, D_MODEL)), full((1, D_MODEL)), full((1, D_MODEL)),
                  full((D_MODEL, PART_W)),
                  pl.BlockSpec((D_MODEL, PART_W), lambda j: (0, PART_V))],
        out_specs=[full((m, PART_W)), full((m, PART_W))],
        out_shape=[jax.ShapeDtypeStruct((m, PART_W), BF16)] * 2,
        name="ctx_kv",
    )(ctx2, sc_c, sh_c, pre_g, wk_bf, w_in_bf)


def _split_halves_layout(w_part):
    k = w_part.shape[0]
    q4 = HEAD_DIM // 4
    return w_part.reshape(k, N_HEADS, 2, 2, q4).transpose(0, 3, 1, 2, 4).reshape(k, PART_W)


def _inproj_kernel(x_ref, sc_ref, sh_ref, g_ref, w_ref, wq_ref, wk_ref, cos_ref, sin_ref,
                   p_ref, qt_ref, vt_ref, h_ref):
    tm = x_ref.shape[0]
    n_half = PART_W // (2 * LANES)
    h_ref[...] = _modulated_norm(x_ref[...], g_ref[...], sc_ref[0], sh_ref[0]).astype(BF16)

    def slab(r, s):
        return r[:, s * LANES:(s + 1) * LANES]

    def rope(r):
        firsts, seconds = [], []
        for s in range(n_half):
            a, b = slab(r, s), slab(r, n_half + s)
            firsts.append(a * cos_ref[...] - b * sin_ref[...])
            seconds.append(b * cos_ref[...] + a * sin_ref[...])
        return firsts + seconds

    def store_transposed(t_ref, s, u):
        ut = u.astype(BF16).T
        for j in range(tm // LANES):
            t_ref[0, j, s * LANES:(s + 1) * LANES, :] = ut[:, j * LANES:(j + 1) * LANES]

    out_col = 0
    for n in range(N_PARTS_IN):
        if n == PART_Q:
            w = wq_ref[...]
        elif n == PART_K:
            w = wk_ref[...]
        else:
            w = w_ref[:, n * PART_W:(n + 1) * PART_W]
        r = jnp.dot(h_ref[...], w, preferred_element_type=F32)
        if n == PART_Q:
            for s, u in enumerate(rope(r)):
                store_transposed(qt_ref, s, u * (HEAD_DIM ** -0.5 * LOG2E))
            continue
        if n == PART_V:
            for s in range(PART_W // LANES):
                store_transposed(vt_ref, s, slab(r, s))
            continue
        if n == PART_K:
            for s, u in enumerate(rope(r)):
                p_ref[:, out_col * PART_W + s * LANES:out_col * PART_W + (s + 1) * LANES] = u.astype(BF16)
        else:
            p_ref[:, out_col * PART_W:(out_col + 1) * PART_W] = r.astype(BF16)
        out_col += 1


def _inproj(x2, sc, sh, pre_g, w_in_bf, wq_bf, wk_bf, cos, sin, batch, seq, tm):
    m = x2.shape[0]
    tpb = seq // tm
    n_out = N_PARTS_IN - 2
    t_spec = pl.BlockSpec((1, tm // LANES, PART_W, LANES), lambda i: (i // tpb, i % tpb, 0, 0))
    t_shape = jax.ShapeDtypeStruct((batch, seq // LANES, PART_W, LANES), BF16)

    def resident(shape):
        return pl.BlockSpec(shape, lambda i: (0, 0), pipeline_mode=pl.Buffered(1))

    return pl.pallas_call(
        _inproj_kernel,
        grid=(m // tm,),
        in_specs=[pl.BlockSpec((tm, D_MODEL), lambda i: (i, 0)),
                  pl.BlockSpec((1, 1, D_MODEL), lambda i: (i // tpb, 0, 0)),
                  pl.BlockSpec((1, 1, D_MODEL), lambda i: (i // tpb, 0, 0)),
                  pl.BlockSpec((1, D_MODEL), lambda i: (0, 0)),
                  resident((D_MODEL, N_PARTS_IN * PART_W)),
                  resident((D_MODEL, PART_W)), resident((D_MODEL, PART_W)),
                  pl.BlockSpec((tm, LANES), lambda i: (i % tpb, 0)),
                  pl.BlockSpec((tm, LANES), lambda i: (i % tpb, 0))],
        out_specs=[pl.BlockSpec((tm, n_out * PART_W), lambda i: (i, 0)), t_spec, t_spec],
        out_shape=[jax.ShapeDtypeStruct((m, n_out * PART_W), BF16), t_shape, t_shape],
        scratch_shapes=[pltpu.VMEM((tm, D_MODEL), BF16)],
        compiler_params=pltpu.CompilerParams(dimension_semantics=("arbitrary",)),
        name="in_projection",
    )(x2, sc, sh, pre_g, w_in_bf, wq_bf, wk_bf, cos, sin)


def _attn_kernel(qta_ref, qtb_ref, ka_ref, kb_ref, vt_ref, kca_ref, kcb_ref, vct_ref, bias_ref,
                 o_ref, *stage_refs, n_row_pairs):
    low = lax.broadcasted_iota(jnp.int32, (LANES, LANES), 0) < HEAD_DIM
    nb = WIN_PAIRS * LANES
    last = n_row_pairs - 1
    head_of_dim = (lax.broadcasted_iota(jnp.int32, (2 * LANES, LANES), 0) % LANES) // (HEAD_DIM // 2)
    first_head = 2 * (pl.program_id(0) % 2)
    sel0 = head_of_dim == first_head
    sel1 = head_of_dim == first_head + 1

    def scores(t, slot):
        sp, var = _pair_window(t, n_row_pairs)
        qt = jnp.concatenate([qta_ref[0, t], qtb_ref[0, t]], axis=0)
        zero = jnp.zeros_like(qt)
        rhs = jnp.concatenate([jnp.where(sel0, qt, zero), jnp.where(sel1, qt, zero)], axis=1)
        rows = pl.ds(pl.multiple_of(sp * LANES, LANES), nb)
        kwin = jnp.concatenate([ka_ref[rows, :], kb_ref[rows, :]], axis=1)
        s_refs[slot][:nb, :] = jnp.dot(kwin, rhs, preferred_element_type=F32) + bias_ref[0, var]
        yield
        kc = jnp.concatenate([kca_ref[0], kcb_ref[0]], axis=1)
        s_refs[slot][nb:, :] = jnp.dot(kc, rhs, preferred_element_type=F32)

    def softmax(slot):
        s_ref, p_ref = s_refs[slot], p_refs[slot]
        chunks = [slice(c * RED_ROWS, (c + 1) * RED_ROWS) for c in range(s_ref.shape[0] // RED_ROWS)]
        m = s_ref[chunks[0], :]
        for c in chunks[1:]:
            m = jnp.maximum(m, s_ref[c, :])
        m = jnp.max(m, axis=0, keepdims=True)
        yield
        for k, c in enumerate(chunks):
            p_ref[c, :] = jnp.exp2(s_ref[c, :] - m).astype(BF16)
            if k % 4 == 3:
                yield

    def values(t, slot):
        sp, _ = _pair_window(t, n_row_pairs)
        p = p_refs[slot][...]
        vext = jnp.concatenate([vt_ref[0, sp + g] for g in range(WIN_PAIRS)] + [vct_ref[0]], axis=1)
        vext = jnp.concatenate([vext, jnp.ones((SUM_ROWS, p.shape[0]), BF16)], axis=0)
        ot = jnp.dot(vext, p, preferred_element_type=F32)
        yield
        ot = ot[:LANES] * (1.0 / ot[LANES:LANES + 1])
        out = jnp.where(low, ot[:, :LANES], ot[:, LANES:]).T
        o_ref[pl.ds(pl.multiple_of(t * LANES, LANES), LANES), :] = out.astype(BF16)

    def run(*stages):
        for stage in stages:
            for _ in stage:
                pass

    def interleave(sm, va, sc):
        next(sm)
        next(va)
        next(sm)
        next(sc)
        next(sm)
        run(sc)
        next(sm)
        run(va, sm)

    g = ATTN_GROUP
    s_refs, p_refs = stage_refs[:2 * g], stage_refs[2 * g:]
    for j in range(g):
        run(scores(j, j))
    for slot in range(g, 2 * g):
        p_refs[slot][...] = jnp.ones(p_refs[slot].shape, BF16)

    def group(t, cur, nxt):
        for j in range(g):
            interleave(softmax(cur + j), values(jnp.maximum(t + j - g, 0), nxt + j),
                       scores(jnp.minimum(t + j + g, last), nxt + j))

    def body(i, carry):
        t = 2 * g * i
        group(t, 0, g)
        group(t + g, g, 0)
        return carry

    lax.fori_loop(0, n_row_pairs // (2 * g), body, 0)
    for j in range(g):
        run(values(n_row_pairs - g + j, g + j))


def _attention(p8, qt, vt, kc, vct, bias, batch, seq):
    n_ctx = kc.shape[1]
    n_keys = WIN_PAIRS * LANES + n_ctx
    n_slots = 2 * ATTN_GROUP
    assert (seq // LANES) % n_slots == 0
    spp = PART_W // LANES
    half = spp // 2

    def t_spec(slab):
        return pl.BlockSpec((1, seq // LANES, LANES, LANES), lambda hp, b: (b, 0, slab(hp), 0))

    return pl.pallas_call(
        functools.partial(_attn_kernel, n_row_pairs=seq // LANES),
        grid=(N_PAIRS, batch),
        in_specs=[t_spec(lambda hp: hp // 2), t_spec(lambda hp: half + hp // 2),
                  pl.BlockSpec((seq, LANES), lambda hp, b: (b, OUT_K * spp + hp // 2)),
                  pl.BlockSpec((seq, LANES), lambda hp, b: (b, OUT_K * spp + half + hp // 2)),
                  t_spec(lambda hp: hp),
                  pl.BlockSpec((1, n_ctx, LANES), lambda hp, b: (b, 0, hp // 2)),
                  pl.BlockSpec((1, n_ctx, LANES), lambda hp, b: (b, 0, half + hp // 2)),
                  pl.BlockSpec((1, LANES, n_ctx), lambda hp, b: (b, hp, 0)),
                  pl.BlockSpec((1, N_BIAS_VAR, WIN_PAIRS * LANES, 2 * LANES),
                               lambda hp, b: (hp, 0, 0, 0))],
        out_specs=pl.BlockSpec((seq, LANES), lambda hp, b: (b, hp)),
        out_shape=jax.ShapeDtypeStruct((batch * seq, N_HEADS * HEAD_DIM), BF16),
        scratch_shapes=[pltpu.VMEM((n_keys, 2 * LANES), F32)] * n_slots
        + [pltpu.VMEM((n_keys, 2 * LANES), BF16)] * n_slots,
        compiler_params=pltpu.CompilerParams(dimension_semantics=("arbitrary", "arbitrary")),
        name="attention",
    )(qt, qt, p8, p8, vt, kc, kc, vct, bias)


def _mixer_kernel(bg_ref, cg_ref, xi_ref, za_ref, zb_ref, ga_ref, gb_ref, at_ref,
                  cgp_ref, xip_ref, cgn_ref, xin_ref, x_ref, gt_ref, cw_ref, cb_ref, pg_ref,
                  woc_ref, woa_ref, wo_ref, o_ref, *, tiles_per_batch, halo):
    t = pl.program_id(0) % tiles_per_batch
    u = cg_ref[...].astype(F32) * xi_ref[...].astype(F32)
    tm = u.shape[0]
    has_prev = jnp.where(t > 0, 1.0, 0.0)
    has_next = jnp.where(t < tiles_per_batch - 1, 1.0, 0.0)
    prev_row = cgp_ref[halo - 1:halo, :].astype(F32) * xip_ref[halo - 1:halo, :].astype(F32) * has_prev
    next_row = cgn_ref[0:1, :].astype(F32) * xin_ref[0:1, :].astype(F32) * has_next
    row = lax.broadcasted_iota(jnp.int32, u.shape, 0)
    u_prev = jnp.where(row == 0, prev_row, pltpu.roll(u, 1, 0))
    u_next = jnp.where(row == tm - 1, next_row, pltpu.roll(u, tm - 1, 0))
    conv = (u_prev * cw_ref[0:1, :] + u * cw_ref[1:2, :] + u_next * cw_ref[2:3, :] + cb_ref[...]).astype(BF16)
    for c in range(tm // MIX_ROWS):
        rows = slice(c * MIX_ROWS, (c + 1) * MIX_ROWS)
        za = za_ref[rows, :]
        a = (za * jax.nn.sigmoid(za)) * bg_ref[rows, :] * conv[rows, :]
        y_a = jnp.dot(a, woc_ref[...], preferred_element_type=F32)
        zb = zb_ref[rows, :]
        bb = (zb * jax.nn.sigmoid(zb)) * at_ref[rows, :]
        y_b = jnp.dot(bb, woa_ref[...], preferred_element_type=F32)
        merged = (jax.nn.sigmoid(ga_ref[rows, :]).astype(F32) * y_a
                  + jax.nn.sigmoid(gb_ref[rows, :]).astype(F32) * y_b)
        y = jnp.dot(merged.astype(BF16), wo_ref[...], preferred_element_type=F32)
        ms = jnp.mean(y * y, axis=-1, keepdims=True)
        o_ref[rows, :] = x_ref[rows, :] + gt_ref[0] * ((y * lax.rsqrt(ms + EPS)) * pg_ref[...])


def _mixer(p8, attn, x2, gt, conv_w, conv_b, post_g, woc, woa, wo, seq, tm):
    m = x2.shape[0]
    tpb = seq // tm
    halo = 16
    hb = tm // halo
    nhb = m // halo

    def part(k):
        return pl.BlockSpec((tm, PART_W), lambda i: (i, k))

    def prev(k):
        return pl.BlockSpec((halo, PART_W), lambda i: (jnp.maximum(i * hb - 1, 0), k))

    def nxt(k):
        return pl.BlockSpec((halo, PART_W), lambda i: (jnp.minimum((i + 1) * hb, nhb - 1), k))

    def full(shape):
        return pl.BlockSpec(shape, lambda i: (0,) * len(shape))

    return pl.pallas_call(
        functools.partial(_mixer_kernel, tiles_per_batch=tpb, halo=halo),
        grid=(m // tm,),
        in_specs=[part(OUT_BG), part(OUT_CG), part(OUT_XI), part(OUT_ZA), part(OUT_ZB),
                  part(OUT_GA), part(OUT_GB),
                  pl.BlockSpec((tm, PART_W), lambda i: (i, 0)),
                  prev(OUT_CG), prev(OUT_XI), nxt(OUT_CG), nxt(OUT_XI),
                  pl.BlockSpec((tm, D_MODEL), lambda i: (i, 0)),
                  pl.BlockSpec((1, 1, D_MODEL), lambda i: (i // tpb, 0, 0)),
                  full((3, PART_W)), full((1, PART_W)), full((1, D_MODEL)),
                  full((PART_W, D_MODEL)), full((PART_W, D_MODEL)), full((D_MODEL, D_MODEL))],
        out_specs=pl.BlockSpec((tm, D_MODEL), lambda i: (i, 0)),
        out_shape=jax.ShapeDtypeStruct((m, D_MODEL), F32),
        compiler_params=pltpu.CompilerParams(dimension_semantics=("arbitrary",)),
        name="mixer_out",
    )(p8, p8, p8, p8, p8, p8, p8, attn, p8, p8, p8, p8, x2, gt,
      conv_w, conv_b, post_g, woc, woa, wo)


def _rope_tables(seq):
    t = jnp.arange(seq, dtype=jnp.int32)
    row = (t // GRID_W).astype(F32)
    col = (t % GRID_W).astype(F32)
    half = HEAD_DIM // 2
    inv = ROPE_BASE ** (-jnp.arange(0, half, 2, dtype=F32) / half)
    ang_r = row[:, None] * inv
    ang_c = col[:, None] * inv
    ang = jnp.tile(jnp.concatenate([ang_r, ang_c], axis=-1), (1, LANES // (HEAD_DIM // 2)))
    return jnp.cos(ang), jnp.sin(ang)


def _layer(x, c, ctx, c_ctx, w_mod, b_mod, pre_g, post_g, w_in, conv_w, conv_b, rpb,
           w_out_conv, w_out_attn, w_o):
    batch, seq, _ = x.shape
    n_ctx = ctx.shape[1]
    rows = seq // GRID_W
    assert rows >= 2 * WIN_PAIRS and seq % LANES == 0 and batch <= 7

    cond8 = jnp.zeros((8, D_MODEL), F32).at[:batch].set(c).at[batch].set(c_ctx)
    mod = _modulation(cond8, w_mod, b_mod.reshape(1, -1))
    sh, sc, gt = (mod[:batch, k * D_MODEL:(k + 1) * D_MODEL].reshape(batch, 1, D_MODEL) for k in range(3))
    sh_c, sc_c = (mod[batch:batch + 1, k * D_MODEL:(k + 1) * D_MODEL] for k in range(2))

    pre_g2 = pre_g.reshape(1, D_MODEL)
    w_in_bf = w_in.astype(BF16)
    wq_bf = _split_halves_layout(w_in_bf[:, PART_Q * PART_W:(PART_Q + 1) * PART_W])
    wk_bf = _split_halves_layout(w_in_bf[:, PART_K * PART_W:(PART_K + 1) * PART_W])
    kc, vc = _ctx_kv(ctx.reshape(batch * n_ctx, D_MODEL), sc_c, sh_c, pre_g2, wk_bf, w_in_bf)
    kc = kc.reshape(batch, n_ctx, PART_W)
    vct = jnp.transpose(vc.reshape(batch, n_ctx, PART_W), (0, 2, 1))

    x2 = x.reshape(batch * seq, D_MODEL)
    cos, sin = _rope_tables(seq)
    tm_in = min(512, seq)
    p8, qt, vt = _inproj(x2, sc, sh, pre_g2, w_in_bf, wq_bf, wk_bf, cos, sin, batch, seq, tm_in)

    bias = _bias_tables(rpb, rows)
    attn = _attention(p8, qt, vt, kc, vct, bias, batch, seq)

    tm_out = min(512, seq)
    out = _mixer(p8, attn, x2, gt, conv_w, conv_b.reshape(1, -1), post_g.reshape(1, -1),
                 w_out_conv.astype(BF16), w_out_attn.astype(BF16), w_o.astype(BF16), seq, tm_out)
    return out.reshape(batch, seq, D_MODEL)


def kernel(x, c, ctx, c_ctx, w_mod, b_mod, pre_g, post_g, w_in, conv_w, conv_b, rpb,
           w_out_conv, w_out_attn, w_o):
    depth = w_mod.shape[0]
    assert depth == 1, "context stream update between layers is not implemented"
    return _layer(x, c, ctx, c_ctx, w_mod[0], b_mod[0], pre_g[0], post_g[0], w_in[0], conv_w[0],
                  conv_b[0], rpb[0], w_out_conv[0], w_out_attn[0], w_o[0])
```

```python
import functools

import numpy as np
import jax
import jax.numpy as jnp
from jax import lax
from jax.experimental import pallas as pl
from jax.experimental.pallas import tpu as pltpu

D_MODEL = 1024
GRID_W = 64
N_HEADS = 16
HEAD_DIM = 64
WIN_ROWS = 8
WIN_COLS = 16
ROPE_BASE = 10000.0
EPS = 1e-6
PART_W = 1024
N_PARTS_IN = 10
PART_BG, PART_CG, PART_XI, PART_ZA, PART_Q, PART_K, PART_V, PART_ZB, PART_GA, PART_GB = range(N_PARTS_IN)
OUT_A, OUT_K, OUT_SZB, OUT_SGA, OUT_SGB = range(5)
N_PARTS_OUT = 5
HALO = 16
LANES = 128
N_PAIRS = N_HEADS // 2
WIN_PAIRS = 5
NEG = -1e30
RED_ROWS = 64
ATTN_GROUP = 4
SUM_ROWS = 16
LOG2E = 1.4426950408889634
BF16 = jnp.bfloat16
F32 = jnp.float32


def _modulated_norm(x, g, sc, sh):
    ms = jnp.mean(x * x, axis=-1, keepdims=True)
    return x * lax.rsqrt(ms + EPS) * (g * (1.0 + sc)) + sh


def _modulation_kernel(c_ref, w_ref, b_ref, o_ref):
    c = c_ref[...]
    o_ref[...] = jnp.dot(c * jax.nn.sigmoid(c), w_ref[...], preferred_element_type=F32,
                         precision=lax.Precision.HIGHEST) + b_ref[...]


def _modulation(cond8, w_mod, b_mod):
    n = w_mod.shape[1]
    tn = 768
    return pl.pallas_call(
        _modulation_kernel,
        grid=(n // tn,),
        in_specs=[pl.BlockSpec((8, D_MODEL), lambda j: (0, 0)),
                  pl.BlockSpec((D_MODEL, tn), lambda j: (0, j)),
                  pl.BlockSpec((1, tn), lambda j: (0, j))],
        out_specs=pl.BlockSpec((8, tn), lambda j: (0, j)),
        out_shape=jax.ShapeDtypeStruct((8, n), F32),
        name="modulation",
    )(cond8, w_mod, b_mod)


def _bias_consts():
    q = np.arange(GRID_W)[None, :]
    c = np.arange(GRID_W)[:, None]
    cs = np.clip(q - WIN_COLS // 2, 0, GRID_W - WIN_COLS)
    inwin = (c >= cs) & (c < cs + WIN_COLS)
    dc = c - q + (WIN_COLS - 1)
    sel = np.zeros((32, GRID_W, GRID_W), np.float32)
    cc, qq = np.nonzero(inwin)
    sel[dc[cc, qq], cc, qq] = 1.0
    negm = np.where(inwin, 0.0, NEG).astype(np.float32)[None]
    sel, negm = (np.concatenate([a, a], axis=-1).reshape(a.shape[0], GRID_W * LANES) for a in (sel, negm))
    return sel, negm


def _bias_kernel(rpb_ref, sel_ref, neg_ref, o_ref):
    o_ref[...] = (jnp.dot(rpb_ref[...], sel_ref[...], preferred_element_type=F32,
                          precision=lax.Precision.HIGHEST) + neg_ref[...]) * LOG2E


def _pair_window(t, n_row_pairs):
    if isinstance(t, int):
        sp = min(max(t - 2, 0), n_row_pairs - WIN_PAIRS)
        var = t if t < 2 else (t - (n_row_pairs - 2) + 3 if t >= n_row_pairs - 2 else 2)
        return sp, var
    sp = jnp.clip(t - 2, 0, n_row_pairs - WIN_PAIRS)
    var = jnp.where(t < 2, t, jnp.where(t >= n_row_pairs - 2, t - (n_row_pairs - 2) + 3, 2))
    return sp, var


N_BIAS_VAR = 5


def _bias_tables(rpb, rows):
    sel, negm = _bias_consts()
    n_dr = 2 * WIN_ROWS - 1
    n_rp = rows // 2
    rpb2 = jnp.pad(rpb.reshape(N_HEADS * n_dr, 2 * WIN_COLS - 1), ((0, 0), (0, 1)))
    bcol = pl.pallas_call(
        _bias_kernel,
        out_shape=jax.ShapeDtypeStruct((N_HEADS * n_dr, GRID_W * LANES), F32),
        name="bias_table",
    )(rpb2, jnp.asarray(sel), jnp.asarray(negm))
    blocks = bcol.reshape(N_PAIRS, 2 * n_dr, GRID_W, LANES)
    idx = np.full((N_BIAS_VAR, 2 * WIN_PAIRS, 2, 2), -1, np.int32)
    for v, t in enumerate([0, 1, 2, n_rp - 2, n_rp - 1]):
        sp, _ = _pair_window(t, n_rp)
        for w in range(2 * WIN_PAIRS):
            krow = 2 * sp + w
            for rr in range(2):
                r = 2 * t + rr
                rs = min(max(r - WIN_ROWS // 2, 0), rows - WIN_ROWS)
                if rs <= krow < rs + WIN_ROWS:
                    idx[v, w, :, rr] = np.arange(2) * n_dr + (krow - r + WIN_ROWS - 1)
    return pl.pallas_call(
        functools.partial(_bias_assemble_kernel, idx=idx),
        grid=(N_PAIRS,),
        in_specs=[pl.BlockSpec((1, 2 * n_dr, GRID_W, LANES), lambda p: (p, 0, 0, 0))],
        out_specs=pl.BlockSpec((1, N_BIAS_VAR, 2 * WIN_PAIRS * GRID_W, 2 * LANES), lambda p: (p, 0, 0, 0)),
        out_shape=jax.ShapeDtypeStruct((N_PAIRS, N_BIAS_VAR, 2 * WIN_PAIRS * GRID_W, 2 * LANES), F32),
        name="bias_assemble",
    )(blocks)


def _bias_assemble_kernel(u_ref, o_ref, *, idx):
    left = lax.broadcasted_iota(jnp.int32, (GRID_W, LANES), 1) < GRID_W
    masked = jnp.full((GRID_W, LANES), NEG, F32)
    n_var, n_win, _, _ = idx.shape
    for v in range(n_var):
        for w in range(n_win):
            for hh in range(2):
                b0, b1 = (u_ref[0, int(i)] if i >= 0 else masked for i in idx[v, w, hh])
                o_ref[0, v, w * GRID_W:(w + 1) * GRID_W, hh * LANES:(hh + 1) * LANES] = jnp.where(left, b0, b1)


def _ctx_kv_kernel(x_ref, sc_ref, sh_ref, g_ref, wk_ref, wv_ref, k_ref, v_ref):
    h = _modulated_norm(x_ref[...], g_ref[...], sc_ref[...], sh_ref[...]).astype(BF16)
    k_ref[...] = jnp.dot(h, wk_ref[...], preferred_element_type=F32).astype(BF16)
    v_ref[...] = jnp.dot(h, wv_ref[...], preferred_element_type=F32).astype(BF16)


def _ctx_kv(ctx2, sc_c, sh_c, pre_g, wk_bf, w_in_bf):
    m = ctx2.shape[0]

    def full(shape):
        return pl.BlockSpec(shape, lambda j: (0,) * len(shape))

    return pl.pallas_call(
        _ctx_kv_kernel,
        grid=(1,),
        in_specs=[full((m, D_MODEL)), full((1, D_MODEL)), full((1, D_MODEL)), full((1, D_MODEL)),
                  full((D_MODEL, PART_W)),
                  pl.BlockSpec((D_MODEL, PART_W), lambda j: (0, PART_V))],
        out_specs=[full((m, PART_W)), full((m, PART_W))],
        out_shape=[jax.ShapeDtypeStruct((m, PART_W), BF16)] * 2,
        name="ctx_kv",
    )(ctx2, sc_c, sh_c, pre_g, wk_bf, w_in_bf)


def _split_halves_layout(w_part):
    k = w_part.shape[0]
    q4 = HEAD_DIM // 4
    return w_part.reshape(k, N_HEADS, 2, 2, q4).transpose(0, 3, 1, 2, 4).reshape(k, PART_W)


def _inproj_kernel(x_ref, xp_ref, xn_ref, sc_ref, sh_ref, g_ref, w_ref, wq_ref, wk_ref, cos_ref, sin_ref,
                   cw_ref, cb_ref, p_ref, qt_ref, vt_ref, h_ref, *, tiles_per_batch):
    tm = x_ref.shape[0]
    n_half = PART_W // (2 * LANES)
    t = pl.program_id(0) % tiles_per_batch
    for rows, ref in ((slice(0, HALO), xp_ref), (slice(HALO, HALO + tm), x_ref),
                      (slice(HALO + tm, 2 * HALO + tm), xn_ref)):
        h_ref[rows, :] = _modulated_norm(ref[...], g_ref[...], sc_ref[0], sh_ref[0]).astype(BF16)
    tile = slice(HALO, HALO + tm)

    def slab(r, s):
        return r[:, s * LANES:(s + 1) * LANES]

    def rope(r):
        firsts, seconds = [], []
        for s in range(n_half):
            a, b = slab(r, s), slab(r, n_half + s)
            firsts.append(a * cos_ref[...] - b * sin_ref[...])
            seconds.append(b * cos_ref[...] + a * sin_ref[...])
        return firsts + seconds

    def store_transposed(t_ref, s, u):
        ut = u.astype(BF16).T
        for j in range(tm // LANES):
            t_ref[0, j, s * LANES:(s + 1) * LANES, :] = ut[:, j * LANES:(j + 1) * LANES]

    def project(n, rows):
        if n == PART_Q:
            w = wq_ref[...]
        elif n == PART_K:
            w = wk_ref[...]
        else:
            w = w_ref[:, n * PART_W:(n + 1) * PART_W]
        return jnp.dot(h_ref[rows, :], w, preferred_element_type=F32)

    def put(col, val):
        p_ref[:, col * PART_W:(col + 1) * PART_W] = val.astype(BF16)

    b_gate = project(PART_BG, tile)
    n_ext = tm + 2 * HALO
    u = project(PART_CG, slice(None)) * project(PART_XI, slice(None))
    row = lax.broadcasted_iota(jnp.int32, (n_ext, 1), 0)
    first_valid = jnp.where(t == 0, HALO, 0)
    end_valid = jnp.where(t == tiles_per_batch - 1, HALO + tm, n_ext)
    u = jnp.where((row >= first_valid) & (row < end_valid), u, 0.0)
    conv = (pltpu.roll(u, 1, 0)[tile] * cw_ref[0:1, :] + u[tile] * cw_ref[1:2, :]
            + pltpu.roll(u, n_ext - 1, 0)[tile] * cw_ref[2:3, :] + cb_ref[...])
    z_a = project(PART_ZA, tile)
    put(OUT_A, (z_a * jax.nn.sigmoid(z_a)) * b_gate * conv)

    for s, rk in enumerate(rope(project(PART_Q, tile))):
        store_transposed(qt_ref, s, rk * (HEAD_DIM ** -0.5 * LOG2E))
    for s, rk in enumerate(rope(project(PART_K, tile))):
        p_ref[:, OUT_K * PART_W + s * LANES:OUT_K * PART_W + (s + 1) * LANES] = rk.astype(BF16)
    r = project(PART_V, tile)
    for s in range(PART_W // LANES):
        store_transposed(vt_ref, s, slab(r, s))
    z_b = project(PART_ZB, tile)
    put(OUT_SZB, z_b * jax.nn.sigmoid(z_b))
    put(OUT_SGA, jax.nn.sigmoid(project(PART_GA, tile)))
    put(OUT_SGB, jax.nn.sigmoid(project(PART_GB, tile)))


def _inproj(x2, sc, sh, pre_g, w_in_bf, wq_bf, wk_bf, cos, sin, conv_w, conv_b, batch, seq, tm):
    m = x2.shape[0]
    tpb = seq // tm
    hb = tm // HALO
    t_spec = pl.BlockSpec((1, tm // LANES, PART_W, LANES), lambda i: (i // tpb, i % tpb, 0, 0))
    t_shape = jax.ShapeDtypeStruct((batch, seq // LANES, PART_W, LANES), BF16)

    def resident(shape):
        return pl.BlockSpec(shape, lambda i: (0, 0), pipeline_mode=pl.Buffered(1))

    return pl.pallas_call(
        functools.partial(_inproj_kernel, tiles_per_batch=tpb),
        grid=(m // tm,),
        in_specs=[pl.BlockSpec((tm, D_MODEL), lambda i: (i, 0)),
                  pl.BlockSpec((HALO, D_MODEL), lambda i: (jnp.maximum(i * hb - 1, 0), 0)),
                  pl.BlockSpec((HALO, D_MODEL), lambda i: (jnp.minimum((i + 1) * hb, m // HALO - 1), 0)),
                  pl.BlockSpec((1, 1, D_MODEL), lambda i: (i // tpb, 0, 0)),
                  pl.BlockSpec((1, 1, D_MODEL), lambda i: (i // tpb, 0, 0)),
                  pl.BlockSpec((1, D_MODEL), lambda i: (0, 0)),
                  resident((D_MODEL, N_PARTS_IN * PART_W)),
                  resident((D_MODEL, PART_W)), resident((D_MODEL, PART_W)),
                  pl.BlockSpec((tm, LANES), lambda i: (i % tpb, 0)),
                  pl.BlockSpec((tm, LANES), lambda i: (i % tpb, 0)),
                  pl.BlockSpec((3, PART_W), lambda i: (0, 0)),
                  pl.BlockSpec((1, PART_W), lambda i: (0, 0))],
        out_specs=[pl.BlockSpec((tm, N_PARTS_OUT * PART_W), lambda i: (i, 0)), t_spec, t_spec],
        out_shape=[jax.ShapeDtypeStruct((m, N_PARTS_OUT * PART_W), BF16), t_shape, t_shape],
        scratch_shapes=[pltpu.VMEM((tm + 2 * HALO, D_MODEL), BF16)],
        compiler_params=pltpu.CompilerParams(dimension_semantics=("arbitrary",)),
        name="in_projection",
    )(x2, x2, x2, sc, sh, pre_g, w_in_bf, wq_bf, wk_bf, cos, sin, conv_w, conv_b)


def _attn_kernel(qta_ref, qtb_ref, ka_ref, kb_ref, vt_ref, kca_ref, kcb_ref, vct_ref, bias_ref,
                 o_ref, *stage_refs, n_row_pairs):
    low = lax.broadcasted_iota(jnp.int32, (LANES, LANES), 0) < HEAD_DIM
    nb = WIN_PAIRS * LANES
    last = n_row_pairs - 1
    head_of_dim = (lax.broadcasted_iota(jnp.int32, (2 * LANES, LANES), 0) % LANES) // (HEAD_DIM // 2)
    first_head = 2 * (pl.program_id(0) % 2)
    sel0 = head_of_dim == first_head
    sel1 = head_of_dim == first_head + 1

    def scores(t, slot):
        sp, var = _pair_window(t, n_row_pairs)
        qt = jnp.concatenate([qta_ref[0, t], qtb_ref[0, t]], axis=0)
        zero = jnp.zeros_like(qt)
        rhs = jnp.concatenate([jnp.where(sel0, qt, zero), jnp.where(sel1, qt, zero)], axis=1)
        rows = pl.ds(pl.multiple_of(sp * LANES, LANES), nb)
        kwin = jnp.concatenate([ka_ref[rows, :], kb_ref[rows, :]], axis=1)
        s_refs[slot][:nb, :] = jnp.dot(kwin, rhs, preferred_element_type=F32) + bias_ref[0, var]
        yield
        kc = jnp.concatenate([kca_ref[0], kcb_ref[0]], axis=1)
        s_refs[slot][nb:, :] = jnp.dot(kc, rhs, preferred_element_type=F32)

    def softmax(slot):
        s_ref, p_ref = s_refs[slot], p_refs[slot]
        chunks = [slice(c * RED_ROWS, (c + 1) * RED_ROWS) for c in range(s_ref.shape[0] // RED_ROWS)]
        m = s_ref[chunks[0], :]
        for c in chunks[1:]:
            m = jnp.maximum(m, s_ref[c, :])
        m = jnp.max(m, axis=0, keepdims=True)
        yield
        for k, c in enumerate(chunks):
            p_ref[c, :] = jnp.exp2(s_ref[c, :] - m).astype(BF16)
            if k % 4 == 3:
                yield

    def values(t, slot):
        sp, _ = _pair_window(t, n_row_pairs)
        p = p_refs[slot][...]
        vext = jnp.concatenate([vt_ref[0, sp + g] for g in range(WIN_PAIRS)] + [vct_ref[0]], axis=1)
        vext = jnp.concatenate([vext, jnp.ones((SUM_ROWS, p.shape[0]), BF16)], axis=0)
        ot = jnp.dot(vext, p, preferred_element_type=F32)
        yield
        ot = ot[:LANES] * (1.0 / ot[LANES:LANES + 1])
        out = jnp.where(low, ot[:, :LANES], ot[:, LANES:]).T
        o_ref[pl.ds(pl.multiple_of(t * LANES, LANES), LANES), :] = out.astype(BF16)

    def run(*stages):
        for stage in stages:
            for _ in stage:
                pass

    def interleave(sm, va, sc):
        next(sm)
        next(sc)
        next(sm)
        next(sm)
        run(sc)
        next(sm)
        run(sm, va)

    g = ATTN_GROUP
    s_refs, p_refs = stage_refs[:2 * g], stage_refs[2 * g:]
    for j in range(g):
        run(scores(j, j))

    def group(t, cur, nxt):
        for j in range(g):
            interleave(softmax(cur + j), values(t + j, cur + j),
                       scores(jnp.minimum(t + j + g, last), nxt + j))

    def body(i, carry):
        t = 2 * g * i
        group(t, 0, g)
        group(t + g, g, 0)
        return carry

    lax.fori_loop(0, n_row_pairs // (2 * g), body, 0)


def _attention(p5, qt, vt, kc, vct, bias, batch, seq):
    n_ctx = kc.shape[1]
    n_keys = WIN_PAIRS * LANES + n_ctx
    n_slots = 2 * ATTN_GROUP
    assert (seq // LANES) % n_slots == 0
    spp = PART_W // LANES
    half = spp // 2

    def t_spec(slab):
        return pl.BlockSpec((1, seq // LANES, LANES, LANES), lambda hp, b: (b, 0, slab(hp), 0))

    return pl.pallas_call(
        functools.partial(_attn_kernel, n_row_pairs=seq // LANES),
        grid=(N_PAIRS, batch),
        in_specs=[t_spec(lambda hp: hp // 2), t_spec(lambda hp: half + hp // 2),
                  pl.BlockSpec((seq, LANES), lambda hp, b: (b, OUT_K * spp + hp // 2)),
                  pl.BlockSpec((seq, LANES), lambda hp, b: (b, OUT_K * spp + half + hp // 2)),
                  t_spec(lambda hp: hp),
                  pl.BlockSpec((1, n_ctx, LANES), lambda hp, b: (b, 0, hp // 2)),
                  pl.BlockSpec((1, n_ctx, LANES), lambda hp, b: (b, 0, half + hp // 2)),
                  pl.BlockSpec((1, LANES, n_ctx), lambda hp, b: (b, hp, 0)),
                  pl.BlockSpec((1, N_BIAS_VAR, WIN_PAIRS * LANES, 2 * LANES),
                               lambda hp, b: (hp, 0, 0, 0))],
        out_specs=pl.BlockSpec((seq, LANES), lambda hp, b: (b, hp)),
        out_shape=jax.ShapeDtypeStruct((batch * seq, N_HEADS * HEAD_DIM), BF16),
        scratch_shapes=[pltpu.VMEM((n_keys, 2 * LANES), F32)] * n_slots
        + [pltpu.VMEM((n_keys, 2 * LANES), BF16)] * n_slots,
        compiler_params=pltpu.CompilerParams(dimension_semantics=("arbitrary", "arbitrary")),
        name="attention",
    )(qt, qt, p5, p5, vt, kc, kc, vct, bias)


def _mixer_kernel(a_ref, szb_ref, sga_ref, sgb_ref, at_ref, x_ref, gt_ref, pg_ref,
                  woc_ref, woa_ref, wo_ref, o_ref):
    y_a = jnp.dot(a_ref[...], woc_ref[...], preferred_element_type=F32)
    bb = szb_ref[...].astype(F32) * at_ref[...].astype(F32)
    y_b = jnp.dot(bb.astype(BF16), woa_ref[...], preferred_element_type=F32)
    merged = sga_ref[...].astype(F32) * y_a + sgb_ref[...].astype(F32) * y_b
    y = jnp.dot(merged.astype(BF16), wo_ref[...], preferred_element_type=F32)
    ms = jnp.mean(y * y, axis=-1, keepdims=True)
    o_ref[...] = x_ref[...] + gt_ref[0] * ((y * lax.rsqrt(ms + EPS)) * pg_ref[...])


def _mixer(p5, attn, x2, gt, post_g, woc, woa, wo, seq, tm):
    m = x2.shape[0]
    tpb = seq // tm

    def part(k):
        return pl.BlockSpec((tm, PART_W), lambda i: (i, k))

    def full(shape):
        return pl.BlockSpec(shape, lambda i: (0,) * len(shape))

    return pl.pallas_call(
        _mixer_kernel,
        grid=(m // tm,),
        in_specs=[part(OUT_A), part(OUT_SZB), part(OUT_SGA), part(OUT_SGB),
                  pl.BlockSpec((tm, PART_W), lambda i: (i, 0)),
                  pl.BlockSpec((tm, D_MODEL), lambda i: (i, 0)),
                  pl.BlockSpec((1, 1, D_MODEL), lambda i: (i // tpb, 0, 0)),
                  full((1, D_MODEL)),
                  full((PART_W, D_MODEL)), full((PART_W, D_MODEL)), full((D_MODEL, D_MODEL))],
        out_specs=pl.BlockSpec((tm, D_MODEL), lambda i: (i, 0)),
        out_shape=jax.ShapeDtypeStruct((m, D_MODEL), F32),
        compiler_params=pltpu.CompilerParams(dimension_semantics=("arbitrary",)),
        name="mixer_out",
    )(p5, p5, p5, p5, attn, x2, gt, post_g, woc, woa, wo)


def _rope_tables(seq):
    t = jnp.arange(seq, dtype=jnp.int32)
    row = (t // GRID_W).astype(F32)
    col = (t % GRID_W).astype(F32)
    half = HEAD_DIM // 2
    inv = ROPE_BASE ** (-jnp.arange(0, half, 2, dtype=F32) / half)
    ang_r = row[:, None] * inv
    ang_c = col[:, None] * inv
    ang = jnp.tile(jnp.concatenate([ang_r, ang_c], axis=-1), (1, LANES // (HEAD_DIM // 2)))
    return jnp.cos(ang), jnp.sin(ang)


def _layer(x, c, ctx, c_ctx, w_mod, b_mod, pre_g, post_g, w_in, conv_w, conv_b, rpb,
           w_out_conv, w_out_attn, w_o):
    batch, seq, _ = x.shape
    n_ctx = ctx.shape[1]
    rows = seq // GRID_W
    assert rows >= 2 * WIN_PAIRS and seq % LANES == 0 and batch <= 7

    cond8 = jnp.zeros((8, D_MODEL), F32).at[:batch].set(c).at[batch].set(c_ctx)
    mod = _modulation(cond8, w_mod, b_mod.reshape(1, -1))
    sh, sc, gt = (mod[:batch, k * D_MODEL:(k + 1) * D_MODEL].reshape(batch, 1, D_MODEL) for k in range(3))
    sh_c, sc_c = (mod[batch:batch + 1, k * D_MODEL:(k + 1) * D_MODEL] for k in range(2))

    pre_g2 = pre_g.reshape(1, D_MODEL)
    w_in_bf = w_in.astype(BF16)
    wq_bf = _split_halves_layout(w_in_bf[:, PART_Q * PART_W:(PART_Q + 1) * PART_W])
    wk_bf = _split_halves_layout(w_in_bf[:, PART_K * PART_W:(PART_K + 1) * PART_W])
    kc, vc = _ctx_kv(ctx.reshape(batch * n_ctx, D_MODEL), sc_c, sh_c, pre_g2, wk_bf, w_in_bf)
    kc = kc.reshape(batch, n_ctx, PART_W)
    vct = jnp.transpose(vc.reshape(batch, n_ctx, PART_W), (0, 2, 1))

    x2 = x.reshape(batch * seq, D_MODEL)
    cos, sin = _rope_tables(seq)
    tm_in = min(512, seq)
    p5, qt, vt = _inproj(x2, sc, sh, pre_g2, w_in_bf, wq_bf, wk_bf, cos, sin, conv_w,
                         conv_b.reshape(1, -1), batch, seq, tm_in)

    bias = _bias_tables(rpb, rows)
    attn = _attention(p5, qt, vt, kc, vct, bias, batch, seq)

    tm_out = min(512, seq)
    out = _mixer(p5, attn, x2, gt, post_g.reshape(1, -1),
                 w_out_conv.astype(BF16), w_out_attn.astype(BF16), w_o.astype(BF16), seq, tm_out)
    return out.reshape(batch, seq, D_MODEL)


def kernel(x, c, ctx, c_ctx, w_mod, b_mod, pre_g, post_g, w_in, conv_w, conv_b, rpb,
           w_out_conv, w_out_attn, w_o):
    depth = w_mod.shape[0]
    assert depth == 1, "context stream update between layers is not implemented"
    return _layer(x, c, ctx, c_ctx, w_mod[0], b_mod[0], pre_g[0], post_g[0], w_in[0], conv_w[0],
                  conv_b[0], rpb[0], w_out_conv[0], w_out_attn[0], w_o[0])
```

```python
import functools

import numpy as np
import jax
import jax.numpy as jnp
from jax import lax
from jax.experimental import pallas as pl
from jax.experimental.pallas import tpu as pltpu

D_MODEL = 1024
GRID_W = 64
N_HEADS = 16
HEAD_DIM = 64
WIN_ROWS = 8
WIN_COLS = 16
ROPE_BASE = 10000.0
EPS = 1e-6
PART_W = 1024
N_PARTS_IN = 10
PART_BG, PART_CG, PART_XI, PART_ZA, PART_Q, PART_K, PART_V, PART_ZB, PART_GA, PART_GB = range(N_PARTS_IN)
OUT_A, OUT_K, OUT_SZB, OUT_SGA, OUT_SGB = range(5)
N_PARTS_OUT = 5
CONV_ROWS = 64
HALO = 16
LANES = 128
N_PAIRS = N_HEADS // 2
WIN_PAIRS = 5
NEG = -1e30
RED_ROWS = 64
ATTN_GROUP = 4
SUM_ROWS = 16
LOG2E = 1.4426950408889634
BF16 = jnp.bfloat16
F32 = jnp.float32


def _modulated_norm(x, g, sc, sh):
    ms = jnp.mean(x * x, axis=-1, keepdims=True)
    return x * lax.rsqrt(ms + EPS) * (g * (1.0 + sc)) + sh


def _modulation_kernel(c_ref, w_ref, b_ref, o_ref):
    c = c_ref[...]
    o_ref[...] = jnp.dot(c * jax.nn.sigmoid(c), w_ref[...], preferred_element_type=F32,
                         precision=lax.Precision.HIGHEST) + b_ref[...]


def _modulation(cond8, w_mod, b_mod):
    n = w_mod.shape[1]
    tn = 768
    return pl.pallas_call(
        _modulation_kernel,
        grid=(n // tn,),
        in_specs=[pl.BlockSpec((8, D_MODEL), lambda j: (0, 0)),
                  pl.BlockSpec((D_MODEL, tn), lambda j: (0, j)),
                  pl.BlockSpec((1, tn), lambda j: (0, j))],
        out_specs=pl.BlockSpec((8, tn), lambda j: (0, j)),
        out_shape=jax.ShapeDtypeStruct((8, n), F32),
        name="modulation",
    )(cond8, w_mod, b_mod)


def _bias_consts():
    q = np.arange(GRID_W)[None, :]
    c = np.arange(GRID_W)[:, None]
    cs = np.clip(q - WIN_COLS // 2, 0, GRID_W - WIN_COLS)
    inwin = (c >= cs) & (c < cs + WIN_COLS)
    dc = c - q + (WIN_COLS - 1)
    sel = np.zeros((32, GRID_W, GRID_W), np.float32)
    cc, qq = np.nonzero(inwin)
    sel[dc[cc, qq], cc, qq] = 1.0
    negm = np.where(inwin, 0.0, NEG).astype(np.float32)[None]
    sel, negm = (np.concatenate([a, a], axis=-1).reshape(a.shape[0], GRID_W * LANES) for a in (sel, negm))
    return sel, negm


def _bias_kernel(rpb_ref, sel_ref, neg_ref, o_ref):
    o_ref[...] = (jnp.dot(rpb_ref[...], sel_ref[...], preferred_element_type=F32,
                          precision=lax.Precision.HIGHEST) + neg_ref[...]) * LOG2E


def _pair_window(t, n_row_pairs):
    if isinstance(t, int):
        sp = min(max(t - 2, 0), n_row_pairs - WIN_PAIRS)
        var = t if t < 2 else (t - (n_row_pairs - 2) + 3 if t >= n_row_pairs - 2 else 2)
        return sp, var
    sp = jnp.clip(t - 2, 0, n_row_pairs - WIN_PAIRS)
    var = jnp.where(t < 2, t, jnp.where(t >= n_row_pairs - 2, t - (n_row_pairs - 2) + 3, 2))
    return sp, var


N_BIAS_VAR = 5


def _bias_tables(rpb, rows):
    sel, negm = _bias_consts()
    n_dr = 2 * WIN_ROWS - 1
    n_rp = rows // 2
    rpb2 = jnp.pad(rpb.reshape(N_HEADS * n_dr, 2 * WIN_COLS - 1), ((0, 0), (0, 1)))
    bcol = pl.pallas_call(
        _bias_kernel,
        out_shape=jax.ShapeDtypeStruct((N_HEADS * n_dr, GRID_W * LANES), F32),
        name="bias_table",
    )(rpb2, jnp.asarray(sel), jnp.asarray(negm))
    blocks = bcol.reshape(N_PAIRS, 2 * n_dr, GRID_W, LANES)
    idx = np.full((N_BIAS_VAR, 2 * WIN_PAIRS, 2, 2), -1, np.int32)
    for v, t in enumerate([0, 1, 2, n_rp - 2, n_rp - 1]):
        sp, _ = _pair_window(t, n_rp)
        for w in range(2 * WIN_PAIRS):
            krow = 2 * sp + w
            for rr in range(2):
                r = 2 * t + rr
                rs = min(max(r - WIN_ROWS // 2, 0), rows - WIN_ROWS)
                if rs <= krow < rs + WIN_ROWS:
                    idx[v, w, :, rr] = np.arange(2) * n_dr + (krow - r + WIN_ROWS - 1)
    return pl.pallas_call(
        functools.partial(_bias_assemble_kernel, idx=idx),
        grid=(N_PAIRS,),
        in_specs=[pl.BlockSpec((1, 2 * n_dr, GRID_W, LANES), lambda p: (p, 0, 0, 0))],
        out_specs=pl.BlockSpec((1, N_BIAS_VAR, 2 * WIN_PAIRS * GRID_W, 2 * LANES), lambda p: (p, 0, 0, 0)),
        out_shape=jax.ShapeDtypeStruct((N_PAIRS, N_BIAS_VAR, 2 * WIN_PAIRS * GRID_W, 2 * LANES), F32),
        name="bias_assemble",
    )(blocks)


def _bias_assemble_kernel(u_ref, o_ref, *, idx):
    left = lax.broadcasted_iota(jnp.int32, (GRID_W, LANES), 1) < GRID_W
    masked = jnp.full((GRID_W, LANES), NEG, F32)
    n_var, n_win, _, _ = idx.shape
    for v in range(n_var):
        for w in range(n_win):
            for hh in range(2):
                b0, b1 = (u_ref[0, int(i)] if i >= 0 else masked for i in idx[v, w, hh])
                o_ref[0, v, w * GRID_W:(w + 1) * GRID_W, hh * LANES:(hh + 1) * LANES] = jnp.where(left, b0, b1)


def _ctx_kv_kernel(x_ref, sc_ref, sh_ref, g_ref, wk_ref, wv_ref, k_ref, v_ref):
    h = _modulated_norm(x_ref[...], g_ref[...], sc_ref[...], sh_ref[...]).astype(BF16)
    k_ref[...] = jnp.dot(h, wk_ref[0], preferred_element_type=F32).astype(BF16)
    v_ref[...] = jnp.dot(h, wv_ref[...], preferred_element_type=F32).astype(BF16)


def _ctx_kv(ctx2, sc_c, sh_c, pre_g, wqk_bf, w_in_bf):
    m = ctx2.shape[0]

    def full(shape):
        return pl.BlockSpec(shape, lambda j: (0,) * len(shape))

    return pl.pallas_call(
        _ctx_kv_kernel,
        grid=(1,),
        in_specs=[full((m, D_MODEL)), full((1, D_MODEL)), full((1, D_MODEL)), full((1, D_MODEL)),
                  pl.BlockSpec((1, D_MODEL, PART_W), lambda j: (1, 0, 0)),
                  pl.BlockSpec((D_MODEL, PART_W), lambda j: (0, PART_V))],
        out_specs=[full((m, PART_W)), full((m, PART_W))],
        out_shape=[jax.ShapeDtypeStruct((m, PART_W), BF16)] * 2,
        name="ctx_kv",
    )(ctx2, sc_c, sh_c, pre_g, wqk_bf, w_in_bf)


def _split_halves_permutation():
    q4 = HEAD_DIM // 4
    old = np.arange(PART_W).reshape(N_HEADS, 2, 2, q4).transpose(2, 0, 1, 3).reshape(PART_W)
    perm = np.zeros((PART_W, PART_W), np.float32)
    perm[old, np.arange(PART_W)] = 1.0
    return perm


def _permute_kernel(w_ref, perm_ref, o_ref):
    o_ref[0] = jnp.dot(w_ref[...], perm_ref[...], preferred_element_type=F32).astype(BF16)


def _split_halves_qk(w_in_bf):
    return pl.pallas_call(
        _permute_kernel,
        grid=(2,),
        in_specs=[pl.BlockSpec((D_MODEL, PART_W), lambda j: (0, PART_Q + j)),
                  pl.BlockSpec((PART_W, PART_W), lambda j: (0, 0))],
        out_specs=pl.BlockSpec((1, D_MODEL, PART_W), lambda j: (j, 0, 0)),
        out_shape=jax.ShapeDtypeStruct((2, D_MODEL, PART_W), BF16),
        name="permute_qk",
    )(w_in_bf, jnp.asarray(_split_halves_permutation(), BF16))


def _inproj_kernel(x_ref, xp_ref, xn_ref, sc_ref, sh_ref, g_ref, w_ref, wq_ref, wk_ref, cos_ref, sin_ref,
                   cw_ref, cb_ref, p_ref, qt_ref, vt_ref, h_ref, u_ref, *, tiles_per_batch):
    tm = x_ref.shape[0]
    n_half = PART_W // (2 * LANES)
    t = pl.program_id(0) % tiles_per_batch
    for rows, ref in ((slice(0, HALO), xp_ref), (slice(HALO, HALO + tm), x_ref),
                      (slice(HALO + tm, 2 * HALO + tm), xn_ref)):
        h_ref[rows, :] = _modulated_norm(ref[...], g_ref[...], sc_ref[0], sh_ref[0]).astype(BF16)
    tile = slice(HALO, HALO + tm)

    def slab(r, s):
        return r[:, s * LANES:(s + 1) * LANES]

    def rope(r):
        firsts, seconds = [], []
        for s in range(n_half):
            a, b = slab(r, s), slab(r, n_half + s)
            firsts.append(a * cos_ref[...] - b * sin_ref[...])
            seconds.append(b * cos_ref[...] + a * sin_ref[...])
        return firsts + seconds

    def store_transposed(t_ref, s, u):
        ut = u.astype(BF16).T
        for j in range(tm // LANES):
            t_ref[0, j, s * LANES:(s + 1) * LANES, :] = ut[:, j * LANES:(j + 1) * LANES]

    def project(n, rows):
        if n == PART_Q:
            w = wq_ref[0]
        elif n == PART_K:
            w = wk_ref[0]
        else:
            w = w_ref[:, n * PART_W:(n + 1) * PART_W]
        return jnp.dot(h_ref[rows, :], w, preferred_element_type=F32)

    def put(col, val):
        p_ref[:, col * PART_W:(col + 1) * PART_W] = val.astype(BF16)

    b_gate = project(PART_BG, tile)
    u_ref[...] = project(PART_CG, slice(None)) * project(PART_XI, slice(None))
    u_ref[0:HALO, :] = jnp.where(t == 0, 0.0, u_ref[0:HALO, :])
    u_ref[HALO + tm:, :] = jnp.where(t == tiles_per_batch - 1, 0.0, u_ref[HALO + tm:, :])
    z_a = project(PART_ZA, tile)
    for c in range(tm // CONV_ROWS):
        lo = HALO + c * CONV_ROWS
        rows = slice(c * CONV_ROWS, (c + 1) * CONV_ROWS)
        conv = (u_ref[lo - 1:lo - 1 + CONV_ROWS, :] * cw_ref[0:1, :] + u_ref[lo:lo + CONV_ROWS, :] * cw_ref[1:2, :]
                + u_ref[lo + 1:lo + 1 + CONV_ROWS, :] * cw_ref[2:3, :] + cb_ref[...])
        zc = z_a[rows]
        p_ref[rows, OUT_A * PART_W:(OUT_A + 1) * PART_W] = (
            (zc * jax.nn.sigmoid(zc)) * b_gate[rows] * conv).astype(BF16)

    for s, rk in enumerate(rope(project(PART_Q, tile))):
        store_transposed(qt_ref, s, rk * (HEAD_DIM ** -0.5 * LOG2E))
    for s, rk in enumerate(rope(project(PART_K, tile))):
        p_ref[:, OUT_K * PART_W + s * LANES:OUT_K * PART_W + (s + 1) * LANES] = rk.astype(BF16)
    r = project(PART_V, tile)
    for s in range(PART_W // LANES):
        store_transposed(vt_ref, s, slab(r, s))
    z_b = project(PART_ZB, tile)
    put(OUT_SZB, z_b * jax.nn.sigmoid(z_b))
    put(OUT_SGA, jax.nn.sigmoid(project(PART_GA, tile)))
    put(OUT_SGB, jax.nn.sigmoid(project(PART_GB, tile)))


def _inproj(x2, sc, sh, pre_g, w_in_bf, wqk_bf, cos, sin, conv_w, conv_b, batch, seq, tm):
    m = x2.shape[0]
    tpb = seq // tm
    hb = tm // HALO
    t_spec = pl.BlockSpec((1, tm // LANES, PART_W, LANES), lambda i: (i // tpb, i % tpb, 0, 0))
    t_shape = jax.ShapeDtypeStruct((batch, seq // LANES, PART_W, LANES), BF16)

    def resident(shape, index):
        return pl.BlockSpec(shape, lambda i: index, pipeline_mode=pl.Buffered(1))

    return pl.pallas_call(
        functools.partial(_inproj_kernel, tiles_per_batch=tpb),
        grid=(m // tm,),
        in_specs=[pl.BlockSpec((tm, D_MODEL), lambda i: (i, 0)),
                  pl.BlockSpec((HALO, D_MODEL), lambda i: (jnp.maximum(i * hb - 1, 0), 0)),
                  pl.BlockSpec((HALO, D_MODEL), lambda i: (jnp.minimum((i + 1) * hb, m // HALO - 1), 0)),
                  pl.BlockSpec((1, 1, D_MODEL), lambda i: (i // tpb, 0, 0)),
                  pl.BlockSpec((1, 1, D_MODEL), lambda i: (i // tpb, 0, 0)),
                  pl.BlockSpec((1, D_MODEL), lambda i: (0, 0)),
                  resident((D_MODEL, N_PARTS_IN * PART_W), (0, 0)),
                  resident((1, D_MODEL, PART_W), (0, 0, 0)), resident((1, D_MODEL, PART_W), (1, 0, 0)),
                  pl.BlockSpec((tm, LANES), lambda i: (i % tpb, 0)),
                  pl.BlockSpec((tm, LANES), lambda i: (i % tpb, 0)),
                  pl.BlockSpec((3, PART_W), lambda i: (0, 0)),
                  pl.BlockSpec((1, PART_W), lambda i: (0, 0))],
        out_specs=[pl.BlockSpec((tm, N_PARTS_OUT * PART_W), lambda i: (i, 0)), t_spec, t_spec],
        out_shape=[jax.ShapeDtypeStruct((m, N_PARTS_OUT * PART_W), BF16), t_shape, t_shape],
        scratch_shapes=[pltpu.VMEM((tm + 2 * HALO, D_MODEL), BF16),
                        pltpu.VMEM((tm + 2 * HALO, PART_W), F32)],
        compiler_params=pltpu.CompilerParams(dimension_semantics=("arbitrary",)),
        name="in_projection",
    )(x2, x2, x2, sc, sh, pre_g, w_in_bf, wqk_bf, wqk_bf, cos, sin, conv_w, conv_b)


def _attn_kernel(qta_ref, qtb_ref, ka_ref, kb_ref, vt_ref, kca_ref, kcb_ref, vct_ref, bias_ref,
                 o_ref, *stage_refs, n_row_pairs):
    low = lax.broadcasted_iota(jnp.int32, (LANES, LANES), 0) < HEAD_DIM
    nb = WIN_PAIRS * LANES
    last = n_row_pairs - 1
    head_of_dim = (lax.broadcasted_iota(jnp.int32, (2 * LANES, LANES), 0) % LANES) // (HEAD_DIM // 2)
    first_head = 2 * (pl.program_id(0) % 2)
    sel0 = head_of_dim == first_head
    sel1 = head_of_dim == first_head + 1

    def scores(t, slot):
        sp, var = _pair_window(t, n_row_pairs)
        qt = jnp.concatenate([qta_ref[0, t], qtb_ref[0, t]], axis=0)
        zero = jnp.zeros_like(qt)
        rhs = jnp.concatenate([jnp.where(sel0, qt, zero), jnp.where(sel1, qt, zero)], axis=1)
        rows = pl.ds(pl.multiple_of(sp * LANES, LANES), nb)
        kwin = jnp.concatenate([ka_ref[rows, :], kb_ref[rows, :]], axis=1)
        s_nb = jnp.dot(kwin, rhs, preferred_element_type=F32) + bias_ref[0, var]
        s_refs[slot][:nb, :] = s_nb
        m_nb = key_max(s_nb)
        yield
        kc = jnp.concatenate([kca_ref[0], kcb_ref[0]], axis=1)
        s_cx = jnp.dot(kc, rhs, preferred_element_type=F32)
        s_refs[slot][nb:, :] = s_cx
        m_refs[slot][...] = jnp.max(jnp.maximum(m_nb, key_max(s_cx)), axis=0, keepdims=True)

    def key_max(s):
        m = s[:RED_ROWS]
        for c in range(1, s.shape[0] // RED_ROWS):
            m = jnp.maximum(m, s[c * RED_ROWS:(c + 1) * RED_ROWS])
        return m

    def softmax(slot):
        s_ref, p_ref = s_refs[slot], p_refs[slot]
        chunks = [slice(c * RED_ROWS, (c + 1) * RED_ROWS) for c in range(s_ref.shape[0] // RED_ROWS)]
        m = m_refs[slot][...]
        for k, c in enumerate(chunks):
            p_ref[c, :] = jnp.exp2(s_ref[c, :] - m).astype(BF16)
            if k % 4 == 3:
                yield

    def values(t, slot):
        sp, _ = _pair_window(t, n_row_pairs)
        p = p_refs[slot][...]
        vext = jnp.concatenate([vt_ref[0, sp + g] for g in range(WIN_PAIRS)] + [vct_ref[0]], axis=1)
        vext = jnp.concatenate([vext, jnp.ones((SUM_ROWS, p.shape[0]), BF16)], axis=0)
        ot = jnp.dot(vext, p, preferred_element_type=F32)
        yield
        ot = ot[:LANES] * (1.0 / ot[LANES:LANES + 1])
        out = jnp.where(low, ot[:, :LANES], ot[:, LANES:]).T
        o_ref[pl.ds(pl.multiple_of(t * LANES, LANES), LANES), :] = out.astype(BF16)

    def run(*stages):
        for stage in stages:
            for _ in stage:
                pass

    def interleave(sm, va, sc):
        next(sm)
        next(sc)
        next(sm)
        run(sc)
        next(sm)
        run(sm, va)

    g = ATTN_GROUP
    s_refs, p_refs, m_refs = (stage_refs[k * 2 * g:(k + 1) * 2 * g] for k in range(3))
    for j in range(g):
        run(scores(j, j))

    def group(t, cur, nxt):
        for j in range(g):
            interleave(softmax(cur + j), values(t + j, cur + j),
                       scores(jnp.minimum(t + j + g, last), nxt + j))

    def body(i, carry):
        t = 2 * g * i
        group(t, 0, g)
        group(t + g, g, 0)
        return carry

    lax.fori_loop(0, n_row_pairs // (2 * g), body, 0)


def _attention(p5, qt, vt, kc, vct, bias, batch, seq):
    n_ctx = kc.shape[1]
    n_keys = WIN_PAIRS * LANES + n_ctx
    n_slots = 2 * ATTN_GROUP
    assert (seq // LANES) % n_slots == 0
    spp = PART_W // LANES
    half = spp // 2

    def t_spec(slab):
        return pl.BlockSpec((1, seq // LANES, LANES, LANES), lambda hp, b: (b, 0, slab(hp), 0))

    return pl.pallas_call(
        functools.partial(_attn_kernel, n_row_pairs=seq // LANES),
        grid=(N_PAIRS, batch),
        in_specs=[t_spec(lambda hp: hp // 2), t_spec(lambda hp: half + hp // 2),
                  pl.BlockSpec((seq, LANES), lambda hp, b: (b, OUT_K * spp + hp // 2)),
                  pl.BlockSpec((seq, LANES), lambda hp, b: (b, OUT_K * spp + half + hp // 2)),
                  t_spec(lambda hp: hp),
                  pl.BlockSpec((1, n_ctx, LANES), lambda hp, b: (b, 0, hp // 2)),
                  pl.BlockSpec((1, n_ctx, LANES), lambda hp, b: (b, 0, half + hp // 2)),
                  pl.BlockSpec((1, LANES, n_ctx), lambda hp, b: (b, hp, 0)),
                  pl.BlockSpec((1, N_BIAS_VAR, WIN_PAIRS * LANES, 2 * LANES),
                               lambda hp, b: (hp, 0, 0, 0))],
        out_specs=pl.BlockSpec((seq, LANES), lambda hp, b: (b, hp)),
        out_shape=jax.ShapeDtypeStruct((batch * seq, N_HEADS * HEAD_DIM), BF16),
        scratch_shapes=[pltpu.VMEM((n_keys, 2 * LANES), F32)] * n_slots
        + [pltpu.VMEM((n_keys, 2 * LANES), BF16)] * n_slots + [pltpu.VMEM((1, 2 * LANES), F32)] * n_slots,
        compiler_params=pltpu.CompilerParams(dimension_semantics=("arbitrary", "arbitrary")),
        name="attention",
    )(qt, qt, p5, p5, vt, kc, kc, vct, bias)


def _mixer_kernel(a_ref, szb_ref, sga_ref, sgb_ref, at_ref, x_ref, gt_ref, pg_ref,
                  woc_ref, woa_ref, wo_ref, o_ref):
    y_a = jnp.dot(a_ref[...], woc_ref[...], preferred_element_type=F32)
    bb = szb_ref[...].astype(F32) * at_ref[...].astype(F32)
    y_b = jnp.dot(bb.astype(BF16), woa_ref[...], preferred_element_type=F32)
    merged = sga_ref[...].astype(F32) * y_a + sgb_ref[...].astype(F32) * y_b
    y = jnp.dot(merged.astype(BF16), wo_ref[...], preferred_element_type=F32)
    ms = jnp.mean(y * y, axis=-1, keepdims=True)
    o_ref[...] = x_ref[...] + gt_ref[0] * ((y * lax.rsqrt(ms + EPS)) * pg_ref[...])


def _mixer(p5, attn, x2, gt, post_g, woc, woa, wo, seq, tm):
    m = x2.shape[0]
    tpb = seq // tm

    def part(k):
        return pl.BlockSpec((tm, PART_W), lambda i: (i, k))

    def full(shape):
        return pl.BlockSpec(shape, lambda i: (0,) * len(shape))

    return pl.pallas_call(
        _mixer_kernel,
        grid=(m // tm,),
        in_specs=[part(OUT_A), part(OUT_SZB), part(OUT_SGA), part(OUT_SGB),
                  pl.BlockSpec((tm, PART_W), lambda i: (i, 0)),
                  pl.BlockSpec((tm, D_MODEL), lambda i: (i, 0)),
                  pl.BlockSpec((1, 1, D_MODEL), lambda i: (i // tpb, 0, 0)),
                  full((1, D_MODEL)),
                  full((PART_W, D_MODEL)), full((PART_W, D_MODEL)), full((D_MODEL, D_MODEL))],
        out_specs=pl.BlockSpec((tm, D_MODEL), lambda i: (i, 0)),
        out_shape=jax.ShapeDtypeStruct((m, D_MODEL), F32),
        compiler_params=pltpu.CompilerParams(dimension_semantics=("arbitrary",)),
        name="mixer_out",
    )(p5, p5, p5, p5, attn, x2, gt, post_g, woc, woa, wo)


def _rope_tables(seq):
    rows = seq // GRID_W
    half = HEAD_DIM // 2
    q4 = HEAD_DIM // 4
    inv = ROPE_BASE ** (-jnp.arange(0, half, 2, dtype=F32) / half)
    ang_r = jnp.arange(rows, dtype=F32)[:, None] * inv
    ang_c = jnp.arange(GRID_W, dtype=F32)[:, None] * inv

    def table(fn):
        r = jnp.broadcast_to(fn(ang_r)[:, None, :], (rows, GRID_W, q4))
        c = jnp.broadcast_to(fn(ang_c)[None, :, :], (rows, GRID_W, q4))
        return jnp.tile(jnp.concatenate([r, c], axis=-1), (1, 1, LANES // half)).reshape(seq, LANES)

    return table(jnp.cos), table(jnp.sin)


def _layer(x, c, ctx, c_ctx, w_mod, b_mod, pre_g, post_g, w_in, conv_w, conv_b, rpb,
           w_out_conv, w_out_attn, w_o):
    batch, seq, _ = x.shape
    n_ctx = ctx.shape[1]
    rows = seq // GRID_W
    assert rows >= 2 * WIN_PAIRS and seq % LANES == 0 and batch <= 7

    cond8 = jnp.zeros((8, D_MODEL), F32).at[:batch].set(c).at[batch].set(c_ctx)
    mod = _modulation(cond8, w_mod, b_mod.reshape(1, -1))
    sh, sc, gt = (mod[:batch, k * D_MODEL:(k + 1) * D_MODEL].reshape(batch, 1, D_MODEL) for k in range(3))
    sh_c, sc_c = (mod[batch:batch + 1, k * D_MODEL:(k + 1) * D_MODEL] for k in range(2))

    pre_g2 = pre_g.reshape(1, D_MODEL)
    w_in_bf = w_in.astype(BF16)
    wqk_bf = _split_halves_qk(w_in_bf)
    kc, vc = _ctx_kv(ctx.reshape(batch * n_ctx, D_MODEL), sc_c, sh_c, pre_g2, wqk_bf, w_in_bf)
    kc = kc.reshape(batch, n_ctx, PART_W)
    vct = jnp.transpose(vc.reshape(batch, n_ctx, PART_W), (0, 2, 1))

    x2 = x.reshape(batch * seq, D_MODEL)
    cos, sin = _rope_tables(seq)
    tm_in = min(512, seq)
    p5, qt, vt = _inproj(x2, sc, sh, pre_g2, w_in_bf, wqk_bf, cos, sin, conv_w,
                         conv_b.reshape(1, -1), batch, seq, tm_in)

    bias = _bias_tables(rpb, rows)
    attn = _attention(p5, qt, vt, kc, vct, bias, batch, seq)

    tm_out = min(1024, seq)
    out = _mixer(p5, attn, x2, gt, post_g.reshape(1, -1),
                 w_out_conv.astype(BF16), w_out_attn.astype(BF16), w_o.astype(BF16), seq, tm_out)
    return out.reshape(batch, seq, D_MODEL)


def kernel(x, c, ctx, c_ctx, w_mod, b_mod, pre_g, post_g, w_in, conv_w, conv_b, rpb,
           w_out_conv, w_out_attn, w_o):
    depth = w_mod.shape[0]
    assert depth == 1, "context stream update between layers is not implemented"
    return _layer(x, c, ctx, c_ctx, w_mod[0], b_mod[0], pre_g[0], post_g[0], w_in[0], conv_w[0],
                  conv_b[0], rpb[0], w_out_conv[0], w_out_attn[0], w_o[0])
```

```python
import functools

import numpy as np
import jax
import jax.numpy as jnp
from jax import lax
from jax.experimental import pallas as pl
from jax.experimental.pallas import tpu as pltpu

D_MODEL = 1024
GRID_W = 64
N_HEADS = 16
HEAD_DIM = 64
WIN_ROWS = 8
WIN_COLS = 16
ROPE_BASE = 10000.0
EPS = 1e-6
PART_W = 1024
N_PARTS_IN = 10
PART_BG, PART_CG, PART_XI, PART_ZA, PART_Q, PART_K, PART_V, PART_ZB, PART_GA, PART_GB = range(N_PARTS_IN)
OUT_A, OUT_K, OUT_SZB, OUT_SGA, OUT_SGB = range(5)
N_PARTS_OUT = 5
CONV_ROWS = 64
HALO = 16
LANES = 128
N_PAIRS = N_HEADS // 2
WIN_PAIRS = 5
NEG = -1e30
RED_ROWS = 64
ATTN_GROUP = 4
SUM_ROWS = 16
LOG2E = 1.4426950408889634
BF16 = jnp.bfloat16
F32 = jnp.float32


def _modulated_norm(x, g, sc, sh):
    ms = jnp.mean(x * x, axis=-1, keepdims=True)
    return x * lax.rsqrt(ms + EPS) * (g * (1.0 + sc)) + sh


def _modulation_kernel(c_ref, w_ref, b_ref, o_ref):
    c = c_ref[...]
    o_ref[...] = jnp.dot(c * jax.nn.sigmoid(c), w_ref[...], preferred_element_type=F32,
                         precision=lax.Precision.HIGHEST) + b_ref[...]


def _modulation(cond8, w_mod, b_mod):
    n = w_mod.shape[1]
    tn = 768
    return pl.pallas_call(
        _modulation_kernel,
        grid=(n // tn,),
        in_specs=[pl.BlockSpec((8, D_MODEL), lambda j: (0, 0)),
                  pl.BlockSpec((D_MODEL, tn), lambda j: (0, j)),
                  pl.BlockSpec((1, tn), lambda j: (0, j))],
        out_specs=pl.BlockSpec((8, tn), lambda j: (0, j)),
        out_shape=jax.ShapeDtypeStruct((8, n), F32),
        name="modulation",
    )(cond8, w_mod, b_mod)


def _bias_consts():
    q = np.arange(GRID_W)[None, :]
    c = np.arange(GRID_W)[:, None]
    cs = np.clip(q - WIN_COLS // 2, 0, GRID_W - WIN_COLS)
    inwin = (c >= cs) & (c < cs + WIN_COLS)
    dc = c - q + (WIN_COLS - 1)
    sel = np.zeros((32, GRID_W, GRID_W), np.float32)
    cc, qq = np.nonzero(inwin)
    sel[dc[cc, qq], cc, qq] = 1.0
    negm = np.where(inwin, 0.0, NEG).astype(np.float32)[None]
    sel, negm = (np.concatenate([a, a], axis=-1).reshape(a.shape[0], GRID_W * LANES) for a in (sel, negm))
    return sel, negm


def _bias_kernel(rpb_ref, sel_ref, neg_ref, o_ref):
    o_ref[...] = (jnp.dot(rpb_ref[...], sel_ref[...], preferred_element_type=F32,
                          precision=lax.Precision.HIGHEST) + neg_ref[...]) * LOG2E


def _pair_window(t, n_row_pairs):
    if isinstance(t, int):
        sp = min(max(t - 2, 0), n_row_pairs - WIN_PAIRS)
        var = t if t < 2 else (t - (n_row_pairs - 2) + 3 if t >= n_row_pairs - 2 else 2)
        return sp, var
    sp = jnp.clip(t - 2, 0, n_row_pairs - WIN_PAIRS)
    var = jnp.where(t < 2, t, jnp.where(t >= n_row_pairs - 2, t - (n_row_pairs - 2) + 3, 2))
    return sp, var


N_BIAS_VAR = 5


def _bias_tables(rpb, rows):
    sel, negm = _bias_consts()
    n_dr = 2 * WIN_ROWS - 1
    n_rp = rows // 2
    rpb2 = jnp.pad(rpb.reshape(N_HEADS * n_dr, 2 * WIN_COLS - 1), ((0, 0), (0, 1)))
    bcol = pl.pallas_call(
        _bias_kernel,
        out_shape=jax.ShapeDtypeStruct((N_HEADS * n_dr, GRID_W * LANES), F32),
        name="bias_table",
    )(rpb2, jnp.asarray(sel), jnp.asarray(negm))
    blocks = bcol.reshape(N_PAIRS, 2 * n_dr, GRID_W, LANES)
    idx = np.full((N_BIAS_VAR, 2 * WIN_PAIRS, 2, 2), -1, np.int32)
    for v, t in enumerate([0, 1, 2, n_rp - 2, n_rp - 1]):
        sp, _ = _pair_window(t, n_rp)
        for w in range(2 * WIN_PAIRS):
            krow = 2 * sp + w
            for rr in range(2):
                r = 2 * t + rr
                rs = min(max(r - WIN_ROWS // 2, 0), rows - WIN_ROWS)
                if rs <= krow < rs + WIN_ROWS:
                    idx[v, w, :, rr] = np.arange(2) * n_dr + (krow - r + WIN_ROWS - 1)
    return pl.pallas_call(
        functools.partial(_bias_assemble_kernel, idx=idx),
        grid=(N_PAIRS,),
        in_specs=[pl.BlockSpec((1, 2 * n_dr, GRID_W, LANES), lambda p: (p, 0, 0, 0))],
        out_specs=pl.BlockSpec((1, N_BIAS_VAR, 2 * WIN_PAIRS * GRID_W, 2 * LANES), lambda p: (p, 0, 0, 0)),
        out_shape=jax.ShapeDtypeStruct((N_PAIRS, N_BIAS_VAR, 2 * WIN_PAIRS * GRID_W, 2 * LANES), F32),
        name="bias_assemble",
    )(blocks)


def _bias_assemble_kernel(u_ref, o_ref, *, idx):
    left = lax.broadcasted_iota(jnp.int32, (GRID_W, LANES), 1) < GRID_W
    masked = jnp.full((GRID_W, LANES), NEG, F32)
    n_var, n_win, _, _ = idx.shape
    for v in range(n_var):
        for w in range(n_win):
            for hh in range(2):
                b0, b1 = (u_ref[0, int(i)] if i >= 0 else masked for i in idx[v, w, hh])
                o_ref[0, v, w * GRID_W:(w + 1) * GRID_W, hh * LANES:(hh + 1) * LANES] = jnp.where(left, b0, b1)


def _ctx_kv_kernel(x_ref, sc_ref, sh_ref, g_ref, wk_ref, wv_ref, k_ref, v_ref):
    h = _modulated_norm(x_ref[...], g_ref[...], sc_ref[...], sh_ref[...]).astype(BF16)
    k_ref[...] = jnp.dot(h, wk_ref[0], preferred_element_type=F32).astype(BF16)
    v_ref[...] = jnp.dot(h, wv_ref[...], preferred_element_type=F32).astype(BF16)


def _ctx_kv(ctx2, sc_c, sh_c, pre_g, wqk_bf, w_in_bf):
    m = ctx2.shape[0]

    def full(shape):
        return pl.BlockSpec(shape, lambda j: (0,) * len(shape))

    return pl.pallas_call(
        _ctx_kv_kernel,
        grid=(1,),
        in_specs=[full((m, D_MODEL)), full((1, D_MODEL)), full((1, D_MODEL)), full((1, D_MODEL)),
                  pl.BlockSpec((1, D_MODEL, PART_W), lambda j: (1, 0, 0)),
                  pl.BlockSpec((D_MODEL, PART_W), lambda j: (0, PART_V))],
        out_specs=[full((m, PART_W)), full((m, PART_W))],
        out_shape=[jax.ShapeDtypeStruct((m, PART_W), BF16)] * 2,
        name="ctx_kv",
    )(ctx2, sc_c, sh_c, pre_g, wqk_bf, w_in_bf)


def _split_halves_permutation():
    q4 = HEAD_DIM // 4
    old = np.arange(PART_W).reshape(N_HEADS, 2, 2, q4).transpose(2, 0, 1, 3).reshape(PART_W)
    perm = np.zeros((PART_W, PART_W), np.float32)
    perm[old, np.arange(PART_W)] = 1.0
    return perm


def _permute_kernel(w_ref, perm_ref, o_ref):
    o_ref[0] = jnp.dot(w_ref[...], perm_ref[...], preferred_element_type=F32).astype(BF16)


def _split_halves_qk(w_in_bf):
    return pl.pallas_call(
        _permute_kernel,
        grid=(2,),
        in_specs=[pl.BlockSpec((D_MODEL, PART_W), lambda j: (0, PART_Q + j)),
                  pl.BlockSpec((PART_W, PART_W), lambda j: (0, 0))],
        out_specs=pl.BlockSpec((1, D_MODEL, PART_W), lambda j: (j, 0, 0)),
        out_shape=jax.ShapeDtypeStruct((2, D_MODEL, PART_W), BF16),
        name="permute_qk",
    )(w_in_bf, jnp.asarray(_split_halves_permutation(), BF16))


def _inproj_kernel(x0_ref, x_ref, xp_ref, xn_ref, sc_ref, sh_ref, g_ref, w_ref, wq_ref, wk_ref,
                   cos_ref, sin_ref, cw_ref, cb_ref, p_ref, qt_ref, vt_ref, ha_ref, hb_ref, u_ref,
                   *, tiles_per_batch):
    tm = x_ref.shape[0]
    i = pl.program_id(0)

    def normed(x):
        return _modulated_norm(x, g_ref[...], sc_ref[0], sh_ref[0]).astype(BF16)

    def fill(h_ref, before, main, after):
        h_ref[0:HALO, :] = normed(before)
        h_ref[HALO:HALO + tm, :] = normed(main)
        h_ref[HALO + tm:, :] = normed(after)

    @pl.when(i == 0)
    def _():
        fill(ha_ref, x0_ref[0:HALO, :], x0_ref[...], x_ref[0:HALO, :])

    def step(h_ref, next_h_ref):
        fill(next_h_ref, xp_ref[...], x_ref[...], xn_ref[...])
        _inproj_tile(h_ref, w_ref, wq_ref, wk_ref, cos_ref, sin_ref, cw_ref, cb_ref, p_ref, qt_ref, vt_ref,
                     u_ref, i % tiles_per_batch, tiles_per_batch)

    @pl.when(i % 2 == 0)
    def _():
        step(ha_ref, hb_ref)

    @pl.when(i % 2 == 1)
    def _():
        step(hb_ref, ha_ref)


def _inproj_tile(h_ref, w_ref, wq_ref, wk_ref, cos_ref, sin_ref, cw_ref, cb_ref, p_ref, qt_ref, vt_ref,
                 u_ref, t, tiles_per_batch):
    tm = p_ref.shape[0]
    n_half = PART_W // (2 * LANES)
    tile = slice(HALO, HALO + tm)

    def slab(r, s):
        return r[:, s * LANES:(s + 1) * LANES]

    def rope(r):
        firsts, seconds = [], []
        for s in range(n_half):
            a, b = slab(r, s), slab(r, n_half + s)
            firsts.append(a * cos_ref[...] - b * sin_ref[...])
            seconds.append(b * cos_ref[...] + a * sin_ref[...])
        return firsts + seconds

    def store_transposed(t_ref, s, u):
        ut = u.astype(BF16).T
        for j in range(tm // LANES):
            t_ref[0, j, s * LANES:(s + 1) * LANES, :] = ut[:, j * LANES:(j + 1) * LANES]

    def project(n, rows):
        if n == PART_Q:
            w = wq_ref[0]
        elif n == PART_K:
            w = wk_ref[0]
        else:
            w = w_ref[:, n * PART_W:(n + 1) * PART_W]
        return jnp.dot(h_ref[rows, :], w, preferred_element_type=F32)

    def put(col, val):
        p_ref[:, col * PART_W:(col + 1) * PART_W] = val.astype(BF16)

    b_gate = project(PART_BG, tile)
    u_ref[...] = project(PART_CG, slice(None)) * project(PART_XI, slice(None))
    u_ref[0:HALO, :] = jnp.where(t == 0, 0.0, u_ref[0:HALO, :])
    u_ref[HALO + tm:, :] = jnp.where(t == tiles_per_batch - 1, 0.0, u_ref[HALO + tm:, :])
    z_a = project(PART_ZA, tile)
    for c in range(tm // CONV_ROWS):
        lo = HALO + c * CONV_ROWS
        rows = slice(c * CONV_ROWS, (c + 1) * CONV_ROWS)
        conv = (u_ref[lo - 1:lo - 1 + CONV_ROWS, :] * cw_ref[0:1, :] + u_ref[lo:lo + CONV_ROWS, :] * cw_ref[1:2, :]
                + u_ref[lo + 1:lo + 1 + CONV_ROWS, :] * cw_ref[2:3, :] + cb_ref[...])
        zc = z_a[rows]
        p_ref[rows, OUT_A * PART_W:(OUT_A + 1) * PART_W] = (
            (zc * jax.nn.sigmoid(zc)) * b_gate[rows] * conv).astype(BF16)

    for s, rk in enumerate(rope(project(PART_Q, tile))):
        store_transposed(qt_ref, s, rk * (HEAD_DIM ** -0.5 * LOG2E))
    for s, rk in enumerate(rope(project(PART_K, tile))):
        p_ref[:, OUT_K * PART_W + s * LANES:OUT_K * PART_W + (s + 1) * LANES] = rk.astype(BF16)
    r = project(PART_V, tile)
    for s in range(PART_W // LANES):
        store_transposed(vt_ref, s, slab(r, s))
    z_b = project(PART_ZB, tile)
    put(OUT_SZB, z_b * jax.nn.sigmoid(z_b))
    put(OUT_SGA, jax.nn.sigmoid(project(PART_GA, tile)))
    put(OUT_SGB, jax.nn.sigmoid(project(PART_GB, tile)))


def _inproj(x2, sc, sh, pre_g, w_in_bf, wqk_bf, cos, sin, conv_w, conv_b, batch, seq, tm):
    m = x2.shape[0]
    tpb = seq // tm
    n_tiles = m // tm
    assert tpb >= 2
    hb = tm // HALO
    t_spec = pl.BlockSpec((1, tm // LANES, PART_W, LANES), lambda i: (i // tpb, i % tpb, 0, 0))
    t_shape = jax.ShapeDtypeStruct((batch, seq // LANES, PART_W, LANES), BF16)

    def resident(shape, index):
        return pl.BlockSpec(shape, lambda i: index, pipeline_mode=pl.Buffered(1))

    def nxt(i):
        return jnp.minimum(i + 1, n_tiles - 1)

    return pl.pallas_call(
        functools.partial(_inproj_kernel, tiles_per_batch=tpb),
        grid=(n_tiles,),
        in_specs=[resident((tm, D_MODEL), (0, 0)),
                  pl.BlockSpec((tm, D_MODEL), lambda i: (nxt(i), 0)),
                  pl.BlockSpec((HALO, D_MODEL), lambda i: (nxt(i) * hb - 1, 0)),
                  pl.BlockSpec((HALO, D_MODEL), lambda i: (jnp.minimum((nxt(i) + 1) * hb, m // HALO - 1), 0)),
                  pl.BlockSpec((1, 1, D_MODEL), lambda i: (nxt(i) // tpb, 0, 0)),
                  pl.BlockSpec((1, 1, D_MODEL), lambda i: (nxt(i) // tpb, 0, 0)),
                  pl.BlockSpec((1, D_MODEL), lambda i: (0, 0)),
                  resident((D_MODEL, N_PARTS_IN * PART_W), (0, 0)),
                  resident((1, D_MODEL, PART_W), (0, 0, 0)), resident((1, D_MODEL, PART_W), (1, 0, 0)),
                  pl.BlockSpec((tm, LANES), lambda i: (i % tpb, 0)),
                  pl.BlockSpec((tm, LANES), lambda i: (i % tpb, 0)),
                  pl.BlockSpec((3, PART_W), lambda i: (0, 0)),
                  pl.BlockSpec((1, PART_W), lambda i: (0, 0))],
        out_specs=[pl.BlockSpec((tm, N_PARTS_OUT * PART_W), lambda i: (i, 0)), t_spec, t_spec],
        out_shape=[jax.ShapeDtypeStruct((m, N_PARTS_OUT * PART_W), BF16), t_shape, t_shape],
        scratch_shapes=[pltpu.VMEM((tm + 2 * HALO, D_MODEL), BF16),
                        pltpu.VMEM((tm + 2 * HALO, D_MODEL), BF16),
                        pltpu.VMEM((tm + 2 * HALO, PART_W), F32)],
        compiler_params=pltpu.CompilerParams(dimension_semantics=("arbitrary",)),
        name="in_projection",
    )(x2, x2, x2, x2, sc, sh, pre_g, w_in_bf, wqk_bf, wqk_bf, cos, sin, conv_w, conv_b)


def _attn_kernel(qta_ref, qtb_ref, ka_ref, kb_ref, vt_ref, kca_ref, kcb_ref, vct_ref, bias_ref,
                 o_ref, *stage_refs, n_row_pairs):
    low = lax.broadcasted_iota(jnp.int32, (LANES, LANES), 0) < HEAD_DIM
    nb = WIN_PAIRS * LANES
    last = n_row_pairs - 1
    head_of_dim = (lax.broadcasted_iota(jnp.int32, (2 * LANES, LANES), 0) % LANES) // (HEAD_DIM // 2)
    first_head = 2 * (pl.program_id(0) % 2)
    sel0 = head_of_dim == first_head
    sel1 = head_of_dim == first_head + 1

    def scores(t, slot):
        sp, var = _pair_window(t, n_row_pairs)
        qt = jnp.concatenate([qta_ref[0, t], qtb_ref[0, t]], axis=0)
        zero = jnp.zeros_like(qt)
        rhs = jnp.concatenate([jnp.where(sel0, qt, zero), jnp.where(sel1, qt, zero)], axis=1)
        rows = pl.ds(pl.multiple_of(sp * LANES, LANES), nb)
        kwin = jnp.concatenate([ka_ref[rows, :], kb_ref[rows, :]], axis=1)
        s_refs[slot][:nb, :] = jnp.dot(kwin, rhs, preferred_element_type=F32) + bias_ref[0, var]
        yield
        kc = jnp.concatenate([kca_ref[0], kcb_ref[0]], axis=1)
        s_refs[slot][nb:, :] = jnp.dot(kc, rhs, preferred_element_type=F32)

    def softmax(slot):
        s_ref, p_ref = s_refs[slot], p_refs[slot]
        chunks = [slice(c * RED_ROWS, (c + 1) * RED_ROWS) for c in range(s_ref.shape[0] // RED_ROWS)]
        m = s_ref[chunks[0], :]
        for c in chunks[1:]:
            m = jnp.maximum(m, s_ref[c, :])
        m = jnp.max(m, axis=0, keepdims=True)
        yield
        for k, c in enumerate(chunks):
            p_ref[c, :] = jnp.exp2(s_ref[c, :] - m).astype(BF16)
            if k % 4 == 3:
                yield

    def values(t, slot):
        sp, _ = _pair_window(t, n_row_pairs)
        p = p_refs[slot][...]
        vext = jnp.concatenate([vt_ref[0, sp + g] for g in range(WIN_PAIRS)] + [vct_ref[0]], axis=1)
        vext = jnp.concatenate([vext, jnp.ones((SUM_ROWS, p.shape[0]), BF16)], axis=0)
        ot = jnp.dot(vext, p, preferred_element_type=F32)
        yield
        ot = ot[:LANES] * (1.0 / ot[LANES:LANES + 1])
        out = jnp.where(low, ot[:, :LANES], ot[:, LANES:]).T
        o_ref[pl.ds(pl.multiple_of(t * LANES, LANES), LANES), :] = out.astype(BF16)

    def run(*stages):
        for stage in stages:
            for _ in stage:
                pass

    def interleave(sm, va, sc):
        next(sm)
        next(sc)
        next(sm)
        next(sm)
        run(sc)
        next(sm)
        run(sm, va)

    g = ATTN_GROUP
    s_refs, p_refs = stage_refs[:2 * g], stage_refs[2 * g:]
    for j in range(g):
        run(scores(j, j))

    def group(t, cur, nxt):
        for j in range(g):
            interleave(softmax(cur + j), values(t + j, cur + j),
                       scores(jnp.minimum(t + j + g, last), nxt + j))

    def body(i, carry):
        t = 2 * g * i
        group(t, 0, g)
        group(t + g, g, 0)
        return carry

    lax.fori_loop(0, n_row_pairs // (2 * g), body, 0)


def _attention(p5, qt, vt, kc, vct, bias, batch, seq):
    n_ctx = kc.shape[1]
    n_keys = WIN_PAIRS * LANES + n_ctx
    n_slots = 2 * ATTN_GROUP
    assert (seq // LANES) % n_slots == 0
    spp = PART_W // LANES
    half = spp // 2

    def t_spec(slab):
        return pl.BlockSpec((1, seq // LANES, LANES, LANES), lambda hp, b: (b, 0, slab(hp), 0))

    return pl.pallas_call(
        functools.partial(_attn_kernel, n_row_pairs=seq // LANES),
        grid=(N_PAIRS, batch),
        in_specs=[t_spec(lambda hp: hp // 2), t_spec(lambda hp: half + hp // 2),
                  pl.BlockSpec((seq, LANES), lambda hp, b: (b, OUT_K * spp + hp // 2)),
                  pl.BlockSpec((seq, LANES), lambda hp, b: (b, OUT_K * spp + half + hp // 2)),
                  t_spec(lambda hp: hp),
                  pl.BlockSpec((1, n_ctx, LANES), lambda hp, b: (b, 0, hp // 2)),
                  pl.BlockSpec((1, n_ctx, LANES), lambda hp, b: (b, 0, half + hp // 2)),
                  pl.BlockSpec((1, LANES, n_ctx), lambda hp, b: (b, hp, 0)),
                  pl.BlockSpec((1, N_BIAS_VAR, WIN_PAIRS * LANES, 2 * LANES),
                               lambda hp, b: (hp, 0, 0, 0))],
        out_specs=pl.BlockSpec((seq, LANES), lambda hp, b: (b, hp)),
        out_shape=jax.ShapeDtypeStruct((batch * seq, N_HEADS * HEAD_DIM), BF16),
        scratch_shapes=[pltpu.VMEM((n_keys, 2 * LANES), F32)] * n_slots
        + [pltpu.VMEM((n_keys, 2 * LANES), BF16)] * n_slots,
        compiler_params=pltpu.CompilerParams(dimension_semantics=("arbitrary", "arbitrary")),
        name="attention",
    )(qt, qt, p5, p5, vt, kc, kc, vct, bias)


def _mixer_kernel(a_ref, szb_ref, sga_ref, sgb_ref, at_ref, x_ref, gt_ref, pg_ref,
                  woc_ref, woa_ref, wo_ref, o_ref):
    y_a = jnp.dot(a_ref[...], woc_ref[...], preferred_element_type=F32)
    bb = szb_ref[...].astype(F32) * at_ref[...].astype(F32)
    y_b = jnp.dot(bb.astype(BF16), woa_ref[...], preferred_element_type=F32)
    merged = sga_ref[...].astype(F32) * y_a + sgb_ref[...].astype(F32) * y_b
    y = jnp.dot(merged.astype(BF16), wo_ref[...], preferred_element_type=F32)
    ms = jnp.mean(y * y, axis=-1, keepdims=True)
    o_ref[...] = x_ref[...] + gt_ref[0] * ((y * lax.rsqrt(ms + EPS)) * pg_ref[...])


def _mixer(p5, attn, x2, gt, post_g, woc, woa, wo, seq, tm):
    m = x2.shape[0]
    tpb = seq // tm

    def part(k):
        return pl.BlockSpec((tm, PART_W), lambda i: (i, k))

    def full(shape):
        return pl.BlockSpec(shape, lambda i: (0,) * len(shape))

    return pl.pallas_call(
        _mixer_kernel,
        grid=(m // tm,),
        in_specs=[part(OUT_A), part(OUT_SZB), part(OUT_SGA), part(OUT_SGB),
                  pl.BlockSpec((tm, PART_W), lambda i: (i, 0)),
                  pl.BlockSpec((tm, D_MODEL), lambda i: (i, 0)),
                  pl.BlockSpec((1, 1, D_MODEL), lambda i: (i // tpb, 0, 0)),
                  full((1, D_MODEL)),
                  full((PART_W, D_MODEL)), full((PART_W, D_MODEL)), full((D_MODEL, D_MODEL))],
        out_specs=pl.BlockSpec((tm, D_MODEL), lambda i: (i, 0)),
        out_shape=jax.ShapeDtypeStruct((m, D_MODEL), F32),
        compiler_params=pltpu.CompilerParams(dimension_semantics=("arbitrary",)),
        name="mixer_out",
    )(p5, p5, p5, p5, attn, x2, gt, post_g, woc, woa, wo)


def _rope_tables(seq):
    rows = seq // GRID_W
    half = HEAD_DIM // 2
    q4 = HEAD_DIM // 4
    inv = ROPE_BASE ** (-jnp.arange(0, half, 2, dtype=F32) / half)
    ang_r = jnp.arange(rows, dtype=F32)[:, None] * inv
    ang_c = jnp.arange(GRID_W, dtype=F32)[:, None] * inv

    lane = np.arange(LANES)
    is_col = jnp.asarray((lane // q4) % 2 == 1)[None, None, :]

    def table(fn):
        r = fn(ang_r)[:, lane % q4][:, None, :]
        c = fn(ang_c)[:, lane % q4][None, :, :]
        return jnp.where(is_col, c, r).reshape(seq, LANES)

    return table(jnp.cos), table(jnp.sin)


def _layer(x, c, ctx, c_ctx, w_mod, b_mod, pre_g, post_g, w_in, conv_w, conv_b, rpb,
           w_out_conv, w_out_attn, w_o):
    batch, seq, _ = x.shape
    n_ctx = ctx.shape[1]
    rows = seq // GRID_W
    assert rows >= 2 * WIN_PAIRS and seq % LANES == 0 and batch <= 7

    cond8 = jnp.zeros((8, D_MODEL), F32).at[:batch].set(c).at[batch].set(c_ctx)
    mod = _modulation(cond8, w_mod, b_mod.reshape(1, -1))
    sh, sc, gt = (mod[:batch, k * D_MODEL:(k + 1) * D_MODEL].reshape(batch, 1, D_MODEL) for k in range(3))
    sh_c, sc_c = (mod[batch:batch + 1, k * D_MODEL:(k + 1) * D_MODEL] for k in range(2))

    pre_g2 = pre_g.reshape(1, D_MODEL)
    w_in_bf = w_in.astype(BF16)
    wqk_bf = _split_halves_qk(w_in_bf)
    kc, vc = _ctx_kv(ctx.reshape(batch * n_ctx, D_MODEL), sc_c, sh_c, pre_g2, wqk_bf, w_in_bf)
    kc = kc.reshape(batch, n_ctx, PART_W)
    vct = jnp.transpose(vc.reshape(batch, n_ctx, PART_W), (0, 2, 1))

    x2 = x.reshape(batch * seq, D_MODEL)
    cos, sin = _rope_tables(seq)
    tm_in = min(512, seq)
    p5, qt, vt = _inproj(x2, sc, sh, pre_g2, w_in_bf, wqk_bf, cos, sin, conv_w,
                         conv_b.reshape(1, -1), batch, seq, tm_in)

    bias = _bias_tables(rpb, rows)
    attn = _attention(p5, qt, vt, kc, vct, bias, batch, seq)

    tm_out = min(1024, seq)
    out = _mixer(p5, attn, x2, gt, post_g.reshape(1, -1),
                 w_out_conv.astype(BF16), w_out_attn.astype(BF16), w_o.astype(BF16), seq, tm_out)
    return out.reshape(batch, seq, D_MODEL)


def kernel(x, c, ctx, c_ctx, w_mod, b_mod, pre_g, post_g, w_in, conv_w, conv_b, rpb,
           w_out_conv, w_out_attn, w_o):
    depth = w_mod.shape[0]
    assert depth == 1, "context stream update between layers is not implemented"
    return _layer(x, c, ctx, c_ctx, w_mod[0], b_mod[0], pre_g[0], post_g[0], w_in[0], conv_w[0],
                  conv_b[0], rpb[0], w_out_conv[0], w_out_attn[0], w_o[0])
```

```python
import functools

import numpy as np
import jax
import jax.numpy as jnp
from jax import lax
from jax.experimental import pallas as pl
from jax.experimental.pallas import tpu as pltpu

D_MODEL = 1024
GRID_W = 64
N_HEADS = 16
HEAD_DIM = 64
WIN_ROWS = 8
WIN_COLS = 16
ROPE_BASE = 10000.0
EPS = 1e-6
PART_W = 1024
N_PARTS_IN = 10
PART_BG, PART_CG, PART_XI, PART_ZA, PART_Q, PART_K, PART_V, PART_ZB, PART_GA, PART_GB = range(N_PARTS_IN)
OUT_A, OUT_K, OUT_SZB, OUT_SGA, OUT_SGB = range(5)
N_PARTS_OUT = 5
CONV_ROWS = 64
HALO = 16
LANES = 128
N_PAIRS = N_HEADS // 2
WIN_PAIRS = 5
NEG = -1e30
RED_ROWS = 64
ATTN_GROUP = 8
SUM_ROWS = 16
LOG2E = 1.4426950408889634
BF16 = jnp.bfloat16
F32 = jnp.float32


def _modulated_norm(x, g, sc, sh):
    ms = jnp.mean(x * x, axis=-1, keepdims=True)
    return x * lax.rsqrt(ms + EPS) * (g * (1.0 + sc)) + sh


def _modulation_kernel(c_ref, w_ref, b_ref, o_ref):
    c = c_ref[...]
    o_ref[...] = jnp.dot(c * jax.nn.sigmoid(c), w_ref[...], preferred_element_type=F32,
                         precision=lax.Precision.HIGHEST) + b_ref[...]


def _modulation(cond8, w_mod, b_mod):
    n = w_mod.shape[1]
    tn = 768
    return pl.pallas_call(
        _modulation_kernel,
        grid=(n // tn,),
        in_specs=[pl.BlockSpec((8, D_MODEL), lambda j: (0, 0)),
                  pl.BlockSpec((D_MODEL, tn), lambda j: (0, j)),
                  pl.BlockSpec((1, tn), lambda j: (0, j))],
        out_specs=pl.BlockSpec((8, tn), lambda j: (0, j)),
        out_shape=jax.ShapeDtypeStruct((8, n), F32),
        name="modulation",
    )(cond8, w_mod, b_mod)


def _bias_consts():
    q = np.arange(GRID_W)[None, :]
    c = np.arange(GRID_W)[:, None]
    cs = np.clip(q - WIN_COLS // 2, 0, GRID_W - WIN_COLS)
    inwin = (c >= cs) & (c < cs + WIN_COLS)
    dc = c - q + (WIN_COLS - 1)
    sel = np.zeros((32, GRID_W, GRID_W), np.float32)
    cc, qq = np.nonzero(inwin)
    sel[dc[cc, qq], cc, qq] = 1.0
    negm = np.where(inwin, 0.0, NEG).astype(np.float32)[None]
    sel, negm = (np.concatenate([a, a], axis=-1).reshape(a.shape[0], GRID_W * LANES) for a in (sel, negm))
    return sel, negm


def _bias_kernel(rpb_ref, sel_ref, neg_ref, o_ref):
    o_ref[...] = (jnp.dot(rpb_ref[...], sel_ref[...], preferred_element_type=F32,
                          precision=lax.Precision.HIGHEST) + neg_ref[...]) * LOG2E


def _pair_window(t, n_row_pairs):
    if isinstance(t, int):
        sp = min(max(t - 2, 0), n_row_pairs - WIN_PAIRS)
        var = t if t < 2 else (t - (n_row_pairs - 2) + 3 if t >= n_row_pairs - 2 else 2)
        return sp, var
    sp = jnp.clip(t - 2, 0, n_row_pairs - WIN_PAIRS)
    var = jnp.where(t < 2, t, jnp.where(t >= n_row_pairs - 2, t - (n_row_pairs - 2) + 3, 2))
    return sp, var


N_BIAS_VAR = 5


def _bias_tables(rpb, rows):
    sel, negm = _bias_consts()
    n_dr = 2 * WIN_ROWS - 1
    n_rp = rows // 2
    rpb2 = jnp.pad(rpb.reshape(N_HEADS * n_dr, 2 * WIN_COLS - 1), ((0, 0), (0, 1)))
    bcol = pl.pallas_call(
        _bias_kernel,
        out_shape=jax.ShapeDtypeStruct((N_HEADS * n_dr, GRID_W * LANES), F32),
        name="bias_table",
    )(rpb2, jnp.asarray(sel), jnp.asarray(negm))
    blocks = bcol.reshape(N_PAIRS, 2 * n_dr, GRID_W, LANES)
    idx = np.full((N_BIAS_VAR, 2 * WIN_PAIRS, 2, 2), -1, np.int32)
    for v, t in enumerate([0, 1, 2, n_rp - 2, n_rp - 1]):
        sp, _ = _pair_window(t, n_rp)
        for w in range(2 * WIN_PAIRS):
            krow = 2 * sp + w
            for rr in range(2):
                r = 2 * t + rr
                rs = min(max(r - WIN_ROWS // 2, 0), rows - WIN_ROWS)
                if rs <= krow < rs + WIN_ROWS:
                    idx[v, w, :, rr] = np.arange(2) * n_dr + (krow - r + WIN_ROWS - 1)
    return pl.pallas_call(
        functools.partial(_bias_assemble_kernel, idx=idx),
        grid=(N_PAIRS,),
        in_specs=[pl.BlockSpec((1, 2 * n_dr, GRID_W, LANES), lambda p: (p, 0, 0, 0))],
        out_specs=pl.BlockSpec((1, N_BIAS_VAR, 2 * WIN_PAIRS * GRID_W, 2 * LANES), lambda p: (p, 0, 0, 0)),
        out_shape=jax.ShapeDtypeStruct((N_PAIRS, N_BIAS_VAR, 2 * WIN_PAIRS * GRID_W, 2 * LANES), F32),
        name="bias_assemble",
    )(blocks)


def _bias_assemble_kernel(u_ref, o_ref, *, idx):
    left = lax.broadcasted_iota(jnp.int32, (GRID_W, LANES), 1) < GRID_W
    masked = jnp.full((GRID_W, LANES), NEG, F32)
    n_var, n_win, _, _ = idx.shape
    for v in range(n_var):
        for w in range(n_win):
            for hh in range(2):
                b0, b1 = (u_ref[0, int(i)] if i >= 0 else masked for i in idx[v, w, hh])
                o_ref[0, v, w * GRID_W:(w + 1) * GRID_W, hh * LANES:(hh + 1) * LANES] = jnp.where(left, b0, b1)


def _ctx_kv_kernel(x_ref, sc_ref, sh_ref, g_ref, wk_ref, wv_ref, k_ref, v_ref):
    h = _modulated_norm(x_ref[...], g_ref[...], sc_ref[...], sh_ref[...]).astype(BF16)
    k_ref[...] = jnp.dot(h, wk_ref[0], preferred_element_type=F32).astype(BF16)
    v_ref[...] = jnp.dot(h, wv_ref[...], preferred_element_type=F32).astype(BF16)


def _ctx_kv(ctx2, sc_c, sh_c, pre_g, wqk_bf, w_in_bf):
    m = ctx2.shape[0]

    def full(shape):
        return pl.BlockSpec(shape, lambda j: (0,) * len(shape))

    return pl.pallas_call(
        _ctx_kv_kernel,
        grid=(1,),
        in_specs=[full((m, D_MODEL)), full((1, D_MODEL)), full((1, D_MODEL)), full((1, D_MODEL)),
                  pl.BlockSpec((1, D_MODEL, PART_W), lambda j: (1, 0, 0)),
                  pl.BlockSpec((D_MODEL, PART_W), lambda j: (0, PART_V))],
        out_specs=[full((m, PART_W)), full((m, PART_W))],
        out_shape=[jax.ShapeDtypeStruct((m, PART_W), BF16)] * 2,
        name="ctx_kv",
    )(ctx2, sc_c, sh_c, pre_g, wqk_bf, w_in_bf)


def _split_halves_permutation():
    q4 = HEAD_DIM // 4
    old = np.arange(PART_W).reshape(N_HEADS, 2, 2, q4).transpose(2, 0, 1, 3).reshape(PART_W)
    perm = np.zeros((PART_W, PART_W), np.float32)
    perm[old, np.arange(PART_W)] = 1.0
    return perm


def _permute_kernel(w_ref, perm_ref, o_ref):
    o_ref[0] = jnp.dot(w_ref[...], perm_ref[...], preferred_element_type=F32).astype(BF16)


def _split_halves_qk(w_in_bf):
    return pl.pallas_call(
        _permute_kernel,
        grid=(2,),
        in_specs=[pl.BlockSpec((D_MODEL, PART_W), lambda j: (0, PART_Q + j)),
                  pl.BlockSpec((PART_W, PART_W), lambda j: (0, 0))],
        out_specs=pl.BlockSpec((1, D_MODEL, PART_W), lambda j: (j, 0, 0)),
        out_shape=jax.ShapeDtypeStruct((2, D_MODEL, PART_W), BF16),
        name="permute_qk",
    )(w_in_bf, jnp.asarray(_split_halves_permutation(), BF16))


def _inproj_kernel(x_ref, xp_ref, xn_ref, sc_ref, sh_ref, g_ref, w_ref, wq_ref, wk_ref, cos_ref, sin_ref,
                   cw_ref, cb_ref, p_ref, qt_ref, vt_ref, h_ref, u_ref, *, tiles_per_batch):
    tm = x_ref.shape[0]
    n_half = PART_W // (2 * LANES)
    t = pl.program_id(0) % tiles_per_batch
    for rows, ref in ((slice(0, HALO), xp_ref), (slice(HALO, HALO + tm), x_ref),
                      (slice(HALO + tm, 2 * HALO + tm), xn_ref)):
        h_ref[rows, :] = _modulated_norm(ref[...], g_ref[...], sc_ref[0], sh_ref[0]).astype(BF16)
    tile = slice(HALO, HALO + tm)

    def slab(r, s):
        return r[:, s * LANES:(s + 1) * LANES]

    def rope(r):
        firsts, seconds = [], []
        for s in range(n_half):
            a, b = slab(r, s), slab(r, n_half + s)
            firsts.append(a * cos_ref[...] - b * sin_ref[...])
            seconds.append(b * cos_ref[...] + a * sin_ref[...])
        return firsts + seconds

    def store_transposed(t_ref, s, u):
        ut = u.astype(BF16).T
        for j in range(tm // LANES):
            t_ref[0, j, s * LANES:(s + 1) * LANES, :] = ut[:, j * LANES:(j + 1) * LANES]

    def project(n, rows):
        if n == PART_Q:
            w = wq_ref[0]
        elif n == PART_K:
            w = wk_ref[0]
        else:
            w = w_ref[:, n * PART_W:(n + 1) * PART_W]
        return jnp.dot(h_ref[rows, :], w, preferred_element_type=F32)

    def put(col, val):
        p_ref[:, col * PART_W:(col + 1) * PART_W] = val.astype(BF16)

    b_gate = project(PART_BG, tile)
    u_ref[...] = project(PART_CG, slice(None)) * project(PART_XI, slice(None))
    u_ref[0:HALO, :] = jnp.where(t == 0, 0.0, u_ref[0:HALO, :])
    u_ref[HALO + tm:, :] = jnp.where(t == tiles_per_batch - 1, 0.0, u_ref[HALO + tm:, :])
    z_a = project(PART_ZA, tile)
    for c in range(tm // CONV_ROWS):
        lo = HALO + c * CONV_ROWS
        rows = slice(c * CONV_ROWS, (c + 1) * CONV_ROWS)
        conv = (u_ref[lo - 1:lo - 1 + CONV_ROWS, :] * cw_ref[0:1, :] + u_ref[lo:lo + CONV_ROWS, :] * cw_ref[1:2, :]
                + u_ref[lo + 1:lo + 1 + CONV_ROWS, :] * cw_ref[2:3, :] + cb_ref[...])
        zc = z_a[rows]
        p_ref[rows, OUT_A * PART_W:(OUT_A + 1) * PART_W] = (
            (zc * jax.nn.sigmoid(zc)) * b_gate[rows] * conv).astype(BF16)

    for s, rk in enumerate(rope(project(PART_Q, tile))):
        store_transposed(qt_ref, s, rk * (HEAD_DIM ** -0.5 * LOG2E))
    for s, rk in enumerate(rope(project(PART_K, tile))):
        p_ref[:, OUT_K * PART_W + s * LANES:OUT_K * PART_W + (s + 1) * LANES] = rk.astype(BF16)
    r = project(PART_V, tile)
    for s in range(PART_W // LANES):
        store_transposed(vt_ref, s, slab(r, s))
    z_b = project(PART_ZB, tile)
    put(OUT_SZB, z_b * jax.nn.sigmoid(z_b))
    put(OUT_SGA, jax.nn.sigmoid(project(PART_GA, tile)))
    put(OUT_SGB, jax.nn.sigmoid(project(PART_GB, tile)))


def _inproj(x2, sc, sh, pre_g, w_in_bf, wqk_bf, cos, sin, conv_w, conv_b, batch, seq, tm):
    m = x2.shape[0]
    tpb = seq // tm
    hb = tm // HALO
    t_spec = pl.BlockSpec((1, tm // LANES, PART_W, LANES), lambda i: (i // tpb, i % tpb, 0, 0))
    t_shape = jax.ShapeDtypeStruct((batch, seq // LANES, PART_W, LANES), BF16)

    def resident(shape, index):
        return pl.BlockSpec(shape, lambda i: index, pipeline_mode=pl.Buffered(1))

    return pl.pallas_call(
        functools.partial(_inproj_kernel, tiles_per_batch=tpb),
        grid=(m // tm,),
        in_specs=[pl.BlockSpec((tm, D_MODEL), lambda i: (i, 0)),
                  pl.BlockSpec((HALO, D_MODEL), lambda i: (jnp.maximum(i * hb - 1, 0), 0)),
                  pl.BlockSpec((HALO, D_MODEL), lambda i: (jnp.minimum((i + 1) * hb, m // HALO - 1), 0)),
                  pl.BlockSpec((1, 1, D_MODEL), lambda i: (i // tpb, 0, 0)),
                  pl.BlockSpec((1, 1, D_MODEL), lambda i: (i // tpb, 0, 0)),
                  pl.BlockSpec((1, D_MODEL), lambda i: (0, 0)),
                  resident((D_MODEL, N_PARTS_IN * PART_W), (0, 0)),
                  resident((1, D_MODEL, PART_W), (0, 0, 0)), resident((1, D_MODEL, PART_W), (1, 0, 0)),
                  pl.BlockSpec((tm, LANES), lambda i: (i % tpb, 0)),
                  pl.BlockSpec((tm, LANES), lambda i: (i % tpb, 0)),
                  pl.BlockSpec((3, PART_W), lambda i: (0, 0)),
                  pl.BlockSpec((1, PART_W), lambda i: (0, 0))],
        out_specs=[pl.BlockSpec((tm, N_PARTS_OUT * PART_W), lambda i: (i, 0)), t_spec, t_spec],
        out_shape=[jax.ShapeDtypeStruct((m, N_PARTS_OUT * PART_W), BF16), t_shape, t_shape],
        scratch_shapes=[pltpu.VMEM((tm + 2 * HALO, D_MODEL), BF16),
                        pltpu.VMEM((tm + 2 * HALO, PART_W), F32)],
        compiler_params=pltpu.CompilerParams(dimension_semantics=("arbitrary",)),
        name="in_projection",
    )(x2, x2, x2, sc, sh, pre_g, w_in_bf, wqk_bf, wqk_bf, cos, sin, conv_w, conv_b)


def _attn_kernel(qta_ref, qtb_ref, ka_ref, kb_ref, vt_ref, kca_ref, kcb_ref, vct_ref, bias_ref,
                 o_ref, *stage_refs, n_row_pairs):
    low = lax.broadcasted_iota(jnp.int32, (LANES, LANES), 0) < HEAD_DIM
    nb = WIN_PAIRS * LANES
    n_keys = nb + kca_ref.shape[1]
    last = n_row_pairs - 1
    head_of_dim = (lax.broadcasted_iota(jnp.int32, (2 * LANES, LANES), 0) % LANES) // (HEAD_DIM // 2)
    first_head = 2 * (pl.program_id(0) % 2)
    sel0 = head_of_dim == first_head
    sel1 = head_of_dim == first_head + 1

    def scores(t, slot):
        sp, var = _pair_window(t, n_row_pairs)
        qt = jnp.concatenate([qta_ref[0, t], qtb_ref[0, t]], axis=0)
        zero = jnp.zeros_like(qt)
        rhs = jnp.concatenate([jnp.where(sel0, qt, zero), jnp.where(sel1, qt, zero)], axis=1)
        rows = pl.ds(pl.multiple_of(sp * LANES, LANES), nb)
        kwin = jnp.concatenate([ka_ref[rows, :], kb_ref[rows, :]], axis=1)
        s_refs[slot][:nb, :] = jnp.dot(kwin, rhs, preferred_element_type=F32) + bias_ref[0, var]
        yield
        kc = jnp.concatenate([kca_ref[0], kcb_ref[0]], axis=1)
        s_refs[slot][nb:n_keys, :] = jnp.dot(kc, rhs, preferred_element_type=F32)

    def softmax(slot):
        s_ref, p_ref = s_refs[slot], p_refs[slot]
        chunks = [slice(c * RED_ROWS, (c + 1) * RED_ROWS) for c in range(n_keys // RED_ROWS)]
        m = s_ref[chunks[0], :]
        for c in chunks[1:]:
            m = jnp.maximum(m, s_ref[c, :])
        m = jnp.max(m, axis=0, keepdims=True)
        yield
        for k, c in enumerate(chunks):
            p_ref[c, :] = jnp.exp2(s_ref[c, :] - m).astype(BF16)
            if k % 4 == 3:
                yield

    def values(t, slot):
        sp, _ = _pair_window(t, n_row_pairs)
        p = p_refs[slot][:n_keys, :]
        vext = jnp.concatenate([vt_ref[0, sp + g] for g in range(WIN_PAIRS)] + [vct_ref[0]], axis=1)
        vext = jnp.concatenate([vext, jnp.ones((SUM_ROWS, p.shape[0]), BF16)], axis=0)
        ot = jnp.dot(vext, p, preferred_element_type=F32)
        yield
        ot = ot[:LANES] * (1.0 / ot[LANES:LANES + 1])
        out = jnp.where(low, ot[:, :LANES], ot[:, LANES:]).T
        o_ref[pl.ds(pl.multiple_of(t * LANES, LANES), LANES), :] = out.astype(BF16)

    def run(*stages):
        for stage in stages:
            for _ in stage:
                pass

    def interleave(sm, va, sc):
        next(sm, None)
        next(sc, None)
        next(sm, None)
        next(sm, None)
        run(sc)
        next(sm, None)
        run(sm, va)

    g = ATTN_GROUP
    s_refs, p_refs = stage_refs[:2 * g], stage_refs[2 * g:]
    for j in range(g):
        run(scores(j, j))

    def group(t, cur, nxt):
        for j in range(g):
            interleave(softmax(cur + j), values(t + j, cur + j),
                       scores(jnp.minimum(t + j + g, last), nxt + j))

    def body(i, carry):
        t = 2 * g * i
        group(t, 0, g)
        group(t + g, g, 0)
        return carry

    lax.fori_loop(0, n_row_pairs // (2 * g), body, 0)


def _attention(p5, qt, vt, kc, vct, bias, batch, seq):
    n_ctx = kc.shape[1]
    n_keys = WIN_PAIRS * LANES + n_ctx
    n_slots = 2 * ATTN_GROUP
    assert (seq // LANES) % n_slots == 0
    spp = PART_W // LANES
    half = spp // 2

    def t_spec(slab):
        return pl.BlockSpec((1, seq // LANES, LANES, LANES), lambda hp, b: (b, 0, slab(hp), 0))

    return pl.pallas_call(
        functools.partial(_attn_kernel, n_row_pairs=seq // LANES),
        grid=(N_PAIRS, batch),
        in_specs=[t_spec(lambda hp: hp // 2), t_spec(lambda hp: half + hp // 2),
                  pl.BlockSpec((seq, LANES), lambda hp, b: (b, OUT_K * spp + hp // 2)),
                  pl.BlockSpec((seq, LANES), lambda hp, b: (b, OUT_K * spp + half + hp // 2)),
                  t_spec(lambda hp: hp),
                  pl.BlockSpec((1, n_ctx, LANES), lambda hp, b: (b, 0, hp // 2)),
                  pl.BlockSpec((1, n_ctx, LANES), lambda hp, b: (b, 0, half + hp // 2)),
                  pl.BlockSpec((1, LANES, n_ctx), lambda hp, b: (b, hp, 0)),
                  pl.BlockSpec((1, N_BIAS_VAR, WIN_PAIRS * LANES, 2 * LANES),
                               lambda hp, b: (hp, 0, 0, 0))],
        out_specs=pl.BlockSpec((seq, LANES), lambda hp, b: (b, hp)),
        out_shape=jax.ShapeDtypeStruct((batch * seq, N_HEADS * HEAD_DIM), BF16),
        scratch_shapes=[pltpu.VMEM((n_keys, 2 * LANES), F32)] * n_slots
        + [pltpu.VMEM((n_keys, 2 * LANES), BF16)] * n_slots,
        compiler_params=pltpu.CompilerParams(dimension_semantics=("arbitrary", "arbitrary")),
        name="attention",
    )(qt, qt, p5, p5, vt, kc, kc, vct, bias)


def _mixer_kernel(a_ref, szb_ref, sga_ref, sgb_ref, at_ref, x_ref, gt_ref, pg_ref,
                  woc_ref, woa_ref, wo_ref, o_ref):
    y_a = jnp.dot(a_ref[...], woc_ref[...], preferred_element_type=F32)
    bb = szb_ref[...].astype(F32) * at_ref[...].astype(F32)
    y_b = jnp.dot(bb.astype(BF16), woa_ref[...], preferred_element_type=F32)
    merged = sga_ref[...].astype(F32) * y_a + sgb_ref[...].astype(F32) * y_b
    y = jnp.dot(merged.astype(BF16), wo_ref[...], preferred_element_type=F32)
    ms = jnp.mean(y * y, axis=-1, keepdims=True)
    o_ref[...] = x_ref[...] + gt_ref[0] * ((y * lax.rsqrt(ms + EPS)) * pg_ref[...])


def _mixer(p5, attn, x2, gt, post_g, woc, woa, wo, seq, tm):
    m = x2.shape[0]
    tpb = seq // tm

    def part(k):
        return pl.BlockSpec((tm, PART_W), lambda i: (i, k))

    def full(shape):
        return pl.BlockSpec(shape, lambda i: (0,) * len(shape))

    return pl.pallas_call(
        _mixer_kernel,
        grid=(m // tm,),
        in_specs=[part(OUT_A), part(OUT_SZB), part(OUT_SGA), part(OUT_SGB),
                  pl.BlockSpec((tm, PART_W), lambda i: (i, 0)),
                  pl.BlockSpec((tm, D_MODEL), lambda i: (i, 0)),
                  pl.BlockSpec((1, 1, D_MODEL), lambda i: (i // tpb, 0, 0)),
                  full((1, D_MODEL)),
                  full((PART_W, D_MODEL)), full((PART_W, D_MODEL)), full((D_MODEL, D_MODEL))],
        out_specs=pl.BlockSpec((tm, D_MODEL), lambda i: (i, 0)),
        out_shape=jax.ShapeDtypeStruct((m, D_MODEL), F32),
        compiler_params=pltpu.CompilerParams(dimension_semantics=("arbitrary",)),
        name="mixer_out",
    )(p5, p5, p5, p5, attn, x2, gt, post_g, woc, woa, wo)


def _rope_tables(seq):
    t = jnp.arange(seq, dtype=jnp.int32)
    row = (t // GRID_W).astype(F32)
    col = (t % GRID_W).astype(F32)
    half = HEAD_DIM // 2
    inv = ROPE_BASE ** (-jnp.arange(0, half, 2, dtype=F32) / half)
    ang_r = row[:, None] * inv
    ang_c = col[:, None] * inv
    ang = jnp.tile(jnp.concatenate([ang_r, ang_c], axis=-1), (1, LANES // (HEAD_DIM // 2)))
    return jnp.cos(ang), jnp.sin(ang)


def _layer(x, c, ctx, c_ctx, w_mod, b_mod, pre_g, post_g, w_in, conv_w, conv_b, rpb,
           w_out_conv, w_out_attn, w_o):
    batch, seq, _ = x.shape
    n_ctx = ctx.shape[1]
    rows = seq // GRID_W
    assert rows >= 2 * WIN_PAIRS and seq % LANES == 0 and batch <= 7

    cond8 = jnp.zeros((8, D_MODEL), F32).at[:batch].set(c).at[batch].set(c_ctx)
    mod = _modulation(cond8, w_mod, b_mod.reshape(1, -1))
    sh, sc, gt = (mod[:batch, k * D_MODEL:(k + 1) * D_MODEL].reshape(batch, 1, D_MODEL) for k in range(3))
    sh_c, sc_c = (mod[batch:batch + 1, k * D_MODEL:(k + 1) * D_MODEL] for k in range(2))

    pre_g2 = pre_g.reshape(1, D_MODEL)
    w_in_bf = w_in.astype(BF16)
    wqk_bf = _split_halves_qk(w_in_bf)
    kc, vc = _ctx_kv(ctx.reshape(batch * n_ctx, D_MODEL), sc_c, sh_c, pre_g2, wqk_bf, w_in_bf)
    kc = kc.reshape(batch, n_ctx, PART_W)
    vct = jnp.transpose(vc.reshape(batch, n_ctx, PART_W), (0, 2, 1))

    x2 = x.reshape(batch * seq, D_MODEL)
    cos, sin = _rope_tables(seq)
    tm_in = min(512, seq)
    p5, qt, vt = _inproj(x2, sc, sh, pre_g2, w_in_bf, wqk_bf, cos, sin, conv_w,
                         conv_b.reshape(1, -1), batch, seq, tm_in)

    bias = _bias_tables(rpb, rows)
    attn = _attention(p5, qt, vt, kc, vct, bias, batch, seq)

    tm_out = min(512, seq)
    out = _mixer(p5, attn, x2, gt, post_g.reshape(1, -1),
                 w_out_conv.astype(BF16), w_out_attn.astype(BF16), w_o.astype(BF16), seq, tm_out)
    return out.reshape(batch, seq, D_MODEL)


def kernel(x, c, ctx, c_ctx, w_mod, b_mod, pre_g, post_g, w_in, conv_w, conv_b, rpb,
           w_out_conv, w_out_attn, w_o):
    depth = w_mod.shape[0]
    assert depth == 1, "context stream update between layers is not implemented"
    return _layer(x, c, ctx, c_ctx, w_mod[0], b_mod[0], pre_g[0], post_g[0], w_in[0], conv_w[0],
                  conv_b[0], rpb[0], w_out_conv[0], w_out_attn[0], w_o[0])
```

```python
import functools

import numpy as np
import jax
import jax.numpy as jnp
from jax import lax
from jax.experimental import pallas as pl
from jax.experimental.pallas import tpu as pltpu

D_MODEL = 1024
GRID_W = 64
N_HEADS = 16
HEAD_DIM = 64
WIN_ROWS = 8
WIN_COLS = 16
ROPE_BASE = 10000.0
EPS = 1e-6
PART_W = 1024
N_PARTS_IN = 10
PART_BG, PART_CG, PART_XI, PART_ZA, PART_Q, PART_K, PART_V, PART_ZB, PART_GA, PART_GB = range(N_PARTS_IN)
OUT_A, OUT_K, OUT_SZB, OUT_SGA, OUT_SGB = range(5)
N_PARTS_OUT = 5
CONV_ROWS = 64
HALO = 16
LANES = 128
N_PAIRS = N_HEADS // 2
WIN_PAIRS = 5
NEG = -1e30
RED_ROWS = 64
ATTN_GROUP = 8
SUM_ROWS = 16
LOG2E = 1.4426950408889634
BF16 = jnp.bfloat16
F32 = jnp.float32


def _modulated_norm(x, g, sc, sh):
    ms = jnp.mean(x * x, axis=-1, keepdims=True)
    return x * lax.rsqrt(ms + EPS) * (g * (1.0 + sc)) + sh


def _modulation_kernel(c_ref, w_ref, b_ref, o_ref):
    c = c_ref[...]
    o_ref[...] = jnp.dot(c * jax.nn.sigmoid(c), w_ref[...], preferred_element_type=F32,
                         precision=lax.Precision.HIGHEST) + b_ref[...]


def _modulation(cond8, w_mod, b_mod):
    n = w_mod.shape[1]
    tn = 768
    return pl.pallas_call(
        _modulation_kernel,
        grid=(n // tn,),
        in_specs=[pl.BlockSpec((8, D_MODEL), lambda j: (0, 0)),
                  pl.BlockSpec((D_MODEL, tn), lambda j: (0, j)),
                  pl.BlockSpec((1, tn), lambda j: (0, j))],
        out_specs=pl.BlockSpec((8, tn), lambda j: (0, j)),
        out_shape=jax.ShapeDtypeStruct((8, n), F32),
        name="modulation",
    )(cond8, w_mod, b_mod)


def _bias_consts():
    q = np.arange(GRID_W)[None, :]
    c = np.arange(GRID_W)[:, None]
    cs = np.clip(q - WIN_COLS // 2, 0, GRID_W - WIN_COLS)
    inwin = (c >= cs) & (c < cs + WIN_COLS)
    dc = c - q + (WIN_COLS - 1)
    sel = np.zeros((32, GRID_W, GRID_W), np.float32)
    cc, qq = np.nonzero(inwin)
    sel[dc[cc, qq], cc, qq] = 1.0
    negm = np.where(inwin, 0.0, NEG).astype(np.float32)[None]
    sel, negm = (np.concatenate([a, a], axis=-1).reshape(a.shape[0], GRID_W * LANES) for a in (sel, negm))
    return sel, negm


def _bias_kernel(rpb_ref, sel_ref, neg_ref, o_ref):
    o_ref[...] = (jnp.dot(rpb_ref[...], sel_ref[...], preferred_element_type=F32,
                          precision=lax.Precision.HIGHEST) + neg_ref[...]) * LOG2E


def _pair_window(t, n_row_pairs):
    if isinstance(t, int):
        sp = min(max(t - 2, 0), n_row_pairs - WIN_PAIRS)
        var = t if t < 2 else (t - (n_row_pairs - 2) + 3 if t >= n_row_pairs - 2 else 2)
        return sp, var
    sp = jnp.clip(t - 2, 0, n_row_pairs - WIN_PAIRS)
    var = jnp.where(t < 2, t, jnp.where(t >= n_row_pairs - 2, t - (n_row_pairs - 2) + 3, 2))
    return sp, var


N_BIAS_VAR = 5


def _bias_tables(rpb, rows):
    sel, negm = _bias_consts()
    n_dr = 2 * WIN_ROWS - 1
    n_rp = rows // 2
    rpb2 = jnp.pad(rpb.reshape(N_HEADS * n_dr, 2 * WIN_COLS - 1), ((0, 0), (0, 1)))
    bcol = pl.pallas_call(
        _bias_kernel,
        out_shape=jax.ShapeDtypeStruct((N_HEADS * n_dr, GRID_W * LANES), F32),
        name="bias_table",
    )(rpb2, jnp.asarray(sel), jnp.asarray(negm))
    blocks = bcol.reshape(N_PAIRS, 2 * n_dr, GRID_W, LANES)
    idx = np.full((N_BIAS_VAR, 2 * WIN_PAIRS, 2, 2), -1, np.int32)
    for v, t in enumerate([0, 1, 2, n_rp - 2, n_rp - 1]):
        sp, _ = _pair_window(t, n_rp)
        for w in range(2 * WIN_PAIRS):
            krow = 2 * sp + w
            for rr in range(2):
                r = 2 * t + rr
                rs = min(max(r - WIN_ROWS // 2, 0), rows - WIN_ROWS)
                if rs <= krow < rs + WIN_ROWS:
                    idx[v, w, :, rr] = np.arange(2) * n_dr + (krow - r + WIN_ROWS - 1)
    return pl.pallas_call(
        functools.partial(_bias_assemble_kernel, idx=idx),
        grid=(N_PAIRS,),
        in_specs=[pl.BlockSpec((1, 2 * n_dr, GRID_W, LANES), lambda p: (p, 0, 0, 0))],
        out_specs=pl.BlockSpec((1, N_BIAS_VAR, 2 * WIN_PAIRS * GRID_W, 2 * LANES), lambda p: (p, 0, 0, 0)),
        out_shape=jax.ShapeDtypeStruct((N_PAIRS, N_BIAS_VAR, 2 * WIN_PAIRS * GRID_W, 2 * LANES), F32),
        name="bias_assemble",
    )(blocks)


def _bias_assemble_kernel(u_ref, o_ref, *, idx):
    left = lax.broadcasted_iota(jnp.int32, (GRID_W, LANES), 1) < GRID_W
    masked = jnp.full((GRID_W, LANES), NEG, F32)
    n_var, n_win, _, _ = idx.shape
    for v in range(n_var):
        for w in range(n_win):
            for hh in range(2):
                b0, b1 = (u_ref[0, int(i)] if i >= 0 else masked for i in idx[v, w, hh])
                o_ref[0, v, w * GRID_W:(w + 1) * GRID_W, hh * LANES:(hh + 1) * LANES] = jnp.where(left, b0, b1)


def _ctx_kv_kernel(x_ref, sc_ref, sh_ref, g_ref, wk_ref, wv_ref, k_ref, v_ref):
    h = _modulated_norm(x_ref[...], g_ref[...], sc_ref[...], sh_ref[...]).astype(BF16)
    k_ref[...] = jnp.dot(h, wk_ref[0], preferred_element_type=F32).astype(BF16)
    v_ref[...] = jnp.dot(h, wv_ref[...], preferred_element_type=F32).astype(BF16)


def _ctx_kv(ctx2, sc_c, sh_c, pre_g, wqk_bf, w_in_bf):
    m = ctx2.shape[0]

    def full(shape):
        return pl.BlockSpec(shape, lambda j: (0,) * len(shape))

    return pl.pallas_call(
        _ctx_kv_kernel,
        grid=(1,),
        in_specs=[full((m, D_MODEL)), full((1, D_MODEL)), full((1, D_MODEL)), full((1, D_MODEL)),
                  pl.BlockSpec((1, D_MODEL, PART_W), lambda j: (1, 0, 0)),
                  pl.BlockSpec((D_MODEL, PART_W), lambda j: (0, PART_V))],
        out_specs=[full((m, PART_W)), full((m, PART_W))],
        out_shape=[jax.ShapeDtypeStruct((m, PART_W), BF16)] * 2,
        name="ctx_kv",
    )(ctx2, sc_c, sh_c, pre_g, wqk_bf, w_in_bf)


def _split_halves_permutation():
    q4 = HEAD_DIM // 4
    old = np.arange(PART_W).reshape(N_HEADS, 2, 2, q4).transpose(2, 0, 1, 3).reshape(PART_W)
    perm = np.zeros((PART_W, PART_W), np.float32)
    perm[old, np.arange(PART_W)] = 1.0
    return perm


def _permute_kernel(w_ref, perm_ref, o_ref):
    o_ref[0] = jnp.dot(w_ref[...], perm_ref[...], preferred_element_type=F32).astype(BF16)


def _split_halves_qk(w_in_bf):
    return pl.pallas_call(
        _permute_kernel,
        grid=(2,),
        in_specs=[pl.BlockSpec((D_MODEL, PART_W), lambda j: (0, PART_Q + j)),
                  pl.BlockSpec((PART_W, PART_W), lambda j: (0, 0))],
        out_specs=pl.BlockSpec((1, D_MODEL, PART_W), lambda j: (j, 0, 0)),
        out_shape=jax.ShapeDtypeStruct((2, D_MODEL, PART_W), BF16),
        name="permute_qk",
    )(w_in_bf, jnp.asarray(_split_halves_permutation(), BF16))


def _inproj_kernel(x_ref, xp_ref, xn_ref, sc_ref, sh_ref, g_ref, w_ref, wq_ref, wk_ref, cos_ref, sin_ref,
                   cw_ref, cb_ref, p_ref, qt_ref, vt_ref, h_ref, u_ref, *, tiles_per_batch):
    tm = x_ref.shape[0]
    n_half = PART_W // (2 * LANES)
    t = pl.program_id(0) % tiles_per_batch
    for rows, ref in ((slice(0, HALO), xp_ref), (slice(HALO, HALO + tm), x_ref),
                      (slice(HALO + tm, 2 * HALO + tm), xn_ref)):
        h_ref[rows, :] = _modulated_norm(ref[...], g_ref[...], sc_ref[0], sh_ref[0]).astype(BF16)
    tile = slice(HALO, HALO + tm)

    def slab(r, s):
        return r[:, s * LANES:(s + 1) * LANES]

    def rope(r):
        firsts, seconds = [], []
        for s in range(n_half):
            a, b = slab(r, s), slab(r, n_half + s)
            firsts.append(a * cos_ref[...] - b * sin_ref[...])
            seconds.append(b * cos_ref[...] + a * sin_ref[...])
        return firsts + seconds

    def store_transposed(t_ref, s, u):
        ut = u.astype(BF16).T
        for j in range(tm // LANES):
            t_ref[0, j, s * LANES:(s + 1) * LANES, :] = ut[:, j * LANES:(j + 1) * LANES]

    def project(n, rows):
        if n == PART_Q:
            w = wq_ref[0]
        elif n == PART_K:
            w = wk_ref[0]
        else:
            w = w_ref[:, n * PART_W:(n + 1) * PART_W]
        return jnp.dot(h_ref[rows, :], w, preferred_element_type=F32)

    def put(col, val):
        p_ref[:, col * PART_W:(col + 1) * PART_W] = val.astype(BF16)

    b_gate = project(PART_BG, tile)
    u_ref[...] = project(PART_CG, slice(None)) * project(PART_XI, slice(None))
    u_ref[0:HALO, :] = jnp.where(t == 0, 0.0, u_ref[0:HALO, :])
    u_ref[HALO + tm:, :] = jnp.where(t == tiles_per_batch - 1, 0.0, u_ref[HALO + tm:, :])
    z_a = project(PART_ZA, tile)
    for c in range(tm // CONV_ROWS):
        lo = HALO + c * CONV_ROWS
        rows = slice(c * CONV_ROWS, (c + 1) * CONV_ROWS)
        conv = (u_ref[lo - 1:lo - 1 + CONV_ROWS, :] * cw_ref[0:1, :] + u_ref[lo:lo + CONV_ROWS, :] * cw_ref[1:2, :]
                + u_ref[lo + 1:lo + 1 + CONV_ROWS, :] * cw_ref[2:3, :] + cb_ref[...])
        zc = z_a[rows]
        p_ref[rows, OUT_A * PART_W:(OUT_A + 1) * PART_W] = (
            (zc * jax.nn.sigmoid(zc)) * b_gate[rows] * conv).astype(BF16)

    for s, rk in enumerate(rope(project(PART_Q, tile))):
        store_transposed(qt_ref, s, rk * (HEAD_DIM ** -0.5 * LOG2E))
    for s, rk in enumerate(rope(project(PART_K, tile))):
        p_ref[:, OUT_K * PART_W + s * LANES:OUT_K * PART_W + (s + 1) * LANES] = rk.astype(BF16)
    r = project(PART_V, tile)
    for s in range(PART_W // LANES):
        store_transposed(vt_ref, s, slab(r, s))
    z_b = project(PART_ZB, tile)
    put(OUT_SZB, z_b * jax.nn.sigmoid(z_b))
    put(OUT_SGA, jax.nn.sigmoid(project(PART_GA, tile)))
    put(OUT_SGB, jax.nn.sigmoid(project(PART_GB, tile)))


def _inproj(x2, sc, sh, pre_g, w_in_bf, wqk_bf, cos, sin, conv_w, conv_b, batch, seq, tm):
    m = x2.shape[0]
    tpb = seq // tm
    hb = tm // HALO
    t_spec = pl.BlockSpec((1, tm // LANES, PART_W, LANES), lambda i: (i // tpb, i % tpb, 0, 0))
    t_shape = jax.ShapeDtypeStruct((batch, seq // LANES, PART_W, LANES), BF16)

    def resident(shape, index):
        return pl.BlockSpec(shape, lambda i: index, pipeline_mode=pl.Buffered(1))

    return pl.pallas_call(
        functools.partial(_inproj_kernel, tiles_per_batch=tpb),
        grid=(m // tm,),
        in_specs=[pl.BlockSpec((tm, D_MODEL), lambda i: (i, 0)),
                  pl.BlockSpec((HALO, D_MODEL), lambda i: (jnp.maximum(i * hb - 1, 0), 0)),
                  pl.BlockSpec((HALO, D_MODEL), lambda i: (jnp.minimum((i + 1) * hb, m // HALO - 1), 0)),
                  pl.BlockSpec((1, 1, D_MODEL), lambda i: (i // tpb, 0, 0)),
                  pl.BlockSpec((1, 1, D_MODEL), lambda i: (i // tpb, 0, 0)),
                  pl.BlockSpec((1, D_MODEL), lambda i: (0, 0)),
                  resident((D_MODEL, N_PARTS_IN * PART_W), (0, 0)),
                  resident((1, D_MODEL, PART_W), (0, 0, 0)), resident((1, D_MODEL, PART_W), (1, 0, 0)),
                  pl.BlockSpec((tm, LANES), lambda i: (i % tpb, 0)),
                  pl.BlockSpec((tm, LANES), lambda i: (i % tpb, 0)),
                  pl.BlockSpec((3, PART_W), lambda i: (0, 0)),
                  pl.BlockSpec((1, PART_W), lambda i: (0, 0))],
        out_specs=[pl.BlockSpec((tm, N_PARTS_OUT * PART_W), lambda i: (i, 0)), t_spec, t_spec],
        out_shape=[jax.ShapeDtypeStruct((m, N_PARTS_OUT * PART_W), BF16), t_shape, t_shape],
        scratch_shapes=[pltpu.VMEM((tm + 2 * HALO, D_MODEL), BF16),
                        pltpu.VMEM((tm + 2 * HALO, PART_W), F32)],
        compiler_params=pltpu.CompilerParams(dimension_semantics=("arbitrary",)),
        name="in_projection",
    )(x2, x2, x2, sc, sh, pre_g, w_in_bf, wqk_bf, wqk_bf, cos, sin, conv_w, conv_b)


def _attn_kernel(qta_ref, qtb_ref, ka_ref, kb_ref, vt_ref, kca_ref, kcb_ref, vct_ref, bias_ref,
                 o_ref, *stage_refs, n_row_pairs):
    low = lax.broadcasted_iota(jnp.int32, (LANES, LANES), 0) < HEAD_DIM
    nb = WIN_PAIRS * LANES
    n_keys = nb + kca_ref.shape[1]
    last = n_row_pairs - 1
    head_of_dim = (lax.broadcasted_iota(jnp.int32, (2 * LANES, LANES), 0) % LANES) // (HEAD_DIM // 2)
    first_head = 2 * (pl.program_id(0) % 2)
    sel0 = head_of_dim == first_head
    sel1 = head_of_dim == first_head + 1

    def scores(t, slot):
        sp, var = _pair_window(t, n_row_pairs)
        qt = jnp.concatenate([qta_ref[0, t], qtb_ref[0, t]], axis=0)
        zero = jnp.zeros_like(qt)
        rhs = jnp.concatenate([jnp.where(sel0, qt, zero), jnp.where(sel1, qt, zero)], axis=1)
        rows = pl.ds(pl.multiple_of(sp * LANES, LANES), nb)
        kwin = jnp.concatenate([ka_ref[rows, :], kb_ref[rows, :]], axis=1)
        s_refs[slot][:nb, :] = jnp.dot(kwin, rhs, preferred_element_type=F32) + bias_ref[0, var]
        kc = jnp.concatenate([kca_ref[0], kcb_ref[0]], axis=1)
        s_refs[slot][nb:, :] = jnp.dot(kc, rhs, preferred_element_type=F32)

    def softmax(slot):
        s_ref, p_ref = s_refs[slot], p_refs[slot]
        chunks = [slice(c * RED_ROWS, (c + 1) * RED_ROWS) for c in range(n_keys // RED_ROWS)]
        m = s_ref[chunks[0], :]
        for c in chunks[1:]:
            m = jnp.maximum(m, s_ref[c, :])
        m = jnp.max(m, axis=0, keepdims=True)
        for c in chunks:
            p_ref[c, :] = jnp.exp2((s_ref[c, :] - m).astype(BF16))

    def values(t, slot):
        sp, _ = _pair_window(t, n_row_pairs)
        p = p_refs[slot][...]
        vext = jnp.concatenate([vt_ref[0, sp + g] for g in range(WIN_PAIRS)] + [vct_ref[0]], axis=1)
        vext = jnp.concatenate([vext, jnp.ones((SUM_ROWS, n_keys), BF16)], axis=0)
        ot = jnp.dot(vext, p, preferred_element_type=F32)
        ot = ot[:LANES] * (1.0 / ot[LANES:LANES + 1])
        out = jnp.where(low, ot[:, :LANES], ot[:, LANES:]).T
        o_ref[pl.ds(pl.multiple_of(t * LANES, LANES), LANES), :] = out.astype(BF16)

    g = ATTN_GROUP
    s_refs, p_refs = stage_refs[:2 * g], stage_refs[2 * g:]
    for j in range(g):
        scores(j, j)

    def group(t, cur, nxt):
        for j in range(g):
            scores(jnp.minimum(t + j + g, last), nxt + j)
            softmax(cur + j)
            values(t + j, cur + j)

    def body(i, carry):
        t = 2 * g * i
        group(t, 0, g)
        group(t + g, g, 0)
        return carry

    lax.fori_loop(0, n_row_pairs // (2 * g), body, 0)


def _attention(p5, qt, vt, kc, vct, bias, batch, seq):
    n_ctx = kc.shape[1]
    n_keys = WIN_PAIRS * LANES + n_ctx
    n_slots = 2 * ATTN_GROUP
    assert (seq // LANES) % n_slots == 0
    spp = PART_W // LANES
    half = spp // 2

    def t_spec(slab):
        return pl.BlockSpec((1, seq // LANES, LANES, LANES), lambda hp, b: (b, 0, slab(hp), 0))

    return pl.pallas_call(
        functools.partial(_attn_kernel, n_row_pairs=seq // LANES),
        grid=(N_PAIRS, batch),
        in_specs=[t_spec(lambda hp: hp // 2), t_spec(lambda hp: half + hp // 2),
                  pl.BlockSpec((seq, LANES), lambda hp, b: (b, OUT_K * spp + hp // 2)),
                  pl.BlockSpec((seq, LANES), lambda hp, b: (b, OUT_K * spp + half + hp // 2)),
                  t_spec(lambda hp: hp),
                  pl.BlockSpec((1, n_ctx, LANES), lambda hp, b: (b, 0, hp // 2)),
                  pl.BlockSpec((1, n_ctx, LANES), lambda hp, b: (b, 0, half + hp // 2)),
                  pl.BlockSpec((1, LANES, n_ctx), lambda hp, b: (b, hp, 0)),
                  pl.BlockSpec((1, N_BIAS_VAR, WIN_PAIRS * LANES, 2 * LANES),
                               lambda hp, b: (hp, 0, 0, 0))],
        out_specs=pl.BlockSpec((seq, LANES), lambda hp, b: (b, hp)),
        out_shape=jax.ShapeDtypeStruct((batch * seq, N_HEADS * HEAD_DIM), BF16),
        scratch_shapes=[pltpu.VMEM((n_keys, 2 * LANES), F32)] * n_slots
        + [pltpu.VMEM((n_keys, 2 * LANES), BF16)] * n_slots,
        compiler_params=pltpu.CompilerParams(dimension_semantics=("arbitrary", "arbitrary")),
        name="attention",
    )(qt, qt, p5, p5, vt, kc, kc, vct, bias)


def _mixer_kernel(a_ref, szb_ref, sga_ref, sgb_ref, at_ref, x_ref, gt_ref, pg_ref,
                  woc_ref, woa_ref, wo_ref, o_ref):
    y_a = jnp.dot(a_ref[...], woc_ref[...], preferred_element_type=F32)
    bb = szb_ref[...].astype(F32) * at_ref[...].astype(F32)
    y_b = jnp.dot(bb.astype(BF16), woa_ref[...], preferred_element_type=F32)
    merged = sga_ref[...].astype(F32) * y_a + sgb_ref[...].astype(F32) * y_b
    y = jnp.dot(merged.astype(BF16), wo_ref[...], preferred_element_type=F32)
    ms = jnp.mean(y * y, axis=-1, keepdims=True)
    o_ref[...] = x_ref[...] + gt_ref[0] * ((y * lax.rsqrt(ms + EPS)) * pg_ref[...])


def _mixer(p5, attn, x2, gt, post_g, woc, woa, wo, seq, tm):
    m = x2.shape[0]
    tpb = seq // tm

    def part(k):
        return pl.BlockSpec((tm, PART_W), lambda i: (i, k))

    def full(shape):
        return pl.BlockSpec(shape, lambda i: (0,) * len(shape))

    return pl.pallas_call(
        _mixer_kernel,
        grid=(m // tm,),
        in_specs=[part(OUT_A), part(OUT_SZB), part(OUT_SGA), part(OUT_SGB),
                  pl.BlockSpec((tm, PART_W), lambda i: (i, 0)),
                  pl.BlockSpec((tm, D_MODEL), lambda i: (i, 0)),
                  pl.BlockSpec((1, 1, D_MODEL), lambda i: (i // tpb, 0, 0)),
                  full((1, D_MODEL)),
                  full((PART_W, D_MODEL)), full((PART_W, D_MODEL)), full((D_MODEL, D_MODEL))],
        out_specs=pl.BlockSpec((tm, D_MODEL), lambda i: (i, 0)),
        out_shape=jax.ShapeDtypeStruct((m, D_MODEL), F32),
        compiler_params=pltpu.CompilerParams(dimension_semantics=("arbitrary",)),
        name="mixer_out",
    )(p5, p5, p5, p5, attn, x2, gt, post_g, woc, woa, wo)


def _rope_tables(seq):
    t = jnp.arange(seq, dtype=jnp.int32)
    row = (t // GRID_W).astype(F32)
    col = (t % GRID_W).astype(F32)
    half = HEAD_DIM // 2
    inv = ROPE_BASE ** (-jnp.arange(0, half, 2, dtype=F32) / half)
    ang_r = row[:, None] * inv
    ang_c = col[:, None] * inv
    ang = jnp.tile(jnp.concatenate([ang_r, ang_c], axis=-1), (1, LANES // (HEAD_DIM // 2)))
    return jnp.cos(ang), jnp.sin(ang)


def _layer(x, c, ctx, c_ctx, w_mod, b_mod, pre_g, post_g, w_in, conv_w, conv_b, rpb,
           w_out_conv, w_out_attn, w_o):
    batch, seq, _ = x.shape
    n_ctx = ctx.shape[1]
    rows = seq // GRID_W
    assert rows >= 2 * WIN_PAIRS and seq % LANES == 0 and batch <= 7

    cond8 = jnp.zeros((8, D_MODEL), F32).at[:batch].set(c).at[batch].set(c_ctx)
    mod = _modulation(cond8, w_mod, b_mod.reshape(1, -1))
    sh, sc, gt = (mod[:batch, k * D_MODEL:(k + 1) * D_MODEL].reshape(batch, 1, D_MODEL) for k in range(3))
    sh_c, sc_c = (mod[batch:batch + 1, k * D_MODEL:(k + 1) * D_MODEL] for k in range(2))

    pre_g2 = pre_g.reshape(1, D_MODEL)
    w_in_bf = w_in.astype(BF16)
    wqk_bf = _split_halves_qk(w_in_bf)
    kc, vc = _ctx_kv(ctx.reshape(batch * n_ctx, D_MODEL), sc_c, sh_c, pre_g2, wqk_bf, w_in_bf)
    kc = kc.reshape(batch, n_ctx, PART_W)
    vct = jnp.transpose(vc.reshape(batch, n_ctx, PART_W), (0, 2, 1))

    x2 = x.reshape(batch * seq, D_MODEL)
    cos, sin = _rope_tables(seq)
    tm_in = min(512, seq)
    p5, qt, vt = _inproj(x2, sc, sh, pre_g2, w_in_bf, wqk_bf, cos, sin, conv_w,
                         conv_b.reshape(1, -1), batch, seq, tm_in)

    bias = _bias_tables(rpb, rows)
    attn = _attention(p5, qt, vt, kc, vct, bias, batch, seq)

    tm_out = min(512, seq)
    out = _mixer(p5, attn, x2, gt, post_g.reshape(1, -1),
                 w_out_conv.astype(BF16), w_out_attn.astype(BF16), w_o.astype(BF16), seq, tm_out)
    return out.reshape(batch, seq, D_MODEL)


def kernel(x, c, ctx, c_ctx, w_mod, b_mod, pre_g, post_g, w_in, conv_w, conv_b, rpb,
           w_out_conv, w_out_attn, w_o):
    depth = w_mod.shape[0]
    assert depth == 1, "context stream update between layers is not implemented"
    return _layer(x, c, ctx, c_ctx, w_mod[0], b_mod[0], pre_g[0], post_g[0], w_in[0], conv_w[0],
                  conv_b[0], rpb[0], w_out_conv[0], w_out_attn[0], w_o[0])
```

```python
import functools

import numpy as np
import jax
import jax.numpy as jnp
from jax import lax
from jax.experimental import pallas as pl
from jax.experimental.pallas import tpu as pltpu

D_MODEL = 1024
GRID_W = 64
N_HEADS = 16
HEAD_DIM = 64
WIN_ROWS = 8
WIN_COLS = 16
ROPE_BASE = 10000.0
EPS = 1e-6
PART_W = 1024
N_PARTS_IN = 10
PART_BG, PART_CG, PART_XI, PART_ZA, PART_Q, PART_K, PART_V, PART_ZB, PART_GA, PART_GB = range(N_PARTS_IN)
OUT_A, OUT_K, OUT_SZB, OUT_SGA, OUT_SGB = range(5)
N_PARTS_OUT = 5
CONV_ROWS = 64
HALO = 16
LANES = 128
N_PAIRS = N_HEADS // 2
WIN_PAIRS = 5
NEG = -1e30
RED_ROWS = 64
ATTN_GROUP = 8
SUM_ROWS = 16
LOG2E = 1.4426950408889634
BF16 = jnp.bfloat16
F32 = jnp.float32


def _modulated_norm(x, g, sc, sh):
    ms = jnp.mean(x * x, axis=-1, keepdims=True)
    return x * lax.rsqrt(ms + EPS) * (g * (1.0 + sc)) + sh


def _modulation_kernel(c_ref, w_ref, b_ref, o_ref):
    c = c_ref[...]
    o_ref[...] = jnp.dot(c * jax.nn.sigmoid(c), w_ref[...], preferred_element_type=F32,
                         precision=lax.Precision.HIGHEST) + b_ref[...]


def _modulation(cond8, w_mod, b_mod):
    n = w_mod.shape[1]
    tn = 768
    return pl.pallas_call(
        _modulation_kernel,
        grid=(n // tn,),
        in_specs=[pl.BlockSpec((8, D_MODEL), lambda j: (0, 0)),
                  pl.BlockSpec((D_MODEL, tn), lambda j: (0, j)),
                  pl.BlockSpec((1, tn), lambda j: (0, j))],
        out_specs=pl.BlockSpec((8, tn), lambda j: (0, j)),
        out_shape=jax.ShapeDtypeStruct((8, n), F32),
        name="modulation",
    )(cond8, w_mod, b_mod)


def _window_mask():
    q = np.arange(GRID_W)[None, :]
    c = np.arange(GRID_W)[:, None]
    cs = np.clip(q - WIN_COLS // 2, 0, GRID_W - WIN_COLS)
    negm = np.where((c >= cs) & (c < cs + WIN_COLS), 0.0, NEG).astype(np.float32)
    return np.concatenate([negm, negm], axis=-1)


def _pair_window(t, n_row_pairs):
    if isinstance(t, int):
        sp = min(max(t - 2, 0), n_row_pairs - WIN_PAIRS)
        var = t if t < 2 else (t - (n_row_pairs - 2) + 3 if t >= n_row_pairs - 2 else 2)
        return sp, var
    sp = jnp.clip(t - 2, 0, n_row_pairs - WIN_PAIRS)
    var = jnp.where(t < 2, t, jnp.where(t >= n_row_pairs - 2, t - (n_row_pairs - 2) + 3, 2))
    return sp, var


N_BIAS_VAR = 5


def _bias_tables(rpb, rows):
    n_dr = 2 * WIN_ROWS - 1
    n_rp = rows // 2
    lines = jnp.concatenate([rpb[..., WIN_COLS - 1::-1],
                             jnp.zeros(rpb.shape[:2] + (LANES - (2 * WIN_COLS - 1),), F32),
                             rpb[..., :WIN_COLS - 1:-1]], axis=-1).reshape(N_PAIRS, 2 * n_dr, LANES)
    idx = np.full((N_BIAS_VAR, 2 * WIN_PAIRS, 2, 2), -1, np.int32)
    for v, t in enumerate([0, 1, 2, n_rp - 2, n_rp - 1]):
        sp, _ = _pair_window(t, n_rp)
        for w in range(2 * WIN_PAIRS):
            krow = 2 * sp + w
            for rr in range(2):
                r = 2 * t + rr
                rs = min(max(r - WIN_ROWS // 2, 0), rows - WIN_ROWS)
                if rs <= krow < rs + WIN_ROWS:
                    idx[v, w, :, rr] = np.arange(2) * n_dr + (krow - r + WIN_ROWS - 1)
    return pl.pallas_call(
        functools.partial(_bias_assemble_kernel, idx=idx),
        grid=(N_PAIRS,),
        in_specs=[pl.BlockSpec((1, 2 * n_dr, LANES), lambda p: (p, 0, 0)),
                  pl.BlockSpec((GRID_W, LANES), lambda p: (0, 0))],
        out_specs=pl.BlockSpec((1, N_BIAS_VAR, 2 * WIN_PAIRS * GRID_W, 2 * LANES), lambda p: (p, 0, 0, 0)),
        out_shape=jax.ShapeDtypeStruct((N_PAIRS, N_BIAS_VAR, 2 * WIN_PAIRS * GRID_W, 2 * LANES), F32),
        name="bias_assemble",
    )(lines, jnp.asarray(_window_mask()))


def _bias_assemble_kernel(line_ref, mask_ref, o_ref, *, idx):
    left = lax.broadcasted_iota(jnp.int32, (GRID_W, LANES), 1) < GRID_W
    masked = jnp.full((GRID_W, LANES), NEG, F32)
    blocks = {}

    def block(i, lane_offset):
        if i < 0:
            return masked
        if (i, lane_offset) not in blocks:
            line = jnp.broadcast_to(line_ref[0, i:i + 1, :] * LOG2E, (GRID_W, LANES))
            blocks[i, lane_offset] = pltpu.roll(line, lane_offset, 1, stride=1, stride_axis=0)
        return blocks[i, lane_offset]

    n_var, n_win, _, _ = idx.shape
    for v in range(n_var):
        for w in range(n_win):
            for hh in range(2):
                i0, i1 = (int(i) for i in idx[v, w, hh])
                tile = jnp.where(left, block(i0, 0), block(i1, GRID_W)) + mask_ref[...]
                o_ref[0, v, w * GRID_W:(w + 1) * GRID_W, hh * LANES:(hh + 1) * LANES] = tile


def _ctx_kv_kernel(x_ref, sc_ref, sh_ref, g_ref, wk_ref, wv_ref, k_ref, v_ref):
    h = _modulated_norm(x_ref[...], g_ref[...], sc_ref[...], sh_ref[...]).astype(BF16)
    k_ref[...] = jnp.dot(h, wk_ref[0], preferred_element_type=F32).astype(BF16)
    v_ref[...] = jnp.dot(h, wv_ref[...], preferred_element_type=F32).astype(BF16)


def _ctx_kv(ctx2, sc_c, sh_c, pre_g, wqk_bf, w_in_bf):
    m = ctx2.shape[0]

    def full(shape):
        return pl.BlockSpec(shape, lambda j: (0,) * len(shape))

    return pl.pallas_call(
        _ctx_kv_kernel,
        grid=(1,),
        in_specs=[full((m, D_MODEL)), full((1, D_MODEL)), full((1, D_MODEL)), full((1, D_MODEL)),
                  pl.BlockSpec((1, D_MODEL, PART_W), lambda j: (1, 0, 0)),
                  pl.BlockSpec((D_MODEL, PART_W), lambda j: (0, PART_V))],
        out_specs=[full((m, PART_W)), full((m, PART_W))],
        out_shape=[jax.ShapeDtypeStruct((m, PART_W), BF16)] * 2,
        name="ctx_kv",
    )(ctx2, sc_c, sh_c, pre_g, wqk_bf, w_in_bf)


def _split_halves_permutation():
    q4 = HEAD_DIM // 4
    old = np.arange(PART_W).reshape(N_HEADS, 2, 2, q4).transpose(2, 0, 1, 3).reshape(PART_W)
    perm = np.zeros((PART_W, PART_W), np.float32)
    perm[old, np.arange(PART_W)] = 1.0
    return perm


def _permute_kernel(w_ref, perm_ref, o_ref):
    o_ref[0] = jnp.dot(w_ref[...], perm_ref[...], preferred_element_type=F32).astype(BF16)


def _split_halves_qk(w_in_bf):
    return pl.pallas_call(
        _permute_kernel,
        grid=(2,),
        in_specs=[pl.BlockSpec((D_MODEL, PART_W), lambda j: (0, PART_Q + j)),
                  pl.BlockSpec((PART_W, PART_W), lambda j: (0, 0))],
        out_specs=pl.BlockSpec((1, D_MODEL, PART_W), lambda j: (j, 0, 0)),
        out_shape=jax.ShapeDtypeStruct((2, D_MODEL, PART_W), BF16),
        name="permute_qk",
    )(w_in_bf, jnp.asarray(_split_halves_permutation(), BF16))


def _inproj_kernel(x_ref, xp_ref, xn_ref, sc_ref, sh_ref, g_ref, w_ref, wq_ref, wk_ref, cos_ref, sin_ref,
                   cw_ref, cb_ref, p_ref, qt_ref, vt_ref, h_ref, u_ref, *, tiles_per_batch):
    tm = x_ref.shape[0]
    n_half = PART_W // (2 * LANES)
    t = pl.program_id(0) % tiles_per_batch
    for rows, ref in ((slice(0, HALO), xp_ref), (slice(HALO, HALO + tm), x_ref),
                      (slice(HALO + tm, 2 * HALO + tm), xn_ref)):
        h_ref[rows, :] = _modulated_norm(ref[...], g_ref[...], sc_ref[0], sh_ref[0]).astype(BF16)
    tile = slice(HALO, HALO + tm)

    def slab(r, s):
        return r[:, s * LANES:(s + 1) * LANES]

    def rope(r):
        firsts, seconds = [], []
        for s in range(n_half):
            a, b = slab(r, s), slab(r, n_half + s)
            firsts.append(a * cos_ref[...] - b * sin_ref[...])
            seconds.append(b * cos_ref[...] + a * sin_ref[...])
        return firsts + seconds

    def store_transposed(t_ref, s, u):
        ut = u.astype(BF16).T
        for j in range(tm // LANES):
            t_ref[0, j, s * LANES:(s + 1) * LANES, :] = ut[:, j * LANES:(j + 1) * LANES]

    def project(n, rows):
        if n == PART_Q:
            w = wq_ref[0]
        elif n == PART_K:
            w = wk_ref[0]
        else:
            w = w_ref[:, n * PART_W:(n + 1) * PART_W]
        return jnp.dot(h_ref[rows, :], w, preferred_element_type=F32)

    def put(col, val):
        p_ref[:, col * PART_W:(col + 1) * PART_W] = val.astype(BF16)

    b_gate = project(PART_BG, tile)
    u_ref[...] = project(PART_CG, slice(None)) * project(PART_XI, slice(None))
    u_ref[0:HALO, :] = jnp.where(t == 0, 0.0, u_ref[0:HALO, :])
    u_ref[HALO + tm:, :] = jnp.where(t == tiles_per_batch - 1, 0.0, u_ref[HALO + tm:, :])
    z_a = project(PART_ZA, tile)
    for c in range(tm // CONV_ROWS):
        lo = HALO + c * CONV_ROWS
        rows = slice(c * CONV_ROWS, (c + 1) * CONV_ROWS)
        conv = (u_ref[lo - 1:lo - 1 + CONV_ROWS, :] * cw_ref[0:1, :] + u_ref[lo:lo + CONV_ROWS, :] * cw_ref[1:2, :]
                + u_ref[lo + 1:lo + 1 + CONV_ROWS, :] * cw_ref[2:3, :] + cb_ref[...])
        zc = z_a[rows]
        p_ref[rows, OUT_A * PART_W:(OUT_A + 1) * PART_W] = (
            (zc * jax.nn.sigmoid(zc)) * b_gate[rows] * conv).astype(BF16)

    for s, rk in enumerate(rope(project(PART_Q, tile))):
        store_transposed(qt_ref, s, rk * (HEAD_DIM ** -0.5 * LOG2E))
    for s, rk in enumerate(rope(project(PART_K, tile))):
        p_ref[:, OUT_K * PART_W + s * LANES:OUT_K * PART_W + (s + 1) * LANES] = rk.astype(BF16)
    r = project(PART_V, tile)
    for s in range(PART_W // LANES):
        store_transposed(vt_ref, s, slab(r, s))
    z_b = project(PART_ZB, tile)
    put(OUT_SZB, z_b * jax.nn.sigmoid(z_b))
    put(OUT_SGA, jax.nn.sigmoid(project(PART_GA, tile)))
    put(OUT_SGB, jax.nn.sigmoid(project(PART_GB, tile)))


def _inproj(x2, sc, sh, pre_g, w_in_bf, wqk_bf, cos, sin, conv_w, conv_b, batch, seq, tm):
    m = x2.shape[0]
    tpb = seq // tm
    hb = tm // HALO
    t_spec = pl.BlockSpec((1, tm // LANES, PART_W, LANES), lambda i: (i // tpb, i % tpb, 0, 0))
    t_shape = jax.ShapeDtypeStruct((batch, seq // LANES, PART_W, LANES), BF16)

    def resident(shape, index):
        return pl.BlockSpec(shape, lambda i: index, pipeline_mode=pl.Buffered(1))

    return pl.pallas_call(
        functools.partial(_inproj_kernel, tiles_per_batch=tpb),
        grid=(m // tm,),
        in_specs=[pl.BlockSpec((tm, D_MODEL), lambda i: (i, 0)),
                  pl.BlockSpec((HALO, D_MODEL), lambda i: (jnp.maximum(i * hb - 1, 0), 0)),
                  pl.BlockSpec((HALO, D_MODEL), lambda i: (jnp.minimum((i + 1) * hb, m // HALO - 1), 0)),
                  pl.BlockSpec((1, 1, D_MODEL), lambda i: (i // tpb, 0, 0)),
                  pl.BlockSpec((1, 1, D_MODEL), lambda i: (i // tpb, 0, 0)),
                  pl.BlockSpec((1, D_MODEL), lambda i: (0, 0)),
                  resident((D_MODEL, N_PARTS_IN * PART_W), (0, 0)),
                  resident((1, D_MODEL, PART_W), (0, 0, 0)), resident((1, D_MODEL, PART_W), (1, 0, 0)),
                  pl.BlockSpec((tm, LANES), lambda i: (i % tpb, 0)),
                  pl.BlockSpec((tm, LANES), lambda i: (i % tpb, 0)),
                  pl.BlockSpec((3, PART_W), lambda i: (0, 0)),
                  pl.BlockSpec((1, PART_W), lambda i: (0, 0))],
        out_specs=[pl.BlockSpec((tm, N_PARTS_OUT * PART_W), lambda i: (i, 0)), t_spec, t_spec],
        out_shape=[jax.ShapeDtypeStruct((m, N_PARTS_OUT * PART_W), BF16), t_shape, t_shape],
        scratch_shapes=[pltpu.VMEM((tm + 2 * HALO, D_MODEL), BF16),
                        pltpu.VMEM((tm + 2 * HALO, PART_W), F32)],
        compiler_params=pltpu.CompilerParams(dimension_semantics=("arbitrary",)),
        name="in_projection",
    )(x2, x2, x2, sc, sh, pre_g, w_in_bf, wqk_bf, wqk_bf, cos, sin, conv_w, conv_b)


def _attn_kernel(qta_ref, qtb_ref, ka_ref, kb_ref, vt_ref, kca_ref, kcb_ref, vct_ref, bias_ref,
                 o_ref, *stage_refs, n_row_pairs):
    low = lax.broadcasted_iota(jnp.int32, (LANES, LANES), 0) < HEAD_DIM
    nb = WIN_PAIRS * LANES
    n_keys = nb + kca_ref.shape[1]
    last = n_row_pairs - 1
    head_of_dim = (lax.broadcasted_iota(jnp.int32, (2 * LANES, LANES), 0) % LANES) // (HEAD_DIM // 2)
    first_head = 2 * (pl.program_id(0) % 2)
    sel0 = head_of_dim == first_head
    sel1 = head_of_dim == first_head + 1

    def scores(t, slot):
        sp, var = _pair_window(t, n_row_pairs)
        qt = jnp.concatenate([qta_ref[0, t], qtb_ref[0, t]], axis=0)
        zero = jnp.zeros_like(qt)
        rhs = jnp.concatenate([jnp.where(sel0, qt, zero), jnp.where(sel1, qt, zero)], axis=1)
        rows = pl.ds(pl.multiple_of(sp * LANES, LANES), nb)
        kwin = jnp.concatenate([ka_ref[rows, :], kb_ref[rows, :]], axis=1)
        s_refs[slot][:nb, :] = jnp.dot(kwin, rhs, preferred_element_type=F32) + bias_ref[0, var]
        kc = jnp.concatenate([kca_ref[0], kcb_ref[0]], axis=1)
        s_refs[slot][nb:, :] = jnp.dot(kc, rhs, preferred_element_type=F32)

    def softmax(slot):
        s_ref, p_ref = s_refs[slot], p_refs[slot]
        chunks = [slice(c * RED_ROWS, (c + 1) * RED_ROWS) for c in range(n_keys // RED_ROWS)]
        m = s_ref[chunks[0], :]
        for c in chunks[1:]:
            m = jnp.maximum(m, s_ref[c, :])
        m = jnp.max(m, axis=0, keepdims=True)
        for c in chunks:
            p_ref[c, :] = jnp.exp2((s_ref[c, :] - m).astype(BF16))

    def values(t, slot):
        sp, _ = _pair_window(t, n_row_pairs)
        p = p_refs[slot][...]
        vext = jnp.concatenate([vt_ref[0, sp + g] for g in range(WIN_PAIRS)] + [vct_ref[0]], axis=1)
        vext = jnp.concatenate([vext, jnp.ones((SUM_ROWS, n_keys), BF16)], axis=0)
        ot = jnp.dot(vext, p, preferred_element_type=F32)
        ot = ot[:LANES] * (1.0 / ot[LANES:LANES + 1])
        out = jnp.where(low, ot[:, :LANES], ot[:, LANES:]).T
        o_ref[pl.ds(pl.multiple_of(t * LANES, LANES), LANES), :] = out.astype(BF16)

    g = ATTN_GROUP
    s_refs, p_refs = stage_refs[:2 * g], stage_refs[2 * g:]
    for j in range(g):
        scores(j, j)

    def group(t, cur, nxt):
        for j in range(g):
            scores(jnp.minimum(t + j + g, last), nxt + j)
            softmax(cur + j)
            values(t + j, cur + j)

    def body(i, carry):
        t = 2 * g * i
        group(t, 0, g)
        group(t + g, g, 0)
        return carry

    lax.fori_loop(0, n_row_pairs // (2 * g), body, 0)


def _attention(p5, qt, vt, kc, vct, bias, batch, seq):
    n_ctx = kc.shape[1]
    n_keys = WIN_PAIRS * LANES + n_ctx
    n_slots = 2 * ATTN_GROUP
    assert (seq // LANES) % n_slots == 0
    spp = PART_W // LANES
    half = spp // 2

    def t_spec(slab):
        return pl.BlockSpec((1, seq // LANES, LANES, LANES), lambda hp, b: (b, 0, slab(hp), 0))

    return pl.pallas_call(
        functools.partial(_attn_kernel, n_row_pairs=seq // LANES),
        grid=(N_PAIRS, batch),
        in_specs=[t_spec(lambda hp: hp // 2), t_spec(lambda hp: half + hp // 2),
                  pl.BlockSpec((seq, LANES), lambda hp, b: (b, OUT_K * spp + hp // 2)),
                  pl.BlockSpec((seq, LANES), lambda hp, b: (b, OUT_K * spp + half + hp // 2)),
                  t_spec(lambda hp: hp),
                  pl.BlockSpec((1, n_ctx, LANES), lambda hp, b: (b, 0, hp // 2)),
                  pl.BlockSpec((1, n_ctx, LANES), lambda hp, b: (b, 0, half + hp // 2)),
                  pl.BlockSpec((1, LANES, n_ctx), lambda hp, b: (b, hp, 0)),
                  pl.BlockSpec((1, N_BIAS_VAR, WIN_PAIRS * LANES, 2 * LANES),
                               lambda hp, b: (hp, 0, 0, 0))],
        out_specs=pl.BlockSpec((seq, LANES), lambda hp, b: (b, hp)),
        out_shape=jax.ShapeDtypeStruct((batch * seq, N_HEADS * HEAD_DIM), BF16),
        scratch_shapes=[pltpu.VMEM((n_keys, 2 * LANES), F32)] * n_slots
        + [pltpu.VMEM((n_keys, 2 * LANES), BF16)] * n_slots,
        compiler_params=pltpu.CompilerParams(dimension_semantics=("arbitrary", "arbitrary")),
        name="attention",
    )(qt, qt, p5, p5, vt, kc, kc, vct, bias)


def _mixer_kernel(a_ref, szb_ref, sga_ref, sgb_ref, at_ref, x_ref, gt_ref, pg_ref,
                  woc_ref, woa_ref, wo_ref, o_ref):
    y_a = jnp.dot(a_ref[...], woc_ref[...], preferred_element_type=F32)
    bb = szb_ref[...].astype(F32) * at_ref[...].astype(F32)
    y_b = jnp.dot(bb.astype(BF16), woa_ref[...], preferred_element_type=F32)
    merged = sga_ref[...].astype(F32) * y_a + sgb_ref[...].astype(F32) * y_b
    y = jnp.dot(merged.astype(BF16), wo_ref[...], preferred_element_type=F32)
    ms = jnp.mean(y * y, axis=-1, keepdims=True)
    o_ref[...] = x_ref[...] + gt_ref[0] * ((y * lax.rsqrt(ms + EPS)) * pg_ref[...])


def _mixer(p5, attn, x2, gt, post_g, woc, woa, wo, seq, tm):
    m = x2.shape[0]
    tpb = seq // tm

    def part(k):
        return pl.BlockSpec((tm, PART_W), lambda i: (i, k))

    def full(shape):
        return pl.BlockSpec(shape, lambda i: (0,) * len(shape))

    return pl.pallas_call(
        _mixer_kernel,
        grid=(m // tm,),
        in_specs=[part(OUT_A), part(OUT_SZB), part(OUT_SGA), part(OUT_SGB),
                  pl.BlockSpec((tm, PART_W), lambda i: (i, 0)),
                  pl.BlockSpec((tm, D_MODEL), lambda i: (i, 0)),
                  pl.BlockSpec((1, 1, D_MODEL), lambda i: (i // tpb, 0, 0)),
                  full((1, D_MODEL)),
                  full((PART_W, D_MODEL)), full((PART_W, D_MODEL)), full((D_MODEL, D_MODEL))],
        out_specs=pl.BlockSpec((tm, D_MODEL), lambda i: (i, 0)),
        out_shape=jax.ShapeDtypeStruct((m, D_MODEL), F32),
        compiler_params=pltpu.CompilerParams(dimension_semantics=("arbitrary",)),
        name="mixer_out",
    )(p5, p5, p5, p5, attn, x2, gt, post_g, woc, woa, wo)


def _rope_tables(seq):
    t = jnp.arange(seq, dtype=jnp.int32)
    row = (t // GRID_W).astype(F32)
    col = (t % GRID_W).astype(F32)
    half = HEAD_DIM // 2
    inv = ROPE_BASE ** (-jnp.arange(0, half, 2, dtype=F32) / half)
    ang_r = row[:, None] * inv
    ang_c = col[:, None] * inv
    ang = jnp.tile(jnp.concatenate([ang_r, ang_c], axis=-1), (1, LANES // (HEAD_DIM // 2)))
    return jnp.cos(ang), jnp.sin(ang)


def _layer(x, c, ctx, c_ctx, w_mod, b_mod, pre_g, post_g, w_in, conv_w, conv_b, rpb,
           w_out_conv, w_out_attn, w_o):
    batch, seq, _ = x.shape
    n_ctx = ctx.shape[1]
    rows = seq // GRID_W
    assert rows >= 2 * WIN_PAIRS and seq % LANES == 0 and batch <= 7

    cond8 = jnp.zeros((8, D_MODEL), F32).at[:batch].set(c).at[batch].set(c_ctx)
    mod = _modulation(cond8, w_mod, b_mod.reshape(1, -1))
    sh, sc, gt = (mod[:batch, k * D_MODEL:(k + 1) * D_MODEL].reshape(batch, 1, D_MODEL) for k in range(3))
    sh_c, sc_c = (mod[batch:batch + 1, k * D_MODEL:(k + 1) * D_MODEL] for k in range(2))

    pre_g2 = pre_g.reshape(1, D_MODEL)
    w_in_bf = w_in.astype(BF16)
    wqk_bf = _split_halves_qk(w_in_bf)
    kc, vc = _ctx_kv(ctx.reshape(batch * n_ctx, D_MODEL), sc_c, sh_c, pre_g2, wqk_bf, w_in_bf)
    kc = kc.reshape(batch, n_ctx, PART_W)
    vct = jnp.transpose(vc.reshape(batch, n_ctx, PART_W), (0, 2, 1))

    x2 = x.reshape(batch * seq, D_MODEL)
    cos, sin = _rope_tables(seq)
    tm_in = min(512, seq)
    p5, qt, vt = _inproj(x2, sc, sh, pre_g2, w_in_bf, wqk_bf, cos, sin, conv_w,
                         conv_b.reshape(1, -1), batch, seq, tm_in)

    bias = _bias_tables(rpb, rows)
    attn = _attention(p5, qt, vt, kc, vct, bias, batch, seq)

    tm_out = min(512, seq)
    out = _mixer(p5, attn, x2, gt, post_g.reshape(1, -1),
                 w_out_conv.astype(BF16), w_out_attn.astype(BF16), w_o.astype(BF16), seq, tm_out)
    return out.reshape(batch, seq, D_MODEL)


def kernel(x, c, ctx, c_ctx, w_mod, b_mod, pre_g, post_g, w_in, conv_w, conv_b, rpb,
           w_out_conv, w_out_attn, w_o):
    depth = w_mod.shape[0]
    assert depth == 1, "context stream update between layers is not implemented"
    return _layer(x, c, ctx, c_ctx, w_mod[0], b_mod[0], pre_g[0], post_g[0], w_in[0], conv_w[0],
                  conv_b[0], rpb[0], w_out_conv[0], w_out_attn[0], w_o[0])
```

```python
import functools

import numpy as np
import jax
import jax.numpy as jnp
from jax import lax
from jax.experimental import pallas as pl
from jax.experimental.pallas import tpu as pltpu

D_MODEL = 1024
GRID_W = 64
N_HEADS = 16
HEAD_DIM = 64
WIN_ROWS = 8
WIN_COLS = 16
ROPE_BASE = 10000.0
EPS = 1e-6
PART_W = 1024
N_PARTS_IN = 10
PART_BG, PART_CG, PART_XI, PART_ZA, PART_Q, PART_K, PART_V, PART_ZB, PART_GA, PART_GB = range(N_PARTS_IN)
OUT_A, OUT_K, OUT_SZB, OUT_SGA, OUT_SGB = range(5)
N_PARTS_OUT = 5
CONV_ROWS = 64
HALO = 16
LANES = 128
N_PAIRS = N_HEADS // 2
WIN_PAIRS = 5
NEG = -1e30
RED_ROWS = 64
ATTN_GROUP = 8
SUM_ROWS = 16
LOG2E = 1.4426950408889634
BF16 = jnp.bfloat16
F32 = jnp.float32


def _modulated_norm(x, g, sc, sh):
    ms = jnp.mean(x * x, axis=-1, keepdims=True)
    return x * lax.rsqrt(ms + EPS) * (g * (1.0 + sc)) + sh


def _modulation_kernel(c_ref, w_ref, b_ref, o_ref):
    c = c_ref[...]
    o_ref[...] = jnp.dot(c * jax.nn.sigmoid(c), w_ref[...], preferred_element_type=F32,
                         precision=lax.Precision.HIGHEST) + b_ref[...]


def _modulation(cond8, w_mod, b_mod):
    n = w_mod.shape[1]
    tn = 768
    return pl.pallas_call(
        _modulation_kernel,
        grid=(n // tn,),
        in_specs=[pl.BlockSpec((8, D_MODEL), lambda j: (0, 0)),
                  pl.BlockSpec((D_MODEL, tn), lambda j: (0, j)),
                  pl.BlockSpec((1, tn), lambda j: (0, j))],
        out_specs=pl.BlockSpec((8, tn), lambda j: (0, j)),
        out_shape=jax.ShapeDtypeStruct((8, n), F32),
        name="modulation",
    )(cond8, w_mod, b_mod)


def _window_mask():
    q = np.arange(GRID_W)[None, :]
    c = np.arange(GRID_W)[:, None]
    cs = np.clip(q - WIN_COLS // 2, 0, GRID_W - WIN_COLS)
    negm = np.where((c >= cs) & (c < cs + WIN_COLS), 0.0, NEG).astype(np.float32)
    return np.concatenate([negm, negm], axis=-1)


def _pair_window(t, n_row_pairs):
    if isinstance(t, int):
        sp = min(max(t - 2, 0), n_row_pairs - WIN_PAIRS)
        var = t if t < 2 else (t - (n_row_pairs - 2) + 3 if t >= n_row_pairs - 2 else 2)
        return sp, var
    sp = jnp.clip(t - 2, 0, n_row_pairs - WIN_PAIRS)
    var = jnp.where(t < 2, t, jnp.where(t >= n_row_pairs - 2, t - (n_row_pairs - 2) + 3, 2))
    return sp, var


N_BIAS_VAR = 5


def _bias_tables(rpb, rows):
    n_dr = 2 * WIN_ROWS - 1
    n_rp = rows // 2
    lines = jnp.concatenate([rpb[..., WIN_COLS - 1::-1],
                             jnp.zeros(rpb.shape[:2] + (LANES - (2 * WIN_COLS - 1),), F32),
                             rpb[..., :WIN_COLS - 1:-1]], axis=-1).reshape(N_PAIRS, 2 * n_dr, LANES)
    idx = np.full((N_BIAS_VAR, 2 * WIN_PAIRS, 2, 2), -1, np.int32)
    for v, t in enumerate([0, 1, 2, n_rp - 2, n_rp - 1]):
        sp, _ = _pair_window(t, n_rp)
        for w in range(2 * WIN_PAIRS):
            krow = 2 * sp + w
            for rr in range(2):
                r = 2 * t + rr
                rs = min(max(r - WIN_ROWS // 2, 0), rows - WIN_ROWS)
                if rs <= krow < rs + WIN_ROWS:
                    idx[v, w, :, rr] = np.arange(2) * n_dr + (krow - r + WIN_ROWS - 1)
    return pl.pallas_call(
        functools.partial(_bias_assemble_kernel, idx=idx),
        grid=(N_PAIRS,),
        in_specs=[pl.BlockSpec((1, 2 * n_dr, LANES), lambda p: (p, 0, 0)),
                  pl.BlockSpec((GRID_W, LANES), lambda p: (0, 0))],
        out_specs=pl.BlockSpec((1, N_BIAS_VAR, 2 * WIN_PAIRS * GRID_W, 2 * LANES), lambda p: (p, 0, 0, 0)),
        out_shape=jax.ShapeDtypeStruct((N_PAIRS, N_BIAS_VAR, 2 * WIN_PAIRS * GRID_W, 2 * LANES), F32),
        name="bias_assemble",
    )(lines, jnp.asarray(_window_mask()))


def _bias_assemble_kernel(line_ref, mask_ref, o_ref, *, idx):
    left = lax.broadcasted_iota(jnp.int32, (GRID_W, LANES), 1) < GRID_W
    masked = jnp.full((GRID_W, LANES), NEG, F32)
    blocks = {}

    def block(i, lane_offset):
        if i < 0:
            return masked
        if (i, lane_offset) not in blocks:
            line = jnp.broadcast_to(line_ref[0, i:i + 1, :] * LOG2E, (GRID_W, LANES))
            blocks[i, lane_offset] = pltpu.roll(line, lane_offset, 1, stride=1, stride_axis=0)
        return blocks[i, lane_offset]

    n_var, n_win, _, _ = idx.shape
    for v in range(n_var):
        for w in range(n_win):
            for hh in range(2):
                i0, i1 = (int(i) for i in idx[v, w, hh])
                tile = jnp.where(left, block(i0, 0), block(i1, GRID_W)) + mask_ref[...]
                o_ref[0, v, w * GRID_W:(w + 1) * GRID_W, hh * LANES:(hh + 1) * LANES] = tile


def _ctx_kv_kernel(x_ref, sc_ref, sh_ref, g_ref, wk_ref, wv_ref, k_ref, vt_ref):
    h = _modulated_norm(x_ref[...], g_ref[...], sc_ref[...], sh_ref[...]).astype(BF16)
    k_ref[...] = jnp.dot(h, wk_ref[0], preferred_element_type=F32).astype(BF16)
    v = jnp.dot(h, wv_ref[...], preferred_element_type=F32)
    n_ctx = vt_ref.shape[2]
    for b in range(vt_ref.shape[0]):
        vt_ref[b] = v[b * n_ctx:(b + 1) * n_ctx].astype(BF16).T


def _ctx_kv(ctx2, sc_c, sh_c, pre_g, wqk_bf, w_in_bf, batch):
    m = ctx2.shape[0]
    n_ctx = m // batch

    def full(shape):
        return pl.BlockSpec(shape, lambda j: (0,) * len(shape))

    return pl.pallas_call(
        _ctx_kv_kernel,
        grid=(1,),
        in_specs=[full((m, D_MODEL)), full((1, D_MODEL)), full((1, D_MODEL)), full((1, D_MODEL)),
                  pl.BlockSpec((1, D_MODEL, PART_W), lambda j: (1, 0, 0)),
                  pl.BlockSpec((D_MODEL, PART_W), lambda j: (0, PART_V))],
        out_specs=[full((m, PART_W)), full((batch, PART_W, n_ctx))],
        out_shape=[jax.ShapeDtypeStruct((m, PART_W), BF16),
                   jax.ShapeDtypeStruct((batch, PART_W, n_ctx), BF16)],
        name="ctx_kv",
    )(ctx2, sc_c, sh_c, pre_g, wqk_bf, w_in_bf)


def _split_halves_permutation():
    q4 = HEAD_DIM // 4
    old = np.arange(PART_W).reshape(N_HEADS, 2, 2, q4).transpose(2, 0, 1, 3).reshape(PART_W)
    perm = np.zeros((PART_W, PART_W), np.float32)
    perm[old, np.arange(PART_W)] = 1.0
    return perm


def _permute_kernel(w_ref, perm_ref, o_ref):
    o_ref[0] = jnp.dot(w_ref[...], perm_ref[...], preferred_element_type=F32).astype(BF16)


def _split_halves_qk(w_in_bf):
    return pl.pallas_call(
        _permute_kernel,
        grid=(2,),
        in_specs=[pl.BlockSpec((D_MODEL, PART_W), lambda j: (0, PART_Q + j)),
                  pl.BlockSpec((PART_W, PART_W), lambda j: (0, 0))],
        out_specs=pl.BlockSpec((1, D_MODEL, PART_W), lambda j: (j, 0, 0)),
        out_shape=jax.ShapeDtypeStruct((2, D_MODEL, PART_W), BF16),
        name="permute_qk",
    )(w_in_bf, jnp.asarray(_split_halves_permutation(), BF16))


def _inproj_kernel(x_ref, xp_ref, xn_ref, sc_ref, sh_ref, g_ref, w_ref, wq_ref, wk_ref, rowtab_ref, coltab_ref,
                   cw_ref, cb_ref, p_ref, qt_ref, vt_ref, h_ref, u_ref, *, tiles_per_batch):
    tm = x_ref.shape[0]
    n_half = PART_W // (2 * LANES)
    t = pl.program_id(0) % tiles_per_batch
    for rows, ref in ((slice(0, HALO), xp_ref), (slice(HALO, HALO + tm), x_ref),
                      (slice(HALO + tm, 2 * HALO + tm), xn_ref)):
        h_ref[rows, :] = _modulated_norm(ref[...], g_ref[...], sc_ref[0], sh_ref[0]).astype(BF16)
    tile = slice(HALO, HALO + tm)

    def slab(r, s):
        return r[:, s * LANES:(s + 1) * LANES]

    col_lane = (lax.broadcasted_iota(jnp.int32, (1, LANES), 1) // (HEAD_DIM // 4)) % 2 == 1

    def rope_table(k):
        return jnp.concatenate(
            [jnp.where(col_lane, coltab_ref[k, g * GRID_W:(g + 1) * GRID_W, :],
                       jnp.broadcast_to(rowtab_ref[k, g:g + 1, :], (GRID_W, LANES)))
             for g in range(tm // GRID_W)], axis=0)

    cos, sin = rope_table(0), rope_table(1)

    def rope(r):
        firsts, seconds = [], []
        for s in range(n_half):
            a, b = slab(r, s), slab(r, n_half + s)
            firsts.append(a * cos - b * sin)
            seconds.append(b * cos + a * sin)
        return firsts + seconds

    def store_transposed(t_ref, s, u):
        ut = u.astype(BF16).T
        for j in range(tm // LANES):
            t_ref[0, j, s * LANES:(s + 1) * LANES, :] = ut[:, j * LANES:(j + 1) * LANES]

    def project(n, rows):
        if n == PART_Q:
            w = wq_ref[0]
        elif n == PART_K:
            w = wk_ref[0]
        else:
            w = w_ref[:, n * PART_W:(n + 1) * PART_W]
        return jnp.dot(h_ref[rows, :], w, preferred_element_type=F32)

    def put(col, val):
        p_ref[:, col * PART_W:(col + 1) * PART_W] = val.astype(BF16)

    b_gate = project(PART_BG, tile)
    u_ref[...] = project(PART_CG, slice(None)) * project(PART_XI, slice(None))
    u_ref[0:HALO, :] = jnp.where(t == 0, 0.0, u_ref[0:HALO, :])
    u_ref[HALO + tm:, :] = jnp.where(t == tiles_per_batch - 1, 0.0, u_ref[HALO + tm:, :])
    z_a = project(PART_ZA, tile)
    for c in range(tm // CONV_ROWS):
        lo = HALO + c * CONV_ROWS
        rows = slice(c * CONV_ROWS, (c + 1) * CONV_ROWS)
        conv = (u_ref[lo - 1:lo - 1 + CONV_ROWS, :] * cw_ref[0:1, :] + u_ref[lo:lo + CONV_ROWS, :] * cw_ref[1:2, :]
                + u_ref[lo + 1:lo + 1 + CONV_ROWS, :] * cw_ref[2:3, :] + cb_ref[...])
        zc = z_a[rows]
        p_ref[rows, OUT_A * PART_W:(OUT_A + 1) * PART_W] = (
            (zc * jax.nn.sigmoid(zc)) * b_gate[rows] * conv).astype(BF16)

    for s, rk in enumerate(rope(project(PART_Q, tile))):
        store_transposed(qt_ref, s, rk * (HEAD_DIM ** -0.5 * LOG2E))
    for s, rk in enumerate(rope(project(PART_K, tile))):
        p_ref[:, OUT_K * PART_W + s * LANES:OUT_K * PART_W + (s + 1) * LANES] = rk.astype(BF16)
    r = project(PART_V, tile)
    for s in range(PART_W // LANES):
        store_transposed(vt_ref, s, slab(r, s))
    z_b = project(PART_ZB, tile)
    put(OUT_SZB, z_b * jax.nn.sigmoid(z_b))
    put(OUT_SGA, jax.nn.sigmoid(project(PART_GA, tile)))
    put(OUT_SGB, jax.nn.sigmoid(project(PART_GB, tile)))


def _inproj(x2, sc, sh, pre_g, w_in_bf, wqk_bf, rowtab, coltab, conv_w, conv_b, batch, seq, tm):
    m = x2.shape[0]
    tpb = seq // tm
    hb = tm // HALO
    t_spec = pl.BlockSpec((1, tm // LANES, PART_W, LANES), lambda i: (i // tpb, i % tpb, 0, 0))
    t_shape = jax.ShapeDtypeStruct((batch, seq // LANES, PART_W, LANES), BF16)

    def resident(shape, index):
        return pl.BlockSpec(shape, lambda i: index, pipeline_mode=pl.Buffered(1))

    return pl.pallas_call(
        functools.partial(_inproj_kernel, tiles_per_batch=tpb),
        grid=(m // tm,),
        in_specs=[pl.BlockSpec((tm, D_MODEL), lambda i: (i, 0)),
                  pl.BlockSpec((HALO, D_MODEL), lambda i: (jnp.maximum(i * hb - 1, 0), 0)),
                  pl.BlockSpec((HALO, D_MODEL), lambda i: (jnp.minimum((i + 1) * hb, m // HALO - 1), 0)),
                  pl.BlockSpec((1, 1, D_MODEL), lambda i: (i // tpb, 0, 0)),
                  pl.BlockSpec((1, 1, D_MODEL), lambda i: (i // tpb, 0, 0)),
                  pl.BlockSpec((1, D_MODEL), lambda i: (0, 0)),
                  resident((D_MODEL, N_PARTS_IN * PART_W), (0, 0)),
                  resident((1, D_MODEL, PART_W), (0, 0, 0)), resident((1, D_MODEL, PART_W), (1, 0, 0)),
                  pl.BlockSpec((2, tm // GRID_W, LANES), lambda i: (0, i % tpb, 0)),
                  pl.BlockSpec((2, tm, LANES), lambda i: (0, 0, 0)),
                  pl.BlockSpec((3, PART_W), lambda i: (0, 0)),
                  pl.BlockSpec((1, PART_W), lambda i: (0, 0))],
        out_specs=[pl.BlockSpec((tm, N_PARTS_OUT * PART_W), lambda i: (i, 0)), t_spec, t_spec],
        out_shape=[jax.ShapeDtypeStruct((m, N_PARTS_OUT * PART_W), BF16), t_shape, t_shape],
        scratch_shapes=[pltpu.VMEM((tm + 2 * HALO, D_MODEL), BF16),
                        pltpu.VMEM((tm + 2 * HALO, PART_W), F32)],
        compiler_params=pltpu.CompilerParams(dimension_semantics=("arbitrary",)),
        name="in_projection",
    )(x2, x2, x2, sc, sh, pre_g, w_in_bf, wqk_bf, wqk_bf, rowtab, coltab, conv_w, conv_b)


def _attn_kernel(qta_ref, qtb_ref, ka_ref, kb_ref, vt_ref, kca_ref, kcb_ref, vct_ref, bias_ref,
                 o_ref, *stage_refs, n_row_pairs):
    low = lax.broadcasted_iota(jnp.int32, (LANES, LANES), 0) < HEAD_DIM
    nb = WIN_PAIRS * LANES
    n_keys = nb + kca_ref.shape[1]
    last = n_row_pairs - 1
    head_of_dim = (lax.broadcasted_iota(jnp.int32, (2 * LANES, LANES), 0) % LANES) // (HEAD_DIM // 2)
    first_head = 2 * (pl.program_id(0) % 2)
    sel0 = head_of_dim == first_head
    sel1 = head_of_dim == first_head + 1

    def scores(t, slot):
        sp, var = _pair_window(t, n_row_pairs)
        qt = jnp.concatenate([qta_ref[0, t], qtb_ref[0, t]], axis=0)
        zero = jnp.zeros_like(qt)
        rhs = jnp.concatenate([jnp.where(sel0, qt, zero), jnp.where(sel1, qt, zero)], axis=1)
        rows = pl.ds(pl.multiple_of(sp * LANES, LANES), nb)
        kwin = jnp.concatenate([ka_ref[rows, :], kb_ref[rows, :]], axis=1)
        s_refs[slot][:nb, :] = jnp.dot(kwin, rhs, preferred_element_type=F32) + bias_ref[0, var]
        kc = jnp.concatenate([kca_ref[0], kcb_ref[0]], axis=1)
        s_refs[slot][nb:, :] = jnp.dot(kc, rhs, preferred_element_type=F32)

    def softmax(slot):
        s_ref, p_ref = s_refs[slot], p_refs[slot]
        chunks = [slice(c * RED_ROWS, (c + 1) * RED_ROWS) for c in range(n_keys // RED_ROWS)]
        m = s_ref[chunks[0], :]
        for c in chunks[1:]:
            m = jnp.maximum(m, s_ref[c, :])
        m = jnp.max(m, axis=0, keepdims=True)
        for c in chunks:
            p_ref[c, :] = jnp.exp2((s_ref[c, :] - m).astype(BF16))

    def values(t, slot):
        sp, _ = _pair_window(t, n_row_pairs)
        p = p_refs[slot][...]
        vext = jnp.concatenate([vt_ref[0, sp + g] for g in range(WIN_PAIRS)] + [vct_ref[0]], axis=1)
        vext = jnp.concatenate([vext, jnp.ones((SUM_ROWS, n_keys), BF16)], axis=0)
        ot = jnp.dot(vext, p, preferred_element_type=F32)
        ot = ot[:LANES] * (1.0 / ot[LANES:LANES + 1])
        out = jnp.where(low, ot[:, :LANES], ot[:, LANES:]).T
        o_ref[pl.ds(pl.multiple_of(t * LANES, LANES), LANES), :] = out.astype(BF16)

    g = ATTN_GROUP
    s_refs, p_refs = stage_refs[:2 * g], stage_refs[2 * g:]
    for j in range(g):
        scores(j, j)

    def group(t, cur, nxt):
        for j in range(g):
            scores(jnp.minimum(t + j + g, last), nxt + j)
            softmax(cur + j)
            values(t + j, cur + j)

    def body(i, carry):
        t = 2 * g * i
        group(t, 0, g)
        group(t + g, g, 0)
        return carry

    lax.fori_loop(0, n_row_pairs // (2 * g), body, 0)


def _attention(p5, qt, vt, kc, vct, bias, batch, seq):
    n_ctx = kc.shape[1]
    n_keys = WIN_PAIRS * LANES + n_ctx
    n_slots = 2 * ATTN_GROUP
    assert (seq // LANES) % n_slots == 0
    spp = PART_W // LANES
    half = spp // 2

    def t_spec(slab):
        return pl.BlockSpec((1, seq // LANES, LANES, LANES), lambda hp, b: (b, 0, slab(hp), 0))

    return pl.pallas_call(
        functools.partial(_attn_kernel, n_row_pairs=seq // LANES),
        grid=(N_PAIRS, batch),
        in_specs=[t_spec(lambda hp: hp // 2), t_spec(lambda hp: half + hp // 2),
                  pl.BlockSpec((seq, LANES), lambda hp, b: (b, OUT_K * spp + hp // 2)),
                  pl.BlockSpec((seq, LANES), lambda hp, b: (b, OUT_K * spp + half + hp // 2)),
                  t_spec(lambda hp: hp),
                  pl.BlockSpec((1, n_ctx, LANES), lambda hp, b: (b, 0, hp // 2)),
                  pl.BlockSpec((1, n_ctx, LANES), lambda hp, b: (b, 0, half + hp // 2)),
                  pl.BlockSpec((1, LANES, n_ctx), lambda hp, b: (b, hp, 0)),
                  pl.BlockSpec((1, N_BIAS_VAR, WIN_PAIRS * LANES, 2 * LANES),
                               lambda hp, b: (hp, 0, 0, 0))],
        out_specs=pl.BlockSpec((seq, LANES), lambda hp, b: (b, hp)),
        out_shape=jax.ShapeDtypeStruct((batch * seq, N_HEADS * HEAD_DIM), BF16),
        scratch_shapes=[pltpu.VMEM((n_keys, 2 * LANES), F32)] * n_slots
        + [pltpu.VMEM((n_keys, 2 * LANES), BF16)] * n_slots,
        compiler_params=pltpu.CompilerParams(dimension_semantics=("arbitrary", "arbitrary")),
        name="attention",
    )(qt, qt, p5, p5, vt, kc, kc, vct, bias)


def _mixer_kernel(a_ref, szb_ref, sga_ref, sgb_ref, at_ref, x_ref, gt_ref, pg_ref,
                  woc32_ref, woa32_ref, wo32_ref, o_ref, woc_ref, woa_ref, wo_ref):
    @pl.when(pl.program_id(0) == 0)
    def _():
        for src, dst in ((woc32_ref, woc_ref), (woa32_ref, woa_ref), (wo32_ref, wo_ref)):
            dst[...] = src[...].astype(BF16)

    y_a = jnp.dot(a_ref[...], woc_ref[...], preferred_element_type=F32)
    bb = szb_ref[...].astype(F32) * at_ref[...].astype(F32)
    y_b = jnp.dot(bb.astype(BF16), woa_ref[...], preferred_element_type=F32)
    merged = sga_ref[...].astype(F32) * y_a + sgb_ref[...].astype(F32) * y_b
    y = jnp.dot(merged.astype(BF16), wo_ref[...], preferred_element_type=F32)
    ms = jnp.mean(y * y, axis=-1, keepdims=True)
    o_ref[...] = x_ref[...] + gt_ref[0] * ((y * lax.rsqrt(ms + EPS)) * pg_ref[...])


def _mixer(p5, attn, x2, gt, post_g, woc, woa, wo, seq, tm):
    m = x2.shape[0]
    tpb = seq // tm

    def part(k):
        return pl.BlockSpec((tm, PART_W), lambda i: (i, k))

    def full(shape):
        return pl.BlockSpec(shape, lambda i: (0,) * len(shape))

    def once(shape):
        return pl.BlockSpec(shape, lambda i: (0,) * len(shape), pipeline_mode=pl.Buffered(1))

    return pl.pallas_call(
        _mixer_kernel,
        grid=(m // tm,),
        in_specs=[part(OUT_A), part(OUT_SZB), part(OUT_SGA), part(OUT_SGB),
                  pl.BlockSpec((tm, PART_W), lambda i: (i, 0)),
                  pl.BlockSpec((tm, D_MODEL), lambda i: (i, 0)),
                  pl.BlockSpec((1, 1, D_MODEL), lambda i: (i // tpb, 0, 0)),
                  full((1, D_MODEL)),
                  once((PART_W, D_MODEL)), once((PART_W, D_MODEL)), once((D_MODEL, D_MODEL))],
        out_specs=pl.BlockSpec((tm, D_MODEL), lambda i: (i, 0)),
        out_shape=jax.ShapeDtypeStruct((m, D_MODEL), F32),
        scratch_shapes=[pltpu.VMEM((PART_W, D_MODEL), BF16)] * 2 + [pltpu.VMEM((D_MODEL, D_MODEL), BF16)],
        compiler_params=pltpu.CompilerParams(dimension_semantics=("arbitrary",)),
        name="mixer_out",
    )(p5, p5, p5, p5, attn, x2, gt, post_g, woc, woa, wo)


def _rope_tables(seq, tm):
    half = HEAD_DIM // 2
    q4 = HEAD_DIM // 4
    inv = ROPE_BASE ** (-jnp.arange(0, half, 2, dtype=F32) / half)
    lane_freq = np.arange(LANES) % q4

    def tables(n):
        ang = (jnp.arange(n, dtype=F32)[:, None] * inv)[:, lane_freq]
        return jnp.stack([jnp.cos(ang), jnp.sin(ang)])

    return tables(seq // GRID_W), jnp.tile(tables(GRID_W), (1, tm // GRID_W, 1))


def _layer(x, c, ctx, c_ctx, w_mod, b_mod, pre_g, post_g, w_in, conv_w, conv_b, rpb,
           w_out_conv, w_out_attn, w_o):
    batch, seq, _ = x.shape
    n_ctx = ctx.shape[1]
    rows = seq // GRID_W
    assert rows >= 2 * WIN_PAIRS and seq % LANES == 0 and batch <= 7

    cond8 = jnp.zeros((8, D_MODEL), F32).at[:batch].set(c).at[batch].set(c_ctx)
    mod = _modulation(cond8, w_mod, b_mod.reshape(1, -1))
    sh, sc, gt = (mod[:batch, k * D_MODEL:(k + 1) * D_MODEL].reshape(batch, 1, D_MODEL) for k in range(3))
    sh_c, sc_c = (mod[batch:batch + 1, k * D_MODEL:(k + 1) * D_MODEL] for k in range(2))

    pre_g2 = pre_g.reshape(1, D_MODEL)
    w_in_bf = w_in.astype(BF16)
    wqk_bf = _split_halves_qk(w_in_bf)
    kc, vct = _ctx_kv(ctx.reshape(batch * n_ctx, D_MODEL), sc_c, sh_c, pre_g2, wqk_bf, w_in_bf, batch)
    kc = kc.reshape(batch, n_ctx, PART_W)

    x2 = x.reshape(batch * seq, D_MODEL)
    tm_in = min(512, seq)
    rowtab, coltab = _rope_tables(seq, tm_in)
    p5, qt, vt = _inproj(x2, sc, sh, pre_g2, w_in_bf, wqk_bf, rowtab, coltab, conv_w,
                         conv_b.reshape(1, -1), batch, seq, tm_in)

    bias = _bias_tables(rpb, rows)
    attn = _attention(p5, qt, vt, kc, vct, bias, batch, seq)

    tm_out = min(512, seq)
    out = _mixer(p5, attn, x2, gt, post_g.reshape(1, -1),
                 w_out_conv, w_out_attn, w_o, seq, tm_out)
    return out.reshape(batch, seq, D_MODEL)


def kernel(x, c, ctx, c_ctx, w_mod, b_mod, pre_g, post_g, w_in, conv_w, conv_b, rpb,
           w_out_conv, w_out_attn, w_o):
    depth = w_mod.shape[0]
    assert depth == 1, "context stream update between layers is not implemented"
    return _layer(x, c, ctx, c_ctx, w_mod[0], b_mod[0], pre_g[0], post_g[0], w_in[0], conv_w[0],
                  conv_b[0], rpb[0], w_out_conv[0], w_out_attn[0], w_o[0])
```

```python
import functools

import numpy as np
import jax
import jax.numpy as jnp
from jax import lax
from jax.experimental import pallas as pl
from jax.experimental.pallas import tpu as pltpu

D_MODEL = 1024
GRID_W = 64
N_HEADS = 16
HEAD_DIM = 64
WIN_ROWS = 8
WIN_COLS = 16
ROPE_BASE = 10000.0
EPS = 1e-6
PART_W = 1024
N_PARTS_IN = 10
PART_BG, PART_CG, PART_XI, PART_ZA, PART_Q, PART_K, PART_V, PART_ZB, PART_GA, PART_GB = range(N_PARTS_IN)
OUT_A, OUT_K, OUT_SZB, OUT_SGA, OUT_SGB = range(5)
N_PARTS_OUT = 5
W_CHUNK_ROWS = 16
W_SLOTS = 4
CONV_ROWS = 64
HALO = 16
LANES = 128
N_PAIRS = N_HEADS // 2
WIN_PAIRS = 5
NEG = -1e30
RED_ROWS = 64
ATTN_GROUP = 8
SUM_ROWS = 16
LOG2E = 1.4426950408889634
BF16 = jnp.bfloat16
F32 = jnp.float32


def _modulated_norm(x, g, sc, sh):
    ms = jnp.mean(x * x, axis=-1, keepdims=True)
    return x * lax.rsqrt(ms + EPS) * (g * (1.0 + sc)) + sh


def _modulation_kernel(c_ref, w_ref, b_ref, o_ref):
    c = c_ref[...]
    o_ref[...] = jnp.dot(c * jax.nn.sigmoid(c), w_ref[...], preferred_element_type=F32,
                         precision=lax.Precision.HIGHEST) + b_ref[...]


def _modulation(cond8, w_mod, b_mod):
    n = w_mod.shape[1]
    tn = 768
    return pl.pallas_call(
        _modulation_kernel,
        grid=(n // tn,),
        in_specs=[pl.BlockSpec((8, D_MODEL), lambda j: (0, 0)),
                  pl.BlockSpec((D_MODEL, tn), lambda j: (0, j)),
                  pl.BlockSpec((1, tn), lambda j: (0, j))],
        out_specs=pl.BlockSpec((8, tn), lambda j: (0, j)),
        out_shape=jax.ShapeDtypeStruct((8, n), F32),
        name="modulation",
    )(cond8, w_mod, b_mod)


def _window_mask():
    q = np.arange(GRID_W)[None, :]
    c = np.arange(GRID_W)[:, None]
    cs = np.clip(q - WIN_COLS // 2, 0, GRID_W - WIN_COLS)
    negm = np.where((c >= cs) & (c < cs + WIN_COLS), 0.0, NEG).astype(np.float32)
    return np.concatenate([negm, negm], axis=-1)


def _pair_window(t, n_row_pairs):
    if isinstance(t, int):
        sp = min(max(t - 2, 0), n_row_pairs - WIN_PAIRS)
        var = t if t < 2 else (t - (n_row_pairs - 2) + 3 if t >= n_row_pairs - 2 else 2)
        return sp, var
    sp = jnp.clip(t - 2, 0, n_row_pairs - WIN_PAIRS)
    var = jnp.where(t < 2, t, jnp.where(t >= n_row_pairs - 2, t - (n_row_pairs - 2) + 3, 2))
    return sp, var


N_BIAS_VAR = 5


def _bias_tables(rpb, rows):
    n_dr = 2 * WIN_ROWS - 1
    n_rp = rows // 2
    lines = jnp.concatenate([rpb[..., WIN_COLS - 1::-1],
                             jnp.zeros(rpb.shape[:2] + (LANES - (2 * WIN_COLS - 1),), F32),
                             rpb[..., :WIN_COLS - 1:-1]], axis=-1).reshape(N_PAIRS, 2 * n_dr, LANES)
    idx = np.full((N_BIAS_VAR, 2 * WIN_PAIRS, 2, 2), -1, np.int32)
    for v, t in enumerate([0, 1, 2, n_rp - 2, n_rp - 1]):
        sp, _ = _pair_window(t, n_rp)
        for w in range(2 * WIN_PAIRS):
            krow = 2 * sp + w
            for rr in range(2):
                r = 2 * t + rr
                rs = min(max(r - WIN_ROWS // 2, 0), rows - WIN_ROWS)
                if rs <= krow < rs + WIN_ROWS:
                    idx[v, w, :, rr] = np.arange(2) * n_dr + (krow - r + WIN_ROWS - 1)
    return pl.pallas_call(
        functools.partial(_bias_assemble_kernel, idx=idx),
        grid=(N_PAIRS,),
        in_specs=[pl.BlockSpec((1, 2 * n_dr, LANES), lambda p: (p, 0, 0)),
                  pl.BlockSpec((GRID_W, LANES), lambda p: (0, 0))],
        out_specs=pl.BlockSpec((1, N_BIAS_VAR, 2 * WIN_PAIRS * GRID_W, 2 * LANES), lambda p: (p, 0, 0, 0)),
        out_shape=jax.ShapeDtypeStruct((N_PAIRS, N_BIAS_VAR, 2 * WIN_PAIRS * GRID_W, 2 * LANES), F32),
        name="bias_assemble",
    )(lines, jnp.asarray(_window_mask()))


def _bias_assemble_kernel(line_ref, mask_ref, o_ref, *, idx):
    left = lax.broadcasted_iota(jnp.int32, (GRID_W, LANES), 1) < GRID_W
    masked = jnp.full((GRID_W, LANES), NEG, F32)
    blocks = {}

    def block(i, lane_offset):
        if i < 0:
            return masked
        if (i, lane_offset) not in blocks:
            line = jnp.broadcast_to(line_ref[0, i:i + 1, :] * LOG2E, (GRID_W, LANES))
            blocks[i, lane_offset] = pltpu.roll(line, lane_offset, 1, stride=1, stride_axis=0)
        return blocks[i, lane_offset]

    n_var, n_win, _, _ = idx.shape
    for v in range(n_var):
        for w in range(n_win):
            for hh in range(2):
                i0, i1 = (int(i) for i in idx[v, w, hh])
                tile = jnp.where(left, block(i0, 0), block(i1, GRID_W)) + mask_ref[...]
                o_ref[0, v, w * GRID_W:(w + 1) * GRID_W, hh * LANES:(hh + 1) * LANES] = tile


def _ctx_kv_kernel(x_ref, sc_ref, sh_ref, g_ref, wk_ref, wv_ref, perm_ref, k_ref, vt_ref):
    h = _modulated_norm(x_ref[...], g_ref[...], sc_ref[...], sh_ref[...]).astype(BF16)
    k = jnp.dot(h, wk_ref[...].astype(BF16), preferred_element_type=F32).astype(BF16)
    k_ref[...] = jnp.dot(k, perm_ref[...], preferred_element_type=F32).astype(BF16)
    v = jnp.dot(h, wv_ref[...].astype(BF16), preferred_element_type=F32)
    n_ctx = vt_ref.shape[2]
    for b in range(vt_ref.shape[0]):
        vt_ref[b] = v[b * n_ctx:(b + 1) * n_ctx].astype(BF16).T


def _ctx_kv(ctx2, sc_c, sh_c, pre_g, w_in, perm, batch):
    m = ctx2.shape[0]
    n_ctx = m // batch

    def full(shape):
        return pl.BlockSpec(shape, lambda j: (0,) * len(shape))

    return pl.pallas_call(
        _ctx_kv_kernel,
        grid=(1,),
        in_specs=[full((m, D_MODEL)), full((1, D_MODEL)), full((1, D_MODEL)), full((1, D_MODEL)),
                  pl.BlockSpec((D_MODEL, PART_W), lambda j: (0, PART_K)),
                  pl.BlockSpec((D_MODEL, PART_W), lambda j: (0, PART_V)),
                  full((PART_W, PART_W))],
        out_specs=[full((m, PART_W)), full((batch, PART_W, n_ctx))],
        out_shape=[jax.ShapeDtypeStruct((m, PART_W), BF16),
                   jax.ShapeDtypeStruct((batch, PART_W, n_ctx), BF16)],
        name="ctx_kv",
    )(ctx2, sc_c, sh_c, pre_g, w_in, w_in, perm)


def _split_halves_permutation():
    q4 = HEAD_DIM // 4
    old = np.arange(PART_W).reshape(N_HEADS, 2, 2, q4).transpose(2, 0, 1, 3).reshape(PART_W)
    perm = np.zeros((PART_W, PART_W), np.float32)
    perm[old, np.arange(PART_W)] = 1.0
    return perm


def _stage_weights(w32_hbm, perm_ref, w_ref, wqk_ref, stage_ref, sem):
    n_slots = stage_ref.shape[0]
    n_chunks = w32_hbm.shape[0] // W_CHUNK_ROWS

    def chunk_copy(c):
        return pltpu.make_async_copy(w32_hbm.at[c * W_CHUNK_ROWS:(c + 1) * W_CHUNK_ROWS, :],
                                     stage_ref.at[c % n_slots], sem.at[c % n_slots])

    for c in range(n_slots - 1):
        chunk_copy(c).start()
    for c in range(n_chunks):
        if c + n_slots - 1 < n_chunks:
            chunk_copy(c + n_slots - 1).start()
        chunk_copy(c).wait()
        w_ref[c * W_CHUNK_ROWS:(c + 1) * W_CHUNK_ROWS, :] = stage_ref[c % n_slots].astype(BF16)
    for j, part in enumerate((PART_Q, PART_K)):
        wqk_ref[j] = jnp.dot(w_ref[:, part * PART_W:(part + 1) * PART_W], perm_ref[...],
                             preferred_element_type=F32).astype(BF16)


def _inproj_kernel(x_ref, xp_ref, xn_ref, sc_ref, sh_ref, g_ref, w32_hbm, perm_ref, rowtab_ref, coltab_ref,
                   cw_ref, cb_ref, p_ref, qt_ref, vt_ref, h_ref, u_ref, w_ref, wqk_ref, stage_ref, sem,
                   *, tiles_per_batch):
    tm = x_ref.shape[0]
    n_half = PART_W // (2 * LANES)
    t = pl.program_id(0) % tiles_per_batch

    @pl.when(pl.program_id(0) == 0)
    def _():
        _stage_weights(w32_hbm, perm_ref, w_ref, wqk_ref, stage_ref, sem)

    for rows, ref in ((slice(0, HALO), xp_ref), (slice(HALO, HALO + tm), x_ref),
                      (slice(HALO + tm, 2 * HALO + tm), xn_ref)):
        h_ref[rows, :] = _modulated_norm(ref[...], g_ref[...], sc_ref[0], sh_ref[0]).astype(BF16)
    tile = slice(HALO, HALO + tm)

    def slab(r, s):
        return r[:, s * LANES:(s + 1) * LANES]

    col_lane = (lax.broadcasted_iota(jnp.int32, (1, LANES), 1) // (HEAD_DIM // 4)) % 2 == 1

    def rope_table(k):
        return jnp.concatenate(
            [jnp.where(col_lane, coltab_ref[k, g * GRID_W:(g + 1) * GRID_W, :],
                       jnp.broadcast_to(rowtab_ref[k, g:g + 1, :], (GRID_W, LANES)))
             for g in range(tm // GRID_W)], axis=0)

    cos, sin = rope_table(0), rope_table(1)

    def rope(r):
        firsts, seconds = [], []
        for s in range(n_half):
            a, b = slab(r, s), slab(r, n_half + s)
            firsts.append(a * cos - b * sin)
            seconds.append(b * cos + a * sin)
        return firsts + seconds

    def store_transposed(t_ref, s, u):
        ut = u.astype(BF16).T
        for j in range(tm // LANES):
            t_ref[0, j, s * LANES:(s + 1) * LANES, :] = ut[:, j * LANES:(j + 1) * LANES]

    def project(n, rows):
        if n == PART_Q:
            w = wqk_ref[0]
        elif n == PART_K:
            w = wqk_ref[1]
        else:
            w = w_ref[:, n * PART_W:(n + 1) * PART_W]
        return jnp.dot(h_ref[rows, :], w, preferred_element_type=F32)

    def put(col, val):
        p_ref[:, col * PART_W:(col + 1) * PART_W] = val.astype(BF16)

    b_gate = project(PART_BG, tile)
    u_ref[...] = project(PART_CG, slice(None)) * project(PART_XI, slice(None))
    u_ref[0:HALO, :] = jnp.where(t == 0, 0.0, u_ref[0:HALO, :])
    u_ref[HALO + tm:, :] = jnp.where(t == tiles_per_batch - 1, 0.0, u_ref[HALO + tm:, :])
    z_a = project(PART_ZA, tile)
    for c in range(tm // CONV_ROWS):
        lo = HALO + c * CONV_ROWS
        rows = slice(c * CONV_ROWS, (c + 1) * CONV_ROWS)
        conv = (u_ref[lo - 1:lo - 1 + CONV_ROWS, :] * cw_ref[0:1, :] + u_ref[lo:lo + CONV_ROWS, :] * cw_ref[1:2, :]
                + u_ref[lo + 1:lo + 1 + CONV_ROWS, :] * cw_ref[2:3, :] + cb_ref[...])
        zc = z_a[rows]
        p_ref[rows, OUT_A * PART_W:(OUT_A + 1) * PART_W] = (
            (zc * jax.nn.sigmoid(zc)) * b_gate[rows] * conv).astype(BF16)

    for s, rk in enumerate(rope(project(PART_Q, tile))):
        store_transposed(qt_ref, s, rk * (HEAD_DIM ** -0.5 * LOG2E))
    for s, rk in enumerate(rope(project(PART_K, tile))):
        p_ref[:, OUT_K * PART_W + s * LANES:OUT_K * PART_W + (s + 1) * LANES] = rk.astype(BF16)
    r = project(PART_V, tile)
    for s in range(PART_W // LANES):
        store_transposed(vt_ref, s, slab(r, s))
    z_b = project(PART_ZB, tile)
    put(OUT_SZB, z_b * jax.nn.sigmoid(z_b))
    put(OUT_SGA, jax.nn.sigmoid(project(PART_GA, tile)))
    put(OUT_SGB, jax.nn.sigmoid(project(PART_GB, tile)))


def _inproj(x2, sc, sh, pre_g, w_in, perm, rowtab, coltab, conv_w, conv_b, batch, seq, tm):
    m = x2.shape[0]
    tpb = seq // tm
    hb = tm // HALO
    t_spec = pl.BlockSpec((1, tm // LANES, PART_W, LANES), lambda i: (i // tpb, i % tpb, 0, 0))
    t_shape = jax.ShapeDtypeStruct((batch, seq // LANES, PART_W, LANES), BF16)

    def resident(shape, index):
        return pl.BlockSpec(shape, lambda i: index, pipeline_mode=pl.Buffered(1))

    return pl.pallas_call(
        functools.partial(_inproj_kernel, tiles_per_batch=tpb),
        grid=(m // tm,),
        in_specs=[pl.BlockSpec((tm, D_MODEL), lambda i: (i, 0)),
                  pl.BlockSpec((HALO, D_MODEL), lambda i: (jnp.maximum(i * hb - 1, 0), 0)),
                  pl.BlockSpec((HALO, D_MODEL), lambda i: (jnp.minimum((i + 1) * hb, m // HALO - 1), 0)),
                  pl.BlockSpec((1, 1, D_MODEL), lambda i: (i // tpb, 0, 0)),
                  pl.BlockSpec((1, 1, D_MODEL), lambda i: (i // tpb, 0, 0)),
                  pl.BlockSpec((1, D_MODEL), lambda i: (0, 0)),
                  pl.BlockSpec(memory_space=pl.ANY),
                  resident((PART_W, PART_W), (0, 0)),
                  pl.BlockSpec((2, tm // GRID_W, LANES), lambda i: (0, i % tpb, 0)),
                  pl.BlockSpec((2, tm, LANES), lambda i: (0, 0, 0)),
                  pl.BlockSpec((3, PART_W), lambda i: (0, 0)),
                  pl.BlockSpec((1, PART_W), lambda i: (0, 0))],
        out_specs=[pl.BlockSpec((tm, N_PARTS_OUT * PART_W), lambda i: (i, 0)), t_spec, t_spec],
        out_shape=[jax.ShapeDtypeStruct((m, N_PARTS_OUT * PART_W), BF16), t_shape, t_shape],
        scratch_shapes=[pltpu.VMEM((tm + 2 * HALO, D_MODEL), BF16),
                        pltpu.VMEM((tm + 2 * HALO, PART_W), F32),
                        pltpu.VMEM((D_MODEL, N_PARTS_IN * PART_W), BF16),
                        pltpu.VMEM((2, D_MODEL, PART_W), BF16),
                        pltpu.VMEM((W_SLOTS, W_CHUNK_ROWS, N_PARTS_IN * PART_W), F32),
                        pltpu.SemaphoreType.DMA((W_SLOTS,))],
        compiler_params=pltpu.CompilerParams(dimension_semantics=("arbitrary",)),
        name="in_projection",
    )(x2, x2, x2, sc, sh, pre_g, w_in, perm, rowtab, coltab, conv_w, conv_b)


def _attn_kernel(qta_ref, qtb_ref, ka_ref, kb_ref, vt_ref, kca_ref, kcb_ref, vct_ref, bias_ref,
                 o_ref, *stage_refs, n_row_pairs):
    low = lax.broadcasted_iota(jnp.int32, (LANES, LANES), 0) < HEAD_DIM
    nb = WIN_PAIRS * LANES
    n_keys = nb + kca_ref.shape[1]
    last = n_row_pairs - 1
    head_of_dim = (lax.broadcasted_iota(jnp.int32, (2 * LANES, LANES), 0) % LANES) // (HEAD_DIM // 2)
    first_head = 2 * (pl.program_id(0) % 2)
    sel0 = head_of_dim == first_head
    sel1 = head_of_dim == first_head + 1

    def scores(t, slot):
        sp, var = _pair_window(t, n_row_pairs)
        qt = jnp.concatenate([qta_ref[0, t], qtb_ref[0, t]], axis=0)
        zero = jnp.zeros_like(qt)
        rhs = jnp.concatenate([jnp.where(sel0, qt, zero), jnp.where(sel1, qt, zero)], axis=1)
        rows = pl.ds(pl.multiple_of(sp * LANES, LANES), nb)
        kwin = jnp.concatenate([ka_ref[rows, :], kb_ref[rows, :]], axis=1)
        s_refs[slot][:nb, :] = jnp.dot(kwin, rhs, preferred_element_type=F32) + bias_ref[0, var]
        kc = jnp.concatenate([kca_ref[0], kcb_ref[0]], axis=1)
        s_refs[slot][nb:, :] = jnp.dot(kc, rhs, preferred_element_type=F32)

    def softmax(slot):
        s_ref, p_ref = s_refs[slot], p_refs[slot]
        chunks = [slice(c * RED_ROWS, (c + 1) * RED_ROWS) for c in range(n_keys // RED_ROWS)]
        m = s_ref[chunks[0], :]
        for c in chunks[1:]:
            m = jnp.maximum(m, s_ref[c, :])
        m = jnp.max(m, axis=0, keepdims=True)
        for c in chunks:
            p_ref[c, :] = jnp.exp2((s_ref[c, :] - m).astype(BF16))

    def values(t, slot):
        sp, _ = _pair_window(t, n_row_pairs)
        p = p_refs[slot][...]
        vext = jnp.concatenate([vt_ref[0, sp + g] for g in range(WIN_PAIRS)] + [vct_ref[0]], axis=1)
        vext = jnp.concatenate([vext, jnp.ones((SUM_ROWS, n_keys), BF16)], axis=0)
        ot = jnp.dot(vext, p, preferred_element_type=F32)
        ot = ot[:LANES] * (1.0 / ot[LANES:LANES + 1])
        out = jnp.where(low, ot[:, :LANES], ot[:, LANES:]).T
        o_ref[pl.ds(pl.multiple_of(t * LANES, LANES), LANES), :] = out.astype(BF16)

    g = ATTN_GROUP
    s_refs, p_refs = stage_refs[:2 * g], stage_refs[2 * g:]
    for j in range(g):
        scores(j, j)

    def group(t, cur, nxt):
        for j in range(g):
            scores(jnp.minimum(t + j + g, last), nxt + j)
            softmax(cur + j)
            values(t + j, cur + j)

    def body(i, carry):
        t = 2 * g * i
        group(t, 0, g)
        group(t + g, g, 0)
        return carry

    lax.fori_loop(0, n_row_pairs // (2 * g), body, 0)


def _attention(p5, qt, vt, kc, vct, bias, batch, seq):
    n_ctx = kc.shape[1]
    n_keys = WIN_PAIRS * LANES + n_ctx
    n_slots = 2 * ATTN_GROUP
    assert (seq // LANES) % n_slots == 0
    spp = PART_W // LANES
    half = spp // 2

    def t_spec(slab):
        return pl.BlockSpec((1, seq // LANES, LANES, LANES), lambda hp, b: (b, 0, slab(hp), 0))

    return pl.pallas_call(
        functools.partial(_attn_kernel, n_row_pairs=seq // LANES),
        grid=(N_PAIRS, batch),
        in_specs=[t_spec(lambda hp: hp // 2), t_spec(lambda hp: half + hp // 2),
                  pl.BlockSpec((seq, LANES), lambda hp, b: (b, OUT_K * spp + hp // 2)),
                  pl.BlockSpec((seq, LANES), lambda hp, b: (b, OUT_K * spp + half + hp // 2)),
                  t_spec(lambda hp: hp),
                  pl.BlockSpec((1, n_ctx, LANES), lambda hp, b: (b, 0, hp // 2)),
                  pl.BlockSpec((1, n_ctx, LANES), lambda hp, b: (b, 0, half + hp // 2)),
                  pl.BlockSpec((1, LANES, n_ctx), lambda hp, b: (b, hp, 0)),
                  pl.BlockSpec((1, N_BIAS_VAR, WIN_PAIRS * LANES, 2 * LANES),
                               lambda hp, b: (hp, 0, 0, 0))],
        out_specs=pl.BlockSpec((seq, LANES), lambda hp, b: (b, hp)),
        out_shape=jax.ShapeDtypeStruct((batch * seq, N_HEADS * HEAD_DIM), BF16),
        scratch_shapes=[pltpu.VMEM((n_keys, 2 * LANES), F32)] * n_slots
        + [pltpu.VMEM((n_keys, 2 * LANES), BF16)] * n_slots,
        compiler_params=pltpu.CompilerParams(dimension_semantics=("arbitrary", "arbitrary")),
        name="attention",
    )(qt, qt, p5, p5, vt, kc, kc, vct, bias)


def _mixer_kernel(a_ref, szb_ref, sga_ref, sgb_ref, at_ref, x_ref, gt_ref, pg_ref,
                  woc32_ref, woa32_ref, wo32_ref, o_ref, woc_ref, woa_ref, wo_ref):
    @pl.when(pl.program_id(0) == 0)
    def _():
        for src, dst in ((woc32_ref, woc_ref), (woa32_ref, woa_ref), (wo32_ref, wo_ref)):
            dst[...] = src[...].astype(BF16)

    y_a = jnp.dot(a_ref[...], woc_ref[...], preferred_element_type=F32)
    bb = szb_ref[...].astype(F32) * at_ref[...].astype(F32)
    y_b = jnp.dot(bb.astype(BF16), woa_ref[...], preferred_element_type=F32)
    merged = sga_ref[...].astype(F32) * y_a + sgb_ref[...].astype(F32) * y_b
    y = jnp.dot(merged.astype(BF16), wo_ref[...], preferred_element_type=F32)
    ms = jnp.mean(y * y, axis=-1, keepdims=True)
    o_ref[...] = x_ref[...] + gt_ref[0] * ((y * lax.rsqrt(ms + EPS)) * pg_ref[...])


def _mixer(p5, attn, x2, gt, post_g, woc, woa, wo, seq, tm):
    m = x2.shape[0]
    tpb = seq // tm

    def part(k):
        return pl.BlockSpec((tm, PART_W), lambda i: (i, k))

    def full(shape):
        return pl.BlockSpec(shape, lambda i: (0,) * len(shape))

    def once(shape):
        return pl.BlockSpec(shape, lambda i: (0,) * len(shape), pipeline_mode=pl.Buffered(1))

    return pl.pallas_call(
        _mixer_kernel,
        grid=(m // tm,),
        in_specs=[part(OUT_A), part(OUT_SZB), part(OUT_SGA), part(OUT_SGB),
                  pl.BlockSpec((tm, PART_W), lambda i: (i, 0)),
                  pl.BlockSpec((tm, D_MODEL), lambda i: (i, 0)),
                  pl.BlockSpec((1, 1, D_MODEL), lambda i: (i // tpb, 0, 0)),
                  full((1, D_MODEL)),
                  once((PART_W, D_MODEL)), once((PART_W, D_MODEL)), once((D_MODEL, D_MODEL))],
        out_specs=pl.BlockSpec((tm, D_MODEL), lambda i: (i, 0)),
        out_shape=jax.ShapeDtypeStruct((m, D_MODEL), F32),
        scratch_shapes=[pltpu.VMEM((PART_W, D_MODEL), BF16)] * 2 + [pltpu.VMEM((D_MODEL, D_MODEL), BF16)],
        compiler_params=pltpu.CompilerParams(dimension_semantics=("arbitrary",)),
        name="mixer_out",
    )(p5, p5, p5, p5, attn, x2, gt, post_g, woc, woa, wo)


def _rope_tables(seq, tm):
    half = HEAD_DIM // 2
    q4 = HEAD_DIM // 4
    inv = ROPE_BASE ** (-jnp.arange(0, half, 2, dtype=F32) / half)
    lane_freq = np.arange(LANES) % q4

    def tables(n):
        ang = (jnp.arange(n, dtype=F32)[:, None] * inv)[:, lane_freq]
        return jnp.stack([jnp.cos(ang), jnp.sin(ang)])

    return tables(seq // GRID_W), jnp.tile(tables(GRID_W), (1, tm // GRID_W, 1))


def _layer(x, c, ctx, c_ctx, w_mod, b_mod, pre_g, post_g, w_in, conv_w, conv_b, rpb,
           w_out_conv, w_out_attn, w_o):
    batch, seq, _ = x.shape
    n_ctx = ctx.shape[1]
    rows = seq // GRID_W
    assert rows >= 2 * WIN_PAIRS and seq % LANES == 0 and batch <= 7

    cond8 = jnp.zeros((8, D_MODEL), F32).at[:batch].set(c).at[batch].set(c_ctx)
    mod = _modulation(cond8, w_mod, b_mod.reshape(1, -1))
    sh, sc, gt = (mod[:batch, k * D_MODEL:(k + 1) * D_MODEL].reshape(batch, 1, D_MODEL) for k in range(3))
    sh_c, sc_c = (mod[batch:batch + 1, k * D_MODEL:(k + 1) * D_MODEL] for k in range(2))

    pre_g2 = pre_g.reshape(1, D_MODEL)
    perm = jnp.asarray(_split_halves_permutation(), BF16)
    kc, vct = _ctx_kv(ctx.reshape(batch * n_ctx, D_MODEL), sc_c, sh_c, pre_g2, w_in, perm, batch)
    kc = kc.reshape(batch, n_ctx, PART_W)

    x2 = x.reshape(batch * seq, D_MODEL)
    tm_in = min(512, seq)
    rowtab, coltab = _rope_tables(seq, tm_in)
    p5, qt, vt = _inproj(x2, sc, sh, pre_g2, w_in, perm, rowtab, coltab, conv_w,
                         conv_b.reshape(1, -1), batch, seq, tm_in)

    bias = _bias_tables(rpb, rows)
    attn = _attention(p5, qt, vt, kc, vct, bias, batch, seq)

    tm_out = min(512, seq)
    out = _mixer(p5, attn, x2, gt, post_g.reshape(1, -1),
                 w_out_conv, w_out_attn, w_o, seq, tm_out)
    return out.reshape(batch, seq, D_MODEL)


def kernel(x, c, ctx, c_ctx, w_mod, b_mod, pre_g, post_g, w_in, conv_w, conv_b, rpb,
           w_out_conv, w_out_attn, w_o):
    depth = w_mod.shape[0]
    assert depth == 1, "context stream update between layers is not implemented"
    return _layer(x, c, ctx, c_ctx, w_mod[0], b_mod[0], pre_g[0], post_g[0], w_in[0], conv_w[0],
                  conv_b[0], rpb[0], w_out_conv[0], w_out_attn[0], w_o[0])
```

```python
import functools

import numpy as np
import jax
import jax.numpy as jnp
from jax import lax
from jax.experimental import pallas as pl
from jax.experimental.pallas import tpu as pltpu

D_MODEL = 1024
GRID_W = 64
N_HEADS = 16
HEAD_DIM = 64
WIN_ROWS = 8
WIN_COLS = 16
ROPE_BASE = 10000.0
EPS = 1e-6
PART_W = 1024
N_PARTS_IN = 10
PART_BG, PART_CG, PART_XI, PART_ZA, PART_Q, PART_K, PART_V, PART_ZB, PART_GA, PART_GB = range(N_PARTS_IN)
OUT_A, OUT_K, OUT_SZB, OUT_SGA, OUT_SGB = range(5)
N_PARTS_OUT = 5
W_CHUNK_ROWS = 16
W_SLOTS = 4
CONV_ROWS = 64
HALO = 16
LANES = 128
N_PAIRS = N_HEADS // 2
WIN_PAIRS = 5
NEG = -1e30
MAX_ROWS = 16
EXP_ROWS = 32
ATTN_GROUP = 8
SUM_ROWS = 16
LOG2E = 1.4426950408889634
BF16 = jnp.bfloat16
F32 = jnp.float32


def _modulated_norm(x, g, sc, sh):
    ms = jnp.mean(x * x, axis=-1, keepdims=True)
    return x * lax.rsqrt(ms + EPS) * (g * (1.0 + sc)) + sh


def _modulation_kernel(c_ref, w_ref, b_ref, o_ref):
    c = c_ref[...]
    o_ref[...] = jnp.dot(c * jax.nn.sigmoid(c), w_ref[...], preferred_element_type=F32,
                         precision=lax.Precision.HIGHEST) + b_ref[...]


def _modulation(cond8, w_mod, b_mod):
    n = w_mod.shape[1]
    tn = 768
    return pl.pallas_call(
        _modulation_kernel,
        grid=(n // tn,),
        in_specs=[pl.BlockSpec((8, D_MODEL), lambda j: (0, 0)),
                  pl.BlockSpec((D_MODEL, tn), lambda j: (0, j)),
                  pl.BlockSpec((1, tn), lambda j: (0, j))],
        out_specs=pl.BlockSpec((8, tn), lambda j: (0, j)),
        out_shape=jax.ShapeDtypeStruct((8, n), F32),
        name="modulation",
    )(cond8, w_mod, b_mod)


def _window_mask():
    q = np.arange(GRID_W)[None, :]
    c = np.arange(GRID_W)[:, None]
    cs = np.clip(q - WIN_COLS // 2, 0, GRID_W - WIN_COLS)
    negm = np.where((c >= cs) & (c < cs + WIN_COLS), 0.0, NEG).astype(np.float32)
    return np.concatenate([negm, negm], axis=-1)


def _pair_window(t, n_row_pairs):
    if isinstance(t, int):
        sp = min(max(t - 2, 0), n_row_pairs - WIN_PAIRS)
        var = t if t < 2 else (t - (n_row_pairs - 2) + 3 if t >= n_row_pairs - 2 else 2)
        return sp, var
    sp = jnp.clip(t - 2, 0, n_row_pairs - WIN_PAIRS)
    var = jnp.where(t < 2, t, jnp.where(t >= n_row_pairs - 2, t - (n_row_pairs - 2) + 3, 2))
    return sp, var


N_BIAS_VAR = 5


def _bias_tables(rpb, rows):
    n_dr = 2 * WIN_ROWS - 1
    n_rp = rows // 2
    lines = jnp.concatenate([rpb[..., WIN_COLS - 1::-1],
                             jnp.zeros(rpb.shape[:2] + (LANES - (2 * WIN_COLS - 1),), F32),
                             rpb[..., :WIN_COLS - 1:-1]], axis=-1).reshape(N_PAIRS, 2 * n_dr, LANES)
    idx = np.full((N_BIAS_VAR, 2 * WIN_PAIRS, 2, 2), -1, np.int32)
    for v, t in enumerate([0, 1, 2, n_rp - 2, n_rp - 1]):
        sp, _ = _pair_window(t, n_rp)
        for w in range(2 * WIN_PAIRS):
            krow = 2 * sp + w
            for rr in range(2):
                r = 2 * t + rr
                rs = min(max(r - WIN_ROWS // 2, 0), rows - WIN_ROWS)
                if rs <= krow < rs + WIN_ROWS:
                    idx[v, w, :, rr] = np.arange(2) * n_dr + (krow - r + WIN_ROWS - 1)
    return pl.pallas_call(
        functools.partial(_bias_assemble_kernel, idx=idx),
        grid=(N_PAIRS,),
        in_specs=[pl.BlockSpec((1, 2 * n_dr, LANES), lambda p: (p, 0, 0)),
                  pl.BlockSpec((GRID_W, LANES), lambda p: (0, 0))],
        out_specs=pl.BlockSpec((1, N_BIAS_VAR, 2 * WIN_PAIRS * GRID_W, 2 * LANES), lambda p: (p, 0, 0, 0)),
        out_shape=jax.ShapeDtypeStruct((N_PAIRS, N_BIAS_VAR, 2 * WIN_PAIRS * GRID_W, 2 * LANES), F32),
        name="bias_assemble",
    )(lines, jnp.asarray(_window_mask()))


def _bias_assemble_kernel(line_ref, mask_ref, o_ref, *, idx):
    left = lax.broadcasted_iota(jnp.int32, (GRID_W, LANES), 1) < GRID_W
    masked = jnp.full((GRID_W, LANES), NEG, F32)
    blocks = {}

    def block(i, lane_offset):
        if i < 0:
            return masked
        if (i, lane_offset) not in blocks:
            line = jnp.broadcast_to(line_ref[0, i:i + 1, :] * LOG2E, (GRID_W, LANES))
            blocks[i, lane_offset] = pltpu.roll(line, lane_offset, 1, stride=1, stride_axis=0)
        return blocks[i, lane_offset]

    n_var, n_win, _, _ = idx.shape
    for v in range(n_var):
        for w in range(n_win):
            for hh in range(2):
                i0, i1 = (int(i) for i in idx[v, w, hh])
                tile = jnp.where(left, block(i0, 0), block(i1, GRID_W)) + mask_ref[...]
                o_ref[0, v, w * GRID_W:(w + 1) * GRID_W, hh * LANES:(hh + 1) * LANES] = tile


def _ctx_kv_kernel(x_ref, sc_ref, sh_ref, g_ref, wk_ref, wv_ref, perm_ref, k_ref, vt_ref):
    h = _modulated_norm(x_ref[...], g_ref[...], sc_ref[...], sh_ref[...]).astype(BF16)
    k = jnp.dot(h, wk_ref[...].astype(BF16), preferred_element_type=F32).astype(BF16)
    k_ref[...] = jnp.dot(k, perm_ref[...], preferred_element_type=F32).astype(BF16)
    v = jnp.dot(h, wv_ref[...].astype(BF16), preferred_element_type=F32)
    n_ctx = vt_ref.shape[2]
    for b in range(vt_ref.shape[0]):
        vt_ref[b] = v[b * n_ctx:(b + 1) * n_ctx].astype(BF16).T


def _ctx_kv(ctx2, sc_c, sh_c, pre_g, w_in, perm, batch):
    m = ctx2.shape[0]
    n_ctx = m // batch

    def full(shape):
        return pl.BlockSpec(shape, lambda j: (0,) * len(shape))

    return pl.pallas_call(
        _ctx_kv_kernel,
        grid=(1,),
        in_specs=[full((m, D_MODEL)), full((1, D_MODEL)), full((1, D_MODEL)), full((1, D_MODEL)),
                  pl.BlockSpec((D_MODEL, PART_W), lambda j: (0, PART_K)),
                  pl.BlockSpec((D_MODEL, PART_W), lambda j: (0, PART_V)),
                  full((PART_W, PART_W))],
        out_specs=[full((m, PART_W)), full((batch, PART_W, n_ctx))],
        out_shape=[jax.ShapeDtypeStruct((m, PART_W), BF16),
                   jax.ShapeDtypeStruct((batch, PART_W, n_ctx), BF16)],
        name="ctx_kv",
    )(ctx2, sc_c, sh_c, pre_g, w_in, w_in, perm)


def _split_halves_permutation():
    q4 = HEAD_DIM // 4
    old = np.arange(PART_W).reshape(N_HEADS, 2, 2, q4).transpose(2, 0, 1, 3).reshape(PART_W)
    perm = np.zeros((PART_W, PART_W), np.float32)
    perm[old, np.arange(PART_W)] = 1.0
    return perm


def _stage_weights(w32_hbm, perm_ref, w_ref, wqk_ref, stage_ref, sem):
    n_slots = stage_ref.shape[0]
    n_chunks = w32_hbm.shape[0] // W_CHUNK_ROWS

    def chunk_copy(c):
        return pltpu.make_async_copy(w32_hbm.at[c * W_CHUNK_ROWS:(c + 1) * W_CHUNK_ROWS, :],
                                     stage_ref.at[c % n_slots], sem.at[c % n_slots])

    for c in range(n_slots - 1):
        chunk_copy(c).start()
    for c in range(n_chunks):
        if c + n_slots - 1 < n_chunks:
            chunk_copy(c + n_slots - 1).start()
        chunk_copy(c).wait()
        w_ref[c * W_CHUNK_ROWS:(c + 1) * W_CHUNK_ROWS, :] = stage_ref[c % n_slots].astype(BF16)
    for j, part in enumerate((PART_Q, PART_K)):
        wqk_ref[j] = jnp.dot(w_ref[:, part * PART_W:(part + 1) * PART_W], perm_ref[...],
                             preferred_element_type=F32).astype(BF16)


def _inproj_kernel(x_ref, xp_ref, xn_ref, sc_ref, sh_ref, g_ref, w32_hbm, perm_ref, rowtab_ref, coltab_ref,
                   cw_ref, cb_ref, p_ref, qt_ref, vt_ref, h_ref, u_ref, w_ref, wqk_ref, stage_ref, sem,
                   *, tiles_per_batch):
    tm = x_ref.shape[0]
    n_half = PART_W // (2 * LANES)
    t = pl.program_id(0) % tiles_per_batch

    @pl.when(pl.program_id(0) == 0)
    def _():
        _stage_weights(w32_hbm, perm_ref, w_ref, wqk_ref, stage_ref, sem)

    for rows, ref in ((slice(0, HALO), xp_ref), (slice(HALO, HALO + tm), x_ref),
                      (slice(HALO + tm, 2 * HALO + tm), xn_ref)):
        h_ref[rows, :] = _modulated_norm(ref[...], g_ref[...], sc_ref[0], sh_ref[0]).astype(BF16)
    tile = slice(HALO, HALO + tm)

    def slab(r, s):
        return r[:, s * LANES:(s + 1) * LANES]

    col_lane = (lax.broadcasted_iota(jnp.int32, (1, LANES), 1) // (HEAD_DIM // 4)) % 2 == 1

    def rope_table(k):
        return jnp.concatenate(
            [jnp.where(col_lane, coltab_ref[k, g * GRID_W:(g + 1) * GRID_W, :],
                       jnp.broadcast_to(rowtab_ref[k, g:g + 1, :], (GRID_W, LANES)))
             for g in range(tm // GRID_W)], axis=0)

    cos, sin = rope_table(0), rope_table(1)

    def rope(r):
        firsts, seconds = [], []
        for s in range(n_half):
            a, b = slab(r, s), slab(r, n_half + s)
            firsts.append(a * cos - b * sin)
            seconds.append(b * cos + a * sin)
        return firsts + seconds

    def store_transposed(t_ref, s, u):
        ut = u.astype(BF16).T
        for j in range(tm // LANES):
            t_ref[0, j, s * LANES:(s + 1) * LANES, :] = ut[:, j * LANES:(j + 1) * LANES]

    def project(n, rows):
        if n == PART_Q:
            w = wqk_ref[0]
        elif n == PART_K:
            w = wqk_ref[1]
        else:
            w = w_ref[:, n * PART_W:(n + 1) * PART_W]
        return jnp.dot(h_ref[rows, :], w, preferred_element_type=F32)

    def put(col, val):
        p_ref[:, col * PART_W:(col + 1) * PART_W] = val.astype(BF16)

    b_gate = project(PART_BG, tile)
    u_ref[...] = project(PART_CG, slice(None)) * project(PART_XI, slice(None))
    u_ref[0:HALO, :] = jnp.where(t == 0, 0.0, u_ref[0:HALO, :])
    u_ref[HALO + tm:, :] = jnp.where(t == tiles_per_batch - 1, 0.0, u_ref[HALO + tm:, :])
    z_a = project(PART_ZA, tile)
    for c in range(tm // CONV_ROWS):
        lo = HALO + c * CONV_ROWS
        rows = slice(c * CONV_ROWS, (c + 1) * CONV_ROWS)
        conv = (u_ref[lo - 1:lo - 1 + CONV_ROWS, :] * cw_ref[0:1, :] + u_ref[lo:lo + CONV_ROWS, :] * cw_ref[1:2, :]
                + u_ref[lo + 1:lo + 1 + CONV_ROWS, :] * cw_ref[2:3, :] + cb_ref[...])
        zc = z_a[rows]
        p_ref[rows, OUT_A * PART_W:(OUT_A + 1) * PART_W] = (
            (zc * jax.nn.sigmoid(zc)) * b_gate[rows] * conv).astype(BF16)

    for s, rk in enumerate(rope(project(PART_Q, tile))):
        store_transposed(qt_ref, s, rk * (HEAD_DIM ** -0.5 * LOG2E))
    for s, rk in enumerate(rope(project(PART_K, tile))):
        p_ref[:, OUT_K * PART_W + s * LANES:OUT_K * PART_W + (s + 1) * LANES] = rk.astype(BF16)
    r = project(PART_V, tile)
    for s in range(PART_W // LANES):
        store_transposed(vt_ref, s, slab(r, s))
    z_b = project(PART_ZB, tile)
    put(OUT_SZB, z_b * jax.nn.sigmoid(z_b))
    put(OUT_SGA, jax.nn.sigmoid(project(PART_GA, tile)))
    put(OUT_SGB, jax.nn.sigmoid(project(PART_GB, tile)))


def _inproj(x2, sc, sh, pre_g, w_in, perm, rowtab, coltab, conv_w, conv_b, batch, seq, tm):
    m = x2.shape[0]
    tpb = seq // tm
    hb = tm // HALO
    t_spec = pl.BlockSpec((1, tm // LANES, PART_W, LANES), lambda i: (i // tpb, i % tpb, 0, 0))
    t_shape = jax.ShapeDtypeStruct((batch, seq // LANES, PART_W, LANES), BF16)

    def resident(shape, index):
        return pl.BlockSpec(shape, lambda i: index, pipeline_mode=pl.Buffered(1))

    return pl.pallas_call(
        functools.partial(_inproj_kernel, tiles_per_batch=tpb),
        grid=(m // tm,),
        in_specs=[pl.BlockSpec((tm, D_MODEL), lambda i: (i, 0)),
                  pl.BlockSpec((HALO, D_MODEL), lambda i: (jnp.maximum(i * hb - 1, 0), 0)),
                  pl.BlockSpec((HALO, D_MODEL), lambda i: (jnp.minimum((i + 1) * hb, m // HALO - 1), 0)),
                  pl.BlockSpec((1, 1, D_MODEL), lambda i: (i // tpb, 0, 0)),
                  pl.BlockSpec((1, 1, D_MODEL), lambda i: (i // tpb, 0, 0)),
                  pl.BlockSpec((1, D_MODEL), lambda i: (0, 0)),
                  pl.BlockSpec(memory_space=pl.ANY),
                  resident((PART_W, PART_W), (0, 0)),
                  pl.BlockSpec((2, tm // GRID_W, LANES), lambda i: (0, i % tpb, 0)),
                  pl.BlockSpec((2, tm, LANES), lambda i: (0, 0, 0)),
                  pl.BlockSpec((3, PART_W), lambda i: (0, 0)),
                  pl.BlockSpec((1, PART_W), lambda i: (0, 0))],
        out_specs=[pl.BlockSpec((tm, N_PARTS_OUT * PART_W), lambda i: (i, 0)), t_spec, t_spec],
        out_shape=[jax.ShapeDtypeStruct((m, N_PARTS_OUT * PART_W), BF16), t_shape, t_shape],
        scratch_shapes=[pltpu.VMEM((tm + 2 * HALO, D_MODEL), BF16),
                        pltpu.VMEM((tm + 2 * HALO, PART_W), F32),
                        pltpu.VMEM((D_MODEL, N_PARTS_IN * PART_W), BF16),
                        pltpu.VMEM((2, D_MODEL, PART_W), BF16),
                        pltpu.VMEM((W_SLOTS, W_CHUNK_ROWS, N_PARTS_IN * PART_W), F32),
                        pltpu.SemaphoreType.DMA((W_SLOTS,))],
        compiler_params=pltpu.CompilerParams(dimension_semantics=("arbitrary",)),
        name="in_projection",
    )(x2, x2, x2, sc, sh, pre_g, w_in, perm, rowtab, coltab, conv_w, conv_b)


def _attn_kernel(qta_ref, qtb_ref, ka_ref, kb_ref, vt_ref, kca_ref, kcb_ref, vct_ref, bias_ref,
                 o_ref, *stage_refs, n_row_pairs):
    low = lax.broadcasted_iota(jnp.int32, (LANES, LANES), 0) < HEAD_DIM
    nb = WIN_PAIRS * LANES
    n_keys = nb + kca_ref.shape[1]
    last = n_row_pairs - 1
    head_of_dim = (lax.broadcasted_iota(jnp.int32, (2 * LANES, LANES), 0) % LANES) // (HEAD_DIM // 2)
    first_head = 2 * (pl.program_id(0) % 2)
    sel0 = head_of_dim == first_head
    sel1 = head_of_dim == first_head + 1

    def scores(t, slot):
        sp, var = _pair_window(t, n_row_pairs)
        qt = jnp.concatenate([qta_ref[0, t], qtb_ref[0, t]], axis=0)
        zero = jnp.zeros_like(qt)
        rhs = jnp.concatenate([jnp.where(sel0, qt, zero), jnp.where(sel1, qt, zero)], axis=1)
        rows = pl.ds(pl.multiple_of(sp * LANES, LANES), nb)
        kwin = jnp.concatenate([ka_ref[rows, :], kb_ref[rows, :]], axis=1)
        s_refs[slot][:nb, :] = jnp.dot(kwin, rhs, preferred_element_type=F32) + bias_ref[0, var]
        kc = jnp.concatenate([kca_ref[0], kcb_ref[0]], axis=1)
        s_refs[slot][nb:, :] = jnp.dot(kc, rhs, preferred_element_type=F32)

    def softmax(slot):
        s_ref, p_ref = s_refs[slot], p_refs[slot]
        m = s_ref[:MAX_ROWS, :]
        for c in range(1, n_keys // MAX_ROWS):
            m = jnp.maximum(m, s_ref[c * MAX_ROWS:(c + 1) * MAX_ROWS, :])
        m = jnp.max(m, axis=0, keepdims=True)
        for c in [slice(c * EXP_ROWS, (c + 1) * EXP_ROWS) for c in range(n_keys // EXP_ROWS)]:
            p_ref[c, :] = jnp.exp2((s_ref[c, :] - m).astype(BF16))

    def values(t, slot):
        sp, _ = _pair_window(t, n_row_pairs)
        p = p_refs[slot][...]
        vext = jnp.concatenate([vt_ref[0, sp + g] for g in range(WIN_PAIRS)] + [vct_ref[0]], axis=1)
        vext = jnp.concatenate([vext, jnp.ones((SUM_ROWS, n_keys), BF16)], axis=0)
        ot = jnp.dot(vext, p, preferred_element_type=F32)
        ot = ot[:LANES] * (1.0 / ot[LANES:LANES + 1])
        out = jnp.where(low, ot[:, :LANES], ot[:, LANES:]).T
        o_ref[pl.ds(pl.multiple_of(t * LANES, LANES), LANES), :] = out.astype(BF16)

    g = ATTN_GROUP
    s_refs, p_refs = stage_refs[:2 * g], stage_refs[2 * g:]
    for j in range(g):
        scores(j, j)

    def group(t, cur, nxt):
        for j in range(g):
            scores(jnp.minimum(t + j + g, last), nxt + j)
            softmax(cur + j)
            values(t + j, cur + j)

    def body(i, carry):
        t = 2 * g * i
        group(t, 0, g)
        group(t + g, g, 0)
        return carry

    lax.fori_loop(0, n_row_pairs // (2 * g), body, 0)


def _attention(p5, qt, vt, kc, vct, bias, batch, seq):
    n_ctx = kc.shape[1]
    n_keys = WIN_PAIRS * LANES + n_ctx
    n_slots = 2 * ATTN_GROUP
    assert (seq // LANES) % n_slots == 0
    spp = PART_W // LANES
    half = spp // 2

    def t_spec(slab):
        return pl.BlockSpec((1, seq // LANES, LANES, LANES), lambda hp, b: (b, 0, slab(hp), 0))

    return pl.pallas_call(
        functools.partial(_attn_kernel, n_row_pairs=seq // LANES),
        grid=(N_PAIRS, batch),
        in_specs=[t_spec(lambda hp: hp // 2), t_spec(lambda hp: half + hp // 2),
                  pl.BlockSpec((seq, LANES), lambda hp, b: (b, OUT_K * spp + hp // 2)),
                  pl.BlockSpec((seq, LANES), lambda hp, b: (b, OUT_K * spp + half + hp // 2)),
                  t_spec(lambda hp: hp),
                  pl.BlockSpec((1, n_ctx, LANES), lambda hp, b: (b, 0, hp // 2)),
                  pl.BlockSpec((1, n_ctx, LANES), lambda hp, b: (b, 0, half + hp // 2)),
                  pl.BlockSpec((1, LANES, n_ctx), lambda hp, b: (b, hp, 0)),
                  pl.BlockSpec((1, N_BIAS_VAR, WIN_PAIRS * LANES, 2 * LANES),
                               lambda hp, b: (hp, 0, 0, 0))],
        out_specs=pl.BlockSpec((seq, LANES), lambda hp, b: (b, hp)),
        out_shape=jax.ShapeDtypeStruct((batch * seq, N_HEADS * HEAD_DIM), BF16),
        scratch_shapes=[pltpu.VMEM((n_keys, 2 * LANES), F32)] * n_slots
        + [pltpu.VMEM((n_keys, 2 * LANES), BF16)] * n_slots,
        compiler_params=pltpu.CompilerParams(dimension_semantics=("arbitrary", "arbitrary")),
        name="attention",
    )(qt, qt, p5, p5, vt, kc, kc, vct, bias)


def _mixer_kernel(a_ref, szb_ref, sga_ref, sgb_ref, at_ref, x_ref, gt_ref, pg_ref,
                  woc32_ref, woa32_ref, wo32_ref, o_ref, woc_ref, woa_ref, wo_ref):
    @pl.when(pl.program_id(0) == 0)
    def _():
        for src, dst in ((woc32_ref, woc_ref), (woa32_ref, woa_ref), (wo32_ref, wo_ref)):
            dst[...] = src[...].astype(BF16)

    y_a = jnp.dot(a_ref[...], woc_ref[...], preferred_element_type=F32)
    bb = szb_ref[...].astype(F32) * at_ref[...].astype(F32)
    y_b = jnp.dot(bb.astype(BF16), woa_ref[...], preferred_element_type=F32)
    merged = sga_ref[...].astype(F32) * y_a + sgb_ref[...].astype(F32) * y_b
    y = jnp.dot(merged.astype(BF16), wo_ref[...], preferred_element_type=F32)
    ms = jnp.mean(y * y, axis=-1, keepdims=True)
    o_ref[...] = x_ref[...] + gt_ref[0] * ((y * lax.rsqrt(ms + EPS)) * pg_ref[...])


def _mixer(p5, attn, x2, gt, post_g, woc, woa, wo, seq, tm):
    m = x2.shape[0]
    tpb = seq // tm

    def part(k):
        return pl.BlockSpec((tm, PART_W), lambda i: (i, k))

    def full(shape):
        return pl.BlockSpec(shape, lambda i: (0,) * len(shape))

    def once(shape):
        return pl.BlockSpec(shape, lambda i: (0,) * len(shape), pipeline_mode=pl.Buffered(1))

    return pl.pallas_call(
        _mixer_kernel,
        grid=(m // tm,),
        in_specs=[part(OUT_A), part(OUT_SZB), part(OUT_SGA), part(OUT_SGB),
                  pl.BlockSpec((tm, PART_W), lambda i: (i, 0)),
                  pl.BlockSpec((tm, D_MODEL), lambda i: (i, 0)),
                  pl.BlockSpec((1, 1, D_MODEL), lambda i: (i // tpb, 0, 0)),
                  full((1, D_MODEL)),
                  once((PART_W, D_MODEL)), once((PART_W, D_MODEL)), once((D_MODEL, D_MODEL))],
        out_specs=pl.BlockSpec((tm, D_MODEL), lambda i: (i, 0)),
        out_shape=jax.ShapeDtypeStruct((m, D_MODEL), F32),
        scratch_shapes=[pltpu.VMEM((PART_W, D_MODEL), BF16)] * 2 + [pltpu.VMEM((D_MODEL, D_MODEL), BF16)],
        compiler_params=pltpu.CompilerParams(dimension_semantics=("arbitrary",)),
        name="mixer_out",
    )(p5, p5, p5, p5, attn, x2, gt, post_g, woc, woa, wo)


def _rope_tables(seq, tm):
    half = HEAD_DIM // 2
    q4 = HEAD_DIM // 4
    inv = ROPE_BASE ** (-jnp.arange(0, half, 2, dtype=F32) / half)
    lane_freq = np.arange(LANES) % q4

    def tables(n):
        ang = (jnp.arange(n, dtype=F32)[:, None] * inv)[:, lane_freq]
        return jnp.stack([jnp.cos(ang), jnp.sin(ang)])

    return tables(seq // GRID_W), jnp.tile(tables(GRID_W), (1, tm // GRID_W, 1))


def _layer(x, c, ctx, c_ctx, w_mod, b_mod, pre_g, post_g, w_in, conv_w, conv_b, rpb,
           w_out_conv, w_out_attn, w_o):
    batch, seq, _ = x.shape
    n_ctx = ctx.shape[1]
    rows = seq // GRID_W
    assert rows >= 2 * WIN_PAIRS and seq % LANES == 0 and batch <= 7

    cond8 = jnp.zeros((8, D_MODEL), F32).at[:batch].set(c).at[batch].set(c_ctx)
    mod = _modulation(cond8, w_mod, b_mod.reshape(1, -1))
    sh, sc, gt = (mod[:batch, k * D_MODEL:(k + 1) * D_MODEL].reshape(batch, 1, D_MODEL) for k in range(3))
    sh_c, sc_c = (mod[batch:batch + 1, k * D_MODEL:(k + 1) * D_MODEL] for k in range(2))

    pre_g2 = pre_g.reshape(1, D_MODEL)
    perm = jnp.asarray(_split_halves_permutation(), BF16)
    kc, vct = _ctx_kv(ctx.reshape(batch * n_ctx, D_MODEL), sc_c, sh_c, pre_g2, w_in, perm, batch)
    kc = kc.reshape(batch, n_ctx, PART_W)

    x2 = x.reshape(batch * seq, D_MODEL)
    tm_in = min(512, seq)
    rowtab, coltab = _rope_tables(seq, tm_in)
    p5, qt, vt = _inproj(x2, sc, sh, pre_g2, w_in, perm, rowtab, coltab, conv_w,
                         conv_b.reshape(1, -1), batch, seq, tm_in)

    bias = _bias_tables(rpb, rows)
    attn = _attention(p5, qt, vt, kc, vct, bias, batch, seq)

    tm_out = min(512, seq)
    out = _mixer(p5, attn, x2, gt, post_g.reshape(1, -1),
                 w_out_conv, w_out_attn, w_o, seq, tm_out)
    return out.reshape(batch, seq, D_MODEL)


def kernel(x, c, ctx, c_ctx, w_mod, b_mod, pre_g, post_g, w_in, conv_w, conv_b, rpb,
           w_out_conv, w_out_attn, w_o):
    depth = w_mod.shape[0]
    assert depth == 1, "context stream update between layers is not implemented"
    return _layer(x, c, ctx, c_ctx, w_mod[0], b_mod[0], pre_g[0], post_g[0], w_in[0], conv_w[0],
                  conv_b[0], rpb[0], w_out_conv[0], w_out_attn[0], w_o[0])
```

```python
import functools

import numpy as np
import jax
import jax.numpy as jnp
from jax import lax
from jax.experimental import pallas as pl
from jax.experimental.pallas import tpu as pltpu

D_MODEL = 1024
GRID_W = 64
N_HEADS = 16
HEAD_DIM = 64
WIN_ROWS = 8
WIN_COLS = 16
ROPE_BASE = 10000.0
EPS = 1e-6
PART_W = 1024
N_PARTS_IN = 10
PART_BG, PART_CG, PART_XI, PART_ZA, PART_Q, PART_K, PART_V, PART_ZB, PART_GA, PART_GB = range(N_PARTS_IN)
OUT_A, OUT_K, OUT_SZB, OUT_SGA, OUT_SGB = range(5)
N_PARTS_OUT = 5
W_CHUNK_ROWS = 16
W_SLOTS = 4
CONV_ROWS = 64
HALO = 16
LANES = 128
N_PAIRS = N_HEADS // 2
WIN_PAIRS = 5
NEG = -1e30
MAX_ROWS = 16
EXP_ROWS = 32
ATTN_GROUP = 8
SUM_ROWS = 16
LOG2E = 1.4426950408889634
BF16 = jnp.bfloat16
F32 = jnp.float32


def _modulated_norm(x, g, sc, sh):
    ms = jnp.mean(x * x, axis=-1, keepdims=True)
    return x * lax.rsqrt(ms + EPS) * (g * (1.0 + sc)) + sh


def _modulation_kernel(c_ref, w_ref, b_ref, o_ref):
    c = c_ref[...]
    o_ref[...] = jnp.dot(c * jax.nn.sigmoid(c), w_ref[...], preferred_element_type=F32,
                         precision=lax.Precision.HIGHEST) + b_ref[...]


def _modulation(cond8, w_mod, b_mod):
    n = w_mod.shape[1]
    tn = 768
    return pl.pallas_call(
        _modulation_kernel,
        grid=(n // tn,),
        in_specs=[pl.BlockSpec((8, D_MODEL), lambda j: (0, 0)),
                  pl.BlockSpec((D_MODEL, tn), lambda j: (0, j)),
                  pl.BlockSpec((1, tn), lambda j: (0, j))],
        out_specs=pl.BlockSpec((8, tn), lambda j: (0, j)),
        out_shape=jax.ShapeDtypeStruct((8, n), F32),
        name="modulation",
    )(cond8, w_mod, b_mod)


def _window_mask():
    q = np.arange(GRID_W)[None, :]
    c = np.arange(GRID_W)[:, None]
    cs = np.clip(q - WIN_COLS // 2, 0, GRID_W - WIN_COLS)
    negm = np.where((c >= cs) & (c < cs + WIN_COLS), 0.0, NEG).astype(np.float32)
    return np.concatenate([negm, negm], axis=-1)


def _pair_window(t, n_row_pairs):
    if isinstance(t, int):
        sp = min(max(t - 2, 0), n_row_pairs - WIN_PAIRS)
        var = t if t < 2 else (t - (n_row_pairs - 2) + 3 if t >= n_row_pairs - 2 else 2)
        return sp, var
    sp = jnp.clip(t - 2, 0, n_row_pairs - WIN_PAIRS)
    var = jnp.where(t < 2, t, jnp.where(t >= n_row_pairs - 2, t - (n_row_pairs - 2) + 3, 2))
    return sp, var


N_BIAS_VAR = 5


def _bias_tables(rpb, rows):
    n_dr = 2 * WIN_ROWS - 1
    n_rp = rows // 2
    lines = jnp.concatenate([rpb[..., WIN_COLS - 1::-1],
                             jnp.zeros(rpb.shape[:2] + (LANES - (2 * WIN_COLS - 1),), F32),
                             rpb[..., :WIN_COLS - 1:-1]], axis=-1).reshape(N_PAIRS, 2 * n_dr, LANES)
    idx = np.full((N_BIAS_VAR, 2 * WIN_PAIRS, 2, 2), -1, np.int32)
    for v, t in enumerate([0, 1, 2, n_rp - 2, n_rp - 1]):
        sp, _ = _pair_window(t, n_rp)
        for w in range(2 * WIN_PAIRS):
            krow = 2 * sp + w
            for rr in range(2):
                r = 2 * t + rr
                rs = min(max(r - WIN_ROWS // 2, 0), rows - WIN_ROWS)
                if rs <= krow < rs + WIN_ROWS:
                    idx[v, w, :, rr] = np.arange(2) * n_dr + (krow - r + WIN_ROWS - 1)
    return pl.pallas_call(
        functools.partial(_bias_assemble_kernel, idx=idx),
        grid=(N_PAIRS,),
        in_specs=[pl.BlockSpec((1, 2 * n_dr, LANES), lambda p: (p, 0, 0)),
                  pl.BlockSpec((GRID_W, LANES), lambda p: (0, 0))],
        out_specs=pl.BlockSpec((1, N_BIAS_VAR, 2 * WIN_PAIRS * GRID_W, 2 * LANES), lambda p: (p, 0, 0, 0)),
        out_shape=jax.ShapeDtypeStruct((N_PAIRS, N_BIAS_VAR, 2 * WIN_PAIRS * GRID_W, 2 * LANES), F32),
        name="bias_assemble",
    )(lines, jnp.asarray(_window_mask()))


def _bias_assemble_kernel(line_ref, mask_ref, o_ref, *, idx):
    left = lax.broadcasted_iota(jnp.int32, (GRID_W, LANES), 1) < GRID_W
    masked = jnp.full((GRID_W, LANES), NEG, F32)
    blocks = {}

    def block(i, lane_offset):
        if i < 0:
            return masked
        if (i, lane_offset) not in blocks:
            line = jnp.broadcast_to(line_ref[0, i:i + 1, :] * LOG2E, (GRID_W, LANES))
            blocks[i, lane_offset] = pltpu.roll(line, lane_offset, 1, stride=1, stride_axis=0)
        return blocks[i, lane_offset]

    n_var, n_win, _, _ = idx.shape
    for v in range(n_var):
        for w in range(n_win):
            for hh in range(2):
                i0, i1 = (int(i) for i in idx[v, w, hh])
                tile = jnp.where(left, block(i0, 0), block(i1, GRID_W)) + mask_ref[...]
                o_ref[0, v, w * GRID_W:(w + 1) * GRID_W, hh * LANES:(hh + 1) * LANES] = tile


def _ctx_kv_kernel(x_ref, sc_ref, sh_ref, g_ref, wk_ref, wv_ref, perm_ref, k_ref, vt_ref):
    h = _modulated_norm(x_ref[...], g_ref[...], sc_ref[...], sh_ref[...]).astype(BF16)
    k = jnp.dot(h, wk_ref[...].astype(BF16), preferred_element_type=F32).astype(BF16)
    k_ref[...] = jnp.dot(k, perm_ref[...], preferred_element_type=F32).astype(BF16)
    v = jnp.dot(h, wv_ref[...].astype(BF16), preferred_element_type=F32)
    n_ctx = vt_ref.shape[2]
    for b in range(vt_ref.shape[0]):
        vt_ref[b] = v[b * n_ctx:(b + 1) * n_ctx].astype(BF16).T


def _ctx_kv(ctx2, sc_c, sh_c, pre_g, w_in, perm, batch):
    m = ctx2.shape[0]
    n_ctx = m // batch

    def full(shape):
        return pl.BlockSpec(shape, lambda j: (0,) * len(shape))

    return pl.pallas_call(
        _ctx_kv_kernel,
        grid=(1,),
        in_specs=[full((m, D_MODEL)), full((1, D_MODEL)), full((1, D_MODEL)), full((1, D_MODEL)),
                  pl.BlockSpec((D_MODEL, PART_W), lambda j: (0, PART_K)),
                  pl.BlockSpec((D_MODEL, PART_W), lambda j: (0, PART_V)),
                  full((PART_W, PART_W))],
        out_specs=[full((m, PART_W)), full((batch, PART_W, n_ctx))],
        out_shape=[jax.ShapeDtypeStruct((m, PART_W), BF16),
                   jax.ShapeDtypeStruct((batch, PART_W, n_ctx), BF16)],
        name="ctx_kv",
    )(ctx2, sc_c, sh_c, pre_g, w_in, w_in, perm)


def _split_halves_permutation():
    q4 = HEAD_DIM // 4
    old = np.arange(PART_W).reshape(N_HEADS, 2, 2, q4).transpose(2, 0, 1, 3).reshape(PART_W)
    perm = np.zeros((PART_W, PART_W), np.float32)
    perm[old, np.arange(PART_W)] = 1.0
    return perm


def _stage_weights(w32_hbm, perm_ref, w_ref, wqk_ref, stage_ref, sem):
    n_slots = stage_ref.shape[0]
    n_chunks = w32_hbm.shape[0] // W_CHUNK_ROWS

    def chunk_copy(c):
        return pltpu.make_async_copy(w32_hbm.at[c * W_CHUNK_ROWS:(c + 1) * W_CHUNK_ROWS, :],
                                     stage_ref.at[c % n_slots], sem.at[c % n_slots])

    for c in range(n_slots - 1):
        chunk_copy(c).start()
    for c in range(n_chunks):
        if c + n_slots - 1 < n_chunks:
            chunk_copy(c + n_slots - 1).start()
        chunk_copy(c).wait()
        w_ref[c * W_CHUNK_ROWS:(c + 1) * W_CHUNK_ROWS, :] = stage_ref[c % n_slots].astype(BF16)
    for j, part in enumerate((PART_Q, PART_K)):
        wqk_ref[j] = jnp.dot(w_ref[:, part * PART_W:(part + 1) * PART_W], perm_ref[...],
                             preferred_element_type=F32).astype(BF16)


def _inproj_kernel(x_ref, xp_ref, xn_ref, sc_ref, sh_ref, g_ref, w32_hbm, perm_ref, rowtab_ref, coltab_ref,
                   cw_ref, cb_ref, p_ref, qt_ref, vt_ref, h_ref, u_ref, w_ref, wqk_ref, stage_ref, sem,
                   *, tiles_per_batch):
    tm = x_ref.shape[0]
    n_half = PART_W // (2 * LANES)
    t = pl.program_id(0) % tiles_per_batch

    @pl.when(pl.program_id(0) == 0)
    def _():
        _stage_weights(w32_hbm, perm_ref, w_ref, wqk_ref, stage_ref, sem)

    for rows, ref in ((slice(0, HALO), xp_ref), (slice(HALO, HALO + tm), x_ref),
                      (slice(HALO + tm, 2 * HALO + tm), xn_ref)):
        h_ref[rows, :] = _modulated_norm(ref[...], g_ref[...], sc_ref[0], sh_ref[0]).astype(BF16)
    tile = slice(HALO, HALO + tm)

    def slab(r, s):
        return r[:, s * LANES:(s + 1) * LANES]

    col_lane = (lax.broadcasted_iota(jnp.int32, (1, LANES), 1) // (HEAD_DIM // 4)) % 2 == 1

    def rope_table(k):
        return jnp.concatenate(
            [jnp.where(col_lane, coltab_ref[k, g * GRID_W:(g + 1) * GRID_W, :],
                       jnp.broadcast_to(rowtab_ref[k, g:g + 1, :], (GRID_W, LANES)))
             for g in range(tm // GRID_W)], axis=0)

    cos, sin = rope_table(0), rope_table(1)

    def rope(r):
        firsts, seconds = [], []
        for s in range(n_half):
            a, b = slab(r, s), slab(r, n_half + s)
            firsts.append(a * cos - b * sin)
            seconds.append(b * cos + a * sin)
        return firsts + seconds

    def store_transposed(t_ref, s, u):
        ut = u.astype(BF16).T
        for j in range(tm // LANES):
            t_ref[0, j, s * LANES:(s + 1) * LANES, :] = ut[:, j * LANES:(j + 1) * LANES]

    def project(n, rows):
        if n == PART_Q:
            w = wqk_ref[0]
        elif n == PART_K:
            w = wqk_ref[1]
        else:
            w = w_ref[:, n * PART_W:(n + 1) * PART_W]
        return jnp.dot(h_ref[rows, :], w, preferred_element_type=F32)

    def put(col, val):
        p_ref[:, col * PART_W:(col + 1) * PART_W] = val.astype(BF16)

    b_gate = project(PART_BG, tile)
    u_ref[...] = project(PART_CG, slice(None)) * project(PART_XI, slice(None))
    u_ref[0:HALO, :] = jnp.where(t == 0, 0.0, u_ref[0:HALO, :])
    u_ref[HALO + tm:, :] = jnp.where(t == tiles_per_batch - 1, 0.0, u_ref[HALO + tm:, :])
    z_a = project(PART_ZA, tile)
    for c in range(tm // CONV_ROWS):
        lo = HALO + c * CONV_ROWS
        rows = slice(c * CONV_ROWS, (c + 1) * CONV_ROWS)
        conv = (u_ref[lo - 1:lo - 1 + CONV_ROWS, :] * cw_ref[0:1, :] + u_ref[lo:lo + CONV_ROWS, :] * cw_ref[1:2, :]
                + u_ref[lo + 1:lo + 1 + CONV_ROWS, :] * cw_ref[2:3, :] + cb_ref[...])
        zc = z_a[rows]
        p_ref[rows, OUT_A * PART_W:(OUT_A + 1) * PART_W] = (
            (zc * jax.nn.sigmoid(zc)) * b_gate[rows] * conv).astype(BF16)

    for s, rk in enumerate(rope(project(PART_Q, tile))):
        store_transposed(qt_ref, s, rk * (HEAD_DIM ** -0.5 * LOG2E))
    for s, rk in enumerate(rope(project(PART_K, tile))):
        p_ref[:, OUT_K * PART_W + s * LANES:OUT_K * PART_W + (s + 1) * LANES] = rk.astype(BF16)
    r = project(PART_V, tile)
    for s in range(PART_W // LANES):
        store_transposed(vt_ref, s, slab(r, s))
    z_b = project(PART_ZB, tile)
    put(OUT_SZB, z_b * jax.nn.sigmoid(z_b))
    put(OUT_SGA, jax.nn.sigmoid(project(PART_GA, tile)))
    put(OUT_SGB, jax.nn.sigmoid(project(PART_GB, tile)))


def _inproj(x2, sc, sh, pre_g, w_in, perm, rowtab, coltab, conv_w, conv_b, batch, seq, tm):
    m = x2.shape[0]
    tpb = seq // tm
    hb = tm // HALO
    t_spec = pl.BlockSpec((1, tm // LANES, PART_W, LANES), lambda i: (i // tpb, i % tpb, 0, 0))
    t_shape = jax.ShapeDtypeStruct((batch, seq // LANES, PART_W, LANES), BF16)

    def resident(shape, index):
        return pl.BlockSpec(shape, lambda i: index, pipeline_mode=pl.Buffered(1))

    return pl.pallas_call(
        functools.partial(_inproj_kernel, tiles_per_batch=tpb),
        grid=(m // tm,),
        in_specs=[pl.BlockSpec((tm, D_MODEL), lambda i: (i, 0)),
                  pl.BlockSpec((HALO, D_MODEL), lambda i: (jnp.maximum(i * hb - 1, 0), 0)),
                  pl.BlockSpec((HALO, D_MODEL), lambda i: (jnp.minimum((i + 1) * hb, m // HALO - 1), 0)),
                  pl.BlockSpec((1, 1, D_MODEL), lambda i: (i // tpb, 0, 0)),
                  pl.BlockSpec((1, 1, D_MODEL), lambda i: (i // tpb, 0, 0)),
                  pl.BlockSpec((1, D_MODEL), lambda i: (0, 0)),
                  pl.BlockSpec(memory_space=pl.ANY),
                  resident((PART_W, PART_W), (0, 0)),
                  pl.BlockSpec((2, tm // GRID_W, LANES), lambda i: (0, i % tpb, 0)),
                  pl.BlockSpec((2, tm, LANES), lambda i: (0, 0, 0)),
                  pl.BlockSpec((3, PART_W), lambda i: (0, 0)),
                  pl.BlockSpec((1, PART_W), lambda i: (0, 0))],
        out_specs=[pl.BlockSpec((tm, N_PARTS_OUT * PART_W), lambda i: (i, 0)), t_spec, t_spec],
        out_shape=[jax.ShapeDtypeStruct((m, N_PARTS_OUT * PART_W), BF16), t_shape, t_shape],
        scratch_shapes=[pltpu.VMEM((tm + 2 * HALO, D_MODEL), BF16),
                        pltpu.VMEM((tm + 2 * HALO, PART_W), F32),
                        pltpu.VMEM((D_MODEL, N_PARTS_IN * PART_W), BF16),
                        pltpu.VMEM((2, D_MODEL, PART_W), BF16),
                        pltpu.VMEM((W_SLOTS, W_CHUNK_ROWS, N_PARTS_IN * PART_W), F32),
                        pltpu.SemaphoreType.DMA((W_SLOTS,))],
        compiler_params=pltpu.CompilerParams(dimension_semantics=("arbitrary",)),
        name="in_projection",
    )(x2, x2, x2, sc, sh, pre_g, w_in, perm, rowtab, coltab, conv_w, conv_b)


def _attn_kernel(qta_ref, qtb_ref, ka_ref, kb_ref, vt_ref, kca_ref, kcb_ref, vct_ref, bias_ref,
                 o_ref, *stage_refs, n_row_pairs):
    low = lax.broadcasted_iota(jnp.int32, (LANES, LANES), 0) < HEAD_DIM
    nb = WIN_PAIRS * LANES
    n_keys = nb + kca_ref.shape[1]
    last = n_row_pairs - 1
    head_of_dim = (lax.broadcasted_iota(jnp.int32, (2 * LANES, LANES), 0) % LANES) // (HEAD_DIM // 2)
    first_head = 2 * (pl.program_id(0) % 2)
    sel0 = head_of_dim == first_head
    sel1 = head_of_dim == first_head + 1

    def scores(t, slot):
        sp, var = _pair_window(t, n_row_pairs)
        qt = jnp.concatenate([qta_ref[0, t], qtb_ref[0, t]], axis=0)
        zero = jnp.zeros_like(qt)
        rhs = jnp.concatenate([jnp.where(sel0, qt, zero), jnp.where(sel1, qt, zero)], axis=1)
        rows = pl.ds(pl.multiple_of(sp * LANES, LANES), nb)
        kwin = jnp.concatenate([ka_ref[rows, :], kb_ref[rows, :]], axis=1)
        kc = jnp.concatenate([kca_ref[0], kcb_ref[0]], axis=1)
        s = jnp.dot(jnp.concatenate([kwin, kc], axis=0), rhs, preferred_element_type=F32)
        s_refs[slot][:nb, :] = s[:nb] + bias_ref[0, var]
        s_refs[slot][nb:, :] = s[nb:]

    def softmax(slot):
        s_ref, p_ref = s_refs[slot], p_refs[slot]
        m = s_ref[:MAX_ROWS, :]
        for c in range(1, n_keys // MAX_ROWS):
            m = jnp.maximum(m, s_ref[c * MAX_ROWS:(c + 1) * MAX_ROWS, :])
        m = jnp.max(m, axis=0, keepdims=True)
        for c in [slice(c * EXP_ROWS, (c + 1) * EXP_ROWS) for c in range(n_keys // EXP_ROWS)]:
            p_ref[c, :] = jnp.exp2((s_ref[c, :] - m).astype(BF16))

    def values(t, slot):
        sp, _ = _pair_window(t, n_row_pairs)
        p = p_refs[slot][...]
        vext = jnp.concatenate([vt_ref[0, sp + g] for g in range(WIN_PAIRS)] + [vct_ref[0]], axis=1)
        vext = jnp.concatenate([vext, jnp.ones((SUM_ROWS, n_keys), BF16)], axis=0)
        ot = jnp.dot(vext, p, preferred_element_type=F32)
        ot = ot[:LANES] * (1.0 / ot[LANES:LANES + 1])
        out = jnp.where(low, ot[:, :LANES], ot[:, LANES:]).T
        o_ref[pl.ds(pl.multiple_of(t * LANES, LANES), LANES), :] = out.astype(BF16)

    g = ATTN_GROUP
    s_refs, p_refs = stage_refs[:2 * g], stage_refs[2 * g:]
    for j in range(g):
        scores(j, j)

    def group(t, cur, nxt):
        for j in range(g):
            scores(jnp.minimum(t + j + g, last), nxt + j)
            softmax(cur + j)
            values(t + j, cur + j)

    def body(i, carry):
        t = 2 * g * i
        group(t, 0, g)
        group(t + g, g, 0)
        return carry

    lax.fori_loop(0, n_row_pairs // (2 * g), body, 0)


def _attention(p5, qt, vt, kc, vct, bias, batch, seq):
    n_ctx = kc.shape[1]
    n_keys = WIN_PAIRS * LANES + n_ctx
    n_slots = 2 * ATTN_GROUP
    assert (seq // LANES) % n_slots == 0
    spp = PART_W // LANES
    half = spp // 2

    def t_spec(slab):
        return pl.BlockSpec((1, seq // LANES, LANES, LANES), lambda hp, b: (b, 0, slab(hp), 0))

    return pl.pallas_call(
        functools.partial(_attn_kernel, n_row_pairs=seq // LANES),
        grid=(N_PAIRS, batch),
        in_specs=[t_spec(lambda hp: hp // 2), t_spec(lambda hp: half + hp // 2),
                  pl.BlockSpec((seq, LANES), lambda hp, b: (b, OUT_K * spp + hp // 2)),
                  pl.BlockSpec((seq, LANES), lambda hp, b: (b, OUT_K * spp + half + hp // 2)),
                  t_spec(lambda hp: hp),
                  pl.BlockSpec((1, n_ctx, LANES), lambda hp, b: (b, 0, hp // 2)),
                  pl.BlockSpec((1, n_ctx, LANES), lambda hp, b: (b, 0, half + hp // 2)),
                  pl.BlockSpec((1, LANES, n_ctx), lambda hp, b: (b, hp, 0)),
                  pl.BlockSpec((1, N_BIAS_VAR, WIN_PAIRS * LANES, 2 * LANES),
                               lambda hp, b: (hp, 0, 0, 0))],
        out_specs=pl.BlockSpec((seq, LANES), lambda hp, b: (b, hp)),
        out_shape=jax.ShapeDtypeStruct((batch * seq, N_HEADS * HEAD_DIM), BF16),
        scratch_shapes=[pltpu.VMEM((n_keys, 2 * LANES), F32)] * n_slots
        + [pltpu.VMEM((n_keys, 2 * LANES), BF16)] * n_slots,
        compiler_params=pltpu.CompilerParams(dimension_semantics=("arbitrary", "arbitrary")),
        name="attention",
    )(qt, qt, p5, p5, vt, kc, kc, vct, bias)


def _mixer_kernel(a_ref, szb_ref, sga_ref, sgb_ref, at_ref, x_ref, gt_ref, pg_ref,
                  woc32_ref, woa32_ref, wo32_ref, o_ref, woc_ref, woa_ref, wo_ref):
    @pl.when(pl.program_id(0) == 0)
    def _():
        for src, dst in ((woc32_ref, woc_ref), (woa32_ref, woa_ref), (wo32_ref, wo_ref)):
            dst[...] = src[...].astype(BF16)

    y_a = jnp.dot(a_ref[...], woc_ref[...], preferred_element_type=F32)
    bb = szb_ref[...].astype(F32) * at_ref[...].astype(F32)
    y_b = jnp.dot(bb.astype(BF16), woa_ref[...], preferred_element_type=F32)
    merged = sga_ref[...].astype(F32) * y_a + sgb_ref[...].astype(F32) * y_b
    y = jnp.dot(merged.astype(BF16), wo_ref[...], preferred_element_type=F32)
    ms = jnp.mean(y * y, axis=-1, keepdims=True)
    o_ref[...] = x_ref[...] + gt_ref[0] * ((y * lax.rsqrt(ms + EPS)) * pg_ref[...])


def _mixer(p5, attn, x2, gt, post_g, woc, woa, wo, seq, tm):
    m = x2.shape[0]
    tpb = seq // tm

    def part(k):
        return pl.BlockSpec((tm, PART_W), lambda i: (i, k))

    def full(shape):
        return pl.BlockSpec(shape, lambda i: (0,) * len(shape))

    def once(shape):
        return pl.BlockSpec(shape, lambda i: (0,) * len(shape), pipeline_mode=pl.Buffered(1))

    return pl.pallas_call(
        _mixer_kernel,
        grid=(m // tm,),
        in_specs=[part(OUT_A), part(OUT_SZB), part(OUT_SGA), part(OUT_SGB),
                  pl.BlockSpec((tm, PART_W), lambda i: (i, 0)),
                  pl.BlockSpec((tm, D_MODEL), lambda i: (i, 0)),
                  pl.BlockSpec((1, 1, D_MODEL), lambda i: (i // tpb, 0, 0)),
                  full((1, D_MODEL)),
                  once((PART_W, D_MODEL)), once((PART_W, D_MODEL)), once((D_MODEL, D_MODEL))],
        out_specs=pl.BlockSpec((tm, D_MODEL), lambda i: (i, 0)),
        out_shape=jax.ShapeDtypeStruct((m, D_MODEL), F32),
        scratch_shapes=[pltpu.VMEM((PART_W, D_MODEL), BF16)] * 2 + [pltpu.VMEM((D_MODEL, D_MODEL), BF16)],
        compiler_params=pltpu.CompilerParams(dimension_semantics=("arbitrary",)),
        name="mixer_out",
    )(p5, p5, p5, p5, attn, x2, gt, post_g, woc, woa, wo)


def _rope_tables(seq, tm):
    half = HEAD_DIM // 2
    q4 = HEAD_DIM // 4
    inv = ROPE_BASE ** (-jnp.arange(0, half, 2, dtype=F32) / half)
    lane_freq = np.arange(LANES) % q4

    def tables(n):
        ang = (jnp.arange(n, dtype=F32)[:, None] * inv)[:, lane_freq]
        return jnp.stack([jnp.cos(ang), jnp.sin(ang)])

    return tables(seq // GRID_W), jnp.tile(tables(GRID_W), (1, tm // GRID_W, 1))


def _layer(x, c, ctx, c_ctx, w_mod, b_mod, pre_g, post_g, w_in, conv_w, conv_b, rpb,
           w_out_conv, w_out_attn, w_o):
    batch, seq, _ = x.shape
    n_ctx = ctx.shape[1]
    rows = seq // GRID_W
    assert rows >= 2 * WIN_PAIRS and seq % LANES == 0 and batch <= 7

    cond8 = jnp.zeros((8, D_MODEL), F32).at[:batch].set(c).at[batch].set(c_ctx)
    mod = _modulation(cond8, w_mod, b_mod.reshape(1, -1))
    sh, sc, gt = (mod[:batch, k * D_MODEL:(k + 1) * D_MODEL].reshape(batch, 1, D_MODEL) for k in range(3))
    sh_c, sc_c = (mod[batch:batch + 1, k * D_MODEL:(k + 1) * D_MODEL] for k in range(2))

    pre_g2 = pre_g.reshape(1, D_MODEL)
    perm = jnp.asarray(_split_halves_permutation(), BF16)
    kc, vct = _ctx_kv(ctx.reshape(batch * n_ctx, D_MODEL), sc_c, sh_c, pre_g2, w_in, perm, batch)
    kc = kc.reshape(batch, n_ctx, PART_W)

    x2 = x.reshape(batch * seq, D_MODEL)
    tm_in = min(512, seq)
    rowtab, coltab = _rope_tables(seq, tm_in)
    p5, qt, vt = _inproj(x2, sc, sh, pre_g2, w_in, perm, rowtab, coltab, conv_w,
                         conv_b.reshape(1, -1), batch, seq, tm_in)

    bias = _bias_tables(rpb, rows)
    attn = _attention(p5, qt, vt, kc, vct, bias, batch, seq)

    tm_out = min(512, seq)
    out = _mixer(p5, attn, x2, gt, post_g.reshape(1, -1),
                 w_out_conv, w_out_attn, w_o, seq, tm_out)
    return out.reshape(batch, seq, D_MODEL)


def kernel(x, c, ctx, c_ctx, w_mod, b_mod, pre_g, post_g, w_in, conv_w, conv_b, rpb,
           w_out_conv, w_out_attn, w_o):
    depth = w_mod.shape[0]
    assert depth == 1, "context stream update between layers is not implemented"
    return _layer(x, c, ctx, c_ctx, w_mod[0], b_mod[0], pre_g[0], post_g[0], w_in[0], conv_w[0],
                  conv_b[0], rpb[0], w_out_conv[0], w_out_attn[0], w_o[0])
```

```python
import functools

import numpy as np
import jax
import jax.numpy as jnp
from jax import lax
from jax.experimental import pallas as pl
from jax.experimental.pallas import tpu as pltpu

D_MODEL = 1024
GRID_W = 64
N_HEADS = 16
HEAD_DIM = 64
WIN_ROWS = 8
WIN_COLS = 16
ROPE_BASE = 10000.0
EPS = 1e-6
PART_W = 1024
N_PARTS_IN = 10
PART_BG, PART_CG, PART_XI, PART_ZA, PART_Q, PART_K, PART_V, PART_ZB, PART_GA, PART_GB = range(N_PARTS_IN)
OUT_A, OUT_K, OUT_SZB, OUT_SGA, OUT_SGB = range(5)
N_PARTS_OUT = 5
W_CHUNK_ROWS = 16
W_SLOTS = 4
CONV_ROWS = 64
HALO = 16
LANES = 128
N_PAIRS = N_HEADS // 2
WIN_PAIRS = 5
NEG = -1e30
MAX_ROWS = 16
EXP_ROWS = 32
ATTN_GROUP = 8
SUM_ROWS = 16
LOG2E = 1.4426950408889634
BF16 = jnp.bfloat16
F32 = jnp.float32


def _modulated_norm(x, g, sc, sh):
    ms = jnp.mean(x * x, axis=-1, keepdims=True)
    return x * lax.rsqrt(ms + EPS) * (g * (1.0 + sc)) + sh


def _modulation_kernel(c_ref, w_ref, b_ref, o_ref):
    c = c_ref[...]
    o_ref[...] = jnp.dot(c * jax.nn.sigmoid(c), w_ref[...], preferred_element_type=F32,
                         precision=lax.Precision.HIGHEST) + b_ref[...]


def _modulation(cond8, w_mod, b_mod):
    n = w_mod.shape[1]
    tn = 768
    return pl.pallas_call(
        _modulation_kernel,
        grid=(n // tn,),
        in_specs=[pl.BlockSpec((8, D_MODEL), lambda j: (0, 0)),
                  pl.BlockSpec((D_MODEL, tn), lambda j: (0, j)),
                  pl.BlockSpec((1, tn), lambda j: (0, j))],
        out_specs=pl.BlockSpec((8, tn), lambda j: (0, j)),
        out_shape=jax.ShapeDtypeStruct((8, n), F32),
        name="modulation",
    )(cond8, w_mod, b_mod)


def _window_mask():
    q = np.arange(GRID_W)[None, :]
    c = np.arange(GRID_W)[:, None]
    cs = np.clip(q - WIN_COLS // 2, 0, GRID_W - WIN_COLS)
    negm = np.where((c >= cs) & (c < cs + WIN_COLS), 0.0, NEG).astype(np.float32)
    return np.concatenate([negm, negm], axis=-1)


def _pair_window(t, n_row_pairs):
    if isinstance(t, int):
        sp = min(max(t - 2, 0), n_row_pairs - WIN_PAIRS)
        var = t if t < 2 else (t - (n_row_pairs - 2) + 3 if t >= n_row_pairs - 2 else 2)
        return sp, var
    sp = jnp.clip(t - 2, 0, n_row_pairs - WIN_PAIRS)
    var = jnp.where(t < 2, t, jnp.where(t >= n_row_pairs - 2, t - (n_row_pairs - 2) + 3, 2))
    return sp, var


N_BIAS_VAR = 5


def _bias_tables(rpb, rows):
    n_dr = 2 * WIN_ROWS - 1
    n_rp = rows // 2
    lines = jnp.concatenate([rpb[..., WIN_COLS - 1::-1],
                             jnp.zeros(rpb.shape[:2] + (LANES - (2 * WIN_COLS - 1),), F32),
                             rpb[..., :WIN_COLS - 1:-1]], axis=-1).reshape(N_PAIRS, 2 * n_dr, LANES)
    idx = np.full((N_BIAS_VAR, 2 * WIN_PAIRS, 2, 2), -1, np.int32)
    for v, t in enumerate([0, 1, 2, n_rp - 2, n_rp - 1]):
        sp, _ = _pair_window(t, n_rp)
        for w in range(2 * WIN_PAIRS):
            krow = 2 * sp + w
            for rr in range(2):
                r = 2 * t + rr
                rs = min(max(r - WIN_ROWS // 2, 0), rows - WIN_ROWS)
                if rs <= krow < rs + WIN_ROWS:
                    idx[v, w, :, rr] = np.arange(2) * n_dr + (krow - r + WIN_ROWS - 1)
    return pl.pallas_call(
        functools.partial(_bias_assemble_kernel, idx=idx),
        grid=(N_PAIRS,),
        in_specs=[pl.BlockSpec((1, 2 * n_dr, LANES), lambda p: (p, 0, 0)),
                  pl.BlockSpec((GRID_W, LANES), lambda p: (0, 0))],
        out_specs=pl.BlockSpec((1, N_BIAS_VAR, 2 * WIN_PAIRS * GRID_W, 2 * LANES), lambda p: (p, 0, 0, 0)),
        out_shape=jax.ShapeDtypeStruct((N_PAIRS, N_BIAS_VAR, 2 * WIN_PAIRS * GRID_W, 2 * LANES), F32),
        name="bias_assemble",
    )(lines, jnp.asarray(_window_mask()))


def _bias_assemble_kernel(line_ref, mask_ref, o_ref, *, idx):
    left = lax.broadcasted_iota(jnp.int32, (GRID_W, LANES), 1) < GRID_W
    masked = jnp.full((GRID_W, LANES), NEG, F32)
    blocks = {}

    def block(i, lane_offset):
        if i < 0:
            return masked
        if (i, lane_offset) not in blocks:
            line = jnp.broadcast_to(line_ref[0, i:i + 1, :] * LOG2E, (GRID_W, LANES))
            blocks[i, lane_offset] = pltpu.roll(line, lane_offset, 1, stride=1, stride_axis=0)
        return blocks[i, lane_offset]

    n_var, n_win, _, _ = idx.shape
    for v in range(n_var):
        for w in range(n_win):
            for hh in range(2):
                i0, i1 = (int(i) for i in idx[v, w, hh])
                tile = jnp.where(left, block(i0, 0), block(i1, GRID_W)) + mask_ref[...]
                o_ref[0, v, w * GRID_W:(w + 1) * GRID_W, hh * LANES:(hh + 1) * LANES] = tile


def _ctx_kv_kernel(x_ref, sc_ref, sh_ref, g_ref, wk_ref, wv_ref, perm_ref, k_ref, vt_ref):
    h = _modulated_norm(x_ref[...], g_ref[...], sc_ref[...], sh_ref[...]).astype(BF16)
    k = jnp.dot(h, wk_ref[...].astype(BF16), preferred_element_type=F32).astype(BF16)
    k_ref[...] = jnp.dot(k, perm_ref[...], preferred_element_type=F32).astype(BF16)
    v = jnp.dot(h, wv_ref[...].astype(BF16), preferred_element_type=F32)
    n_ctx = vt_ref.shape[2]
    for b in range(vt_ref.shape[0]):
        vt_ref[b] = v[b * n_ctx:(b + 1) * n_ctx].astype(BF16).T


def _ctx_kv(ctx2, sc_c, sh_c, pre_g, w_in, perm, batch):
    m = ctx2.shape[0]
    n_ctx = m // batch

    def full(shape):
        return pl.BlockSpec(shape, lambda j: (0,) * len(shape))

    return pl.pallas_call(
        _ctx_kv_kernel,
        grid=(1,),
        in_specs=[full((m, D_MODEL)), full((1, D_MODEL)), full((1, D_MODEL)), full((1, D_MODEL)),
                  pl.BlockSpec((D_MODEL, PART_W), lambda j: (0, PART_K)),
                  pl.BlockSpec((D_MODEL, PART_W), lambda j: (0, PART_V)),
                  full((PART_W, PART_W))],
        out_specs=[full((m, PART_W)), full((batch, PART_W, n_ctx))],
        out_shape=[jax.ShapeDtypeStruct((m, PART_W), BF16),
                   jax.ShapeDtypeStruct((batch, PART_W, n_ctx), BF16)],
        name="ctx_kv",
    )(ctx2, sc_c, sh_c, pre_g, w_in, w_in, perm)


def _split_halves_permutation():
    q4 = HEAD_DIM // 4
    old = np.arange(PART_W).reshape(N_HEADS, 2, 2, q4).transpose(2, 0, 1, 3).reshape(PART_W)
    perm = np.zeros((PART_W, PART_W), np.float32)
    perm[old, np.arange(PART_W)] = 1.0
    return perm


def _stage_weights(w32_hbm, perm_ref, w_ref, wqk_ref, stage_ref, sem):
    n_slots = stage_ref.shape[0]
    n_chunks = w32_hbm.shape[0] // W_CHUNK_ROWS

    def chunk_copy(c):
        return pltpu.make_async_copy(w32_hbm.at[c * W_CHUNK_ROWS:(c + 1) * W_CHUNK_ROWS, :],
                                     stage_ref.at[c % n_slots], sem.at[c % n_slots])

    for c in range(n_slots - 1):
        chunk_copy(c).start(priority=c % 2)
    for c in range(n_chunks):
        if c + n_slots - 1 < n_chunks:
            chunk_copy(c + n_slots - 1).start(priority=(c + n_slots - 1) % 2)
        chunk_copy(c).wait()
        w_ref[c * W_CHUNK_ROWS:(c + 1) * W_CHUNK_ROWS, :] = stage_ref[c % n_slots].astype(BF16)
    for j, part in enumerate((PART_Q, PART_K)):
        wqk_ref[j] = jnp.dot(w_ref[:, part * PART_W:(part + 1) * PART_W], perm_ref[...],
                             preferred_element_type=F32).astype(BF16)


def _inproj_kernel(x_ref, xp_ref, xn_ref, sc_ref, sh_ref, g_ref, w32_hbm, perm_ref, rowtab_ref, coltab_ref,
                   cw_ref, cb_ref, p_ref, qt_ref, vt_ref, h_ref, u_ref, w_ref, wqk_ref, stage_ref, sem,
                   *, tiles_per_batch):
    tm = x_ref.shape[0]
    n_half = PART_W // (2 * LANES)
    t = pl.program_id(0) % tiles_per_batch

    @pl.when(pl.program_id(0) == 0)
    def _():
        _stage_weights(w32_hbm, perm_ref, w_ref, wqk_ref, stage_ref, sem)

    for rows, ref in ((slice(0, HALO), xp_ref), (slice(HALO, HALO + tm), x_ref),
                      (slice(HALO + tm, 2 * HALO + tm), xn_ref)):
        h_ref[rows, :] = _modulated_norm(ref[...], g_ref[...], sc_ref[0], sh_ref[0]).astype(BF16)
    tile = slice(HALO, HALO + tm)

    def slab(r, s):
        return r[:, s * LANES:(s + 1) * LANES]

    col_lane = (lax.broadcasted_iota(jnp.int32, (1, LANES), 1) // (HEAD_DIM // 4)) % 2 == 1

    def rope_table(k):
        return jnp.concatenate(
            [jnp.where(col_lane, coltab_ref[k, g * GRID_W:(g + 1) * GRID_W, :],
                       jnp.broadcast_to(rowtab_ref[k, g:g + 1, :], (GRID_W, LANES)))
             for g in range(tm // GRID_W)], axis=0)

    cos, sin = rope_table(0), rope_table(1)

    def rope(r):
        firsts, seconds = [], []
        for s in range(n_half):
            a, b = slab(r, s), slab(r, n_half + s)
            firsts.append(a * cos - b * sin)
            seconds.append(b * cos + a * sin)
        return firsts + seconds

    def store_transposed(t_ref, s, u):
        ut = u.astype(BF16).T
        for j in range(tm // LANES):
            t_ref[0, j, s * LANES:(s + 1) * LANES, :] = ut[:, j * LANES:(j + 1) * LANES]

    def project(n, rows):
        if n == PART_Q:
            w = wqk_ref[0]
        elif n == PART_K:
            w = wqk_ref[1]
        else:
            w = w_ref[:, n * PART_W:(n + 1) * PART_W]
        return jnp.dot(h_ref[rows, :], w, preferred_element_type=F32)

    def put(col, val):
        p_ref[:, col * PART_W:(col + 1) * PART_W] = val.astype(BF16)

    b_gate = project(PART_BG, tile)
    u_ref[...] = project(PART_CG, slice(None)) * project(PART_XI, slice(None))
    u_ref[0:HALO, :] = jnp.where(t == 0, 0.0, u_ref[0:HALO, :])
    u_ref[HALO + tm:, :] = jnp.where(t == tiles_per_batch - 1, 0.0, u_ref[HALO + tm:, :])
    z_a = project(PART_ZA, tile)
    for c in range(tm // CONV_ROWS):
        lo = HALO + c * CONV_ROWS
        rows = slice(c * CONV_ROWS, (c + 1) * CONV_ROWS)
        conv = (u_ref[lo - 1:lo - 1 + CONV_ROWS, :] * cw_ref[0:1, :] + u_ref[lo:lo + CONV_ROWS, :] * cw_ref[1:2, :]
                + u_ref[lo + 1:lo + 1 + CONV_ROWS, :] * cw_ref[2:3, :] + cb_ref[...])
        zc = z_a[rows]
        p_ref[rows, OUT_A * PART_W:(OUT_A + 1) * PART_W] = (
            (zc * jax.nn.sigmoid(zc)) * b_gate[rows] * conv).astype(BF16)

    for s, rk in enumerate(rope(project(PART_Q, tile))):
        store_transposed(qt_ref, s, rk * (HEAD_DIM ** -0.5 * LOG2E))
    for s, rk in enumerate(rope(project(PART_K, tile))):
        p_ref[:, OUT_K * PART_W + s * LANES:OUT_K * PART_W + (s + 1) * LANES] = rk.astype(BF16)
    r = project(PART_V, tile)
    for s in range(PART_W // LANES):
        store_transposed(vt_ref, s, slab(r, s))
    z_b = project(PART_ZB, tile)
    put(OUT_SZB, z_b * jax.nn.sigmoid(z_b))
    put(OUT_SGA, jax.nn.sigmoid(project(PART_GA, tile)))
    put(OUT_SGB, jax.nn.sigmoid(project(PART_GB, tile)))


def _inproj(x2, sc, sh, pre_g, w_in, perm, rowtab, coltab, conv_w, conv_b, batch, seq, tm):
    m = x2.shape[0]
    tpb = seq // tm
    hb = tm // HALO
    t_spec = pl.BlockSpec((1, tm // LANES, PART_W, LANES), lambda i: (i // tpb, i % tpb, 0, 0))
    t_shape = jax.ShapeDtypeStruct((batch, seq // LANES, PART_W, LANES), BF16)

    def resident(shape, index):
        return pl.BlockSpec(shape, lambda i: index, pipeline_mode=pl.Buffered(1))

    return pl.pallas_call(
        functools.partial(_inproj_kernel, tiles_per_batch=tpb),
        grid=(m // tm,),
        in_specs=[pl.BlockSpec((tm, D_MODEL), lambda i: (i, 0)),
                  pl.BlockSpec((HALO, D_MODEL), lambda i: (jnp.maximum(i * hb - 1, 0), 0)),
                  pl.BlockSpec((HALO, D_MODEL), lambda i: (jnp.minimum((i + 1) * hb, m // HALO - 1), 0)),
                  pl.BlockSpec((1, 1, D_MODEL), lambda i: (i // tpb, 0, 0)),
                  pl.BlockSpec((1, 1, D_MODEL), lambda i: (i // tpb, 0, 0)),
                  pl.BlockSpec((1, D_MODEL), lambda i: (0, 0)),
                  pl.BlockSpec(memory_space=pl.ANY),
                  resident((PART_W, PART_W), (0, 0)),
                  pl.BlockSpec((2, tm // GRID_W, LANES), lambda i: (0, i % tpb, 0)),
                  pl.BlockSpec((2, tm, LANES), lambda i: (0, 0, 0)),
                  pl.BlockSpec((3, PART_W), lambda i: (0, 0)),
                  pl.BlockSpec((1, PART_W), lambda i: (0, 0))],
        out_specs=[pl.BlockSpec((tm, N_PARTS_OUT * PART_W), lambda i: (i, 0)), t_spec, t_spec],
        out_shape=[jax.ShapeDtypeStruct((m, N_PARTS_OUT * PART_W), BF16), t_shape, t_shape],
        scratch_shapes=[pltpu.VMEM((tm + 2 * HALO, D_MODEL), BF16),
                        pltpu.VMEM((tm + 2 * HALO, PART_W), F32),
                        pltpu.VMEM((D_MODEL, N_PARTS_IN * PART_W), BF16),
                        pltpu.VMEM((2, D_MODEL, PART_W), BF16),
                        pltpu.VMEM((W_SLOTS, W_CHUNK_ROWS, N_PARTS_IN * PART_W), F32),
                        pltpu.SemaphoreType.DMA((W_SLOTS,))],
        compiler_params=pltpu.CompilerParams(dimension_semantics=("arbitrary",)),
        name="in_projection",
    )(x2, x2, x2, sc, sh, pre_g, w_in, perm, rowtab, coltab, conv_w, conv_b)


def _attn_kernel(qta_ref, qtb_ref, ka_ref, kb_ref, vt_ref, kca_ref, kcb_ref, vct_ref, bias_ref,
                 o_ref, *stage_refs, n_row_pairs):
    low = lax.broadcasted_iota(jnp.int32, (LANES, LANES), 0) < HEAD_DIM
    nb = WIN_PAIRS * LANES
    n_keys = nb + kca_ref.shape[1]
    last = n_row_pairs - 1
    head_of_dim = (lax.broadcasted_iota(jnp.int32, (2 * LANES, LANES), 0) % LANES) // (HEAD_DIM // 2)
    first_head = 2 * (pl.program_id(0) % 2)
    sel0 = head_of_dim == first_head
    sel1 = head_of_dim == first_head + 1

    def scores(t, slot):
        sp, var = _pair_window(t, n_row_pairs)
        qt = jnp.concatenate([qta_ref[0, t], qtb_ref[0, t]], axis=0)
        zero = jnp.zeros_like(qt)
        rhs = jnp.concatenate([jnp.where(sel0, qt, zero), jnp.where(sel1, qt, zero)], axis=1)
        rows = pl.ds(pl.multiple_of(sp * LANES, LANES), nb)
        kwin = jnp.concatenate([ka_ref[rows, :], kb_ref[rows, :]], axis=1)
        kc = jnp.concatenate([kca_ref[0], kcb_ref[0]], axis=1)
        s = jnp.dot(jnp.concatenate([kwin, kc], axis=0), rhs, preferred_element_type=F32)
        s_refs[slot][:nb, :] = s[:nb] + bias_ref[0, var]
        s_refs[slot][nb:, :] = s[nb:]

    def softmax(slot):
        s_ref, p_ref = s_refs[slot], p_refs[slot]
        m = s_ref[:MAX_ROWS, :]
        for c in range(1, n_keys // MAX_ROWS):
            m = jnp.maximum(m, s_ref[c * MAX_ROWS:(c + 1) * MAX_ROWS, :])
        m = jnp.max(m, axis=0, keepdims=True)
        for c in [slice(c * EXP_ROWS, (c + 1) * EXP_ROWS) for c in range(n_keys // EXP_ROWS)]:
            p_ref[c, :] = jnp.exp2((s_ref[c, :] - m).astype(BF16))

    def values(t, slot):
        sp, _ = _pair_window(t, n_row_pairs)
        p = p_refs[slot][...]
        vext = jnp.concatenate([vt_ref[0, sp + g] for g in range(WIN_PAIRS)] + [vct_ref[0]], axis=1)
        vext = jnp.concatenate([vext, jnp.ones((SUM_ROWS, n_keys), BF16)], axis=0)
        ot = jnp.dot(vext, p, preferred_element_type=F32)
        ot = ot[:LANES] * (1.0 / ot[LANES:LANES + 1])
        out = jnp.where(low, ot[:, :LANES], ot[:, LANES:]).T
        o_ref[pl.ds(pl.multiple_of(t * LANES, LANES), LANES), :] = out.astype(BF16)

    g = ATTN_GROUP
    s_refs, p_refs = stage_refs[:2 * g], stage_refs[2 * g:]
    for j in range(g):
        scores(j, j)

    def group(t, cur, nxt):
        for j in range(g):
            scores(jnp.minimum(t + j + g, last), nxt + j)
            softmax(cur + j)
            values(t + j, cur + j)

    def body(i, carry):
        t = 2 * g * i
        group(t, 0, g)
        group(t + g, g, 0)
        return carry

    lax.fori_loop(0, n_row_pairs // (2 * g), body, 0)


def _attention(p5, qt, vt, kc, vct, bias, batch, seq):
    n_ctx = kc.shape[1]
    n_keys = WIN_PAIRS * LANES + n_ctx
    n_slots = 2 * ATTN_GROUP
    assert (seq // LANES) % n_slots == 0
    spp = PART_W // LANES
    half = spp // 2

    def t_spec(slab):
        return pl.BlockSpec((1, seq // LANES, LANES, LANES), lambda hp, b: (b, 0, slab(hp), 0))

    return pl.pallas_call(
        functools.partial(_attn_kernel, n_row_pairs=seq // LANES),
        grid=(N_PAIRS, batch),
        in_specs=[t_spec(lambda hp: hp // 2), t_spec(lambda hp: half + hp // 2),
                  pl.BlockSpec((seq, LANES), lambda hp, b: (b, OUT_K * spp + hp // 2)),
                  pl.BlockSpec((seq, LANES), lambda hp, b: (b, OUT_K * spp + half + hp // 2)),
                  t_spec(lambda hp: hp),
                  pl.BlockSpec((1, n_ctx, LANES), lambda hp, b: (b, 0, hp // 2)),
                  pl.BlockSpec((1, n_ctx, LANES), lambda hp, b: (b, 0, half + hp // 2)),
                  pl.BlockSpec((1, LANES, n_ctx), lambda hp, b: (b, hp, 0)),
                  pl.BlockSpec((1, N_BIAS_VAR, WIN_PAIRS * LANES, 2 * LANES),
                               lambda hp, b: (hp, 0, 0, 0))],
        out_specs=pl.BlockSpec((seq, LANES), lambda hp, b: (b, hp)),
        out_shape=jax.ShapeDtypeStruct((batch * seq, N_HEADS * HEAD_DIM), BF16),
        scratch_shapes=[pltpu.VMEM((n_keys, 2 * LANES), F32)] * n_slots
        + [pltpu.VMEM((n_keys, 2 * LANES), BF16)] * n_slots,
        compiler_params=pltpu.CompilerParams(dimension_semantics=("arbitrary", "arbitrary")),
        name="attention",
    )(qt, qt, p5, p5, vt, kc, kc, vct, bias)


def _mixer_kernel(a_ref, szb_ref, sga_ref, sgb_ref, at_ref, x_ref, gt_ref, pg_ref,
                  woc32_ref, woa32_ref, wo32_ref, o_ref, woc_ref, woa_ref, wo_ref):
    @pl.when(pl.program_id(0) == 0)
    def _():
        for src, dst in ((woc32_ref, woc_ref), (woa32_ref, woa_ref), (wo32_ref, wo_ref)):
            dst[...] = src[...].astype(BF16)

    y_a = jnp.dot(a_ref[...], woc_ref[...], preferred_element_type=F32)
    bb = szb_ref[...].astype(F32) * at_ref[...].astype(F32)
    y_b = jnp.dot(bb.astype(BF16), woa_ref[...], preferred_element_type=F32)
    merged = sga_ref[...].astype(F32) * y_a + sgb_ref[...].astype(F32) * y_b
    y = jnp.dot(merged.astype(BF16), wo_ref[...], preferred_element_type=F32)
    ms = jnp.mean(y * y, axis=-1, keepdims=True)
    o_ref[...] = x_ref[...] + gt_ref[0] * ((y * lax.rsqrt(ms + EPS)) * pg_ref[...])


def _mixer(p5, attn, x2, gt, post_g, woc, woa, wo, seq, tm):
    m = x2.shape[0]
    tpb = seq // tm

    def part(k):
        return pl.BlockSpec((tm, PART_W), lambda i: (i, k))

    def full(shape):
        return pl.BlockSpec(shape, lambda i: (0,) * len(shape))

    def once(shape):
        return pl.BlockSpec(shape, lambda i: (0,) * len(shape), pipeline_mode=pl.Buffered(1))

    return pl.pallas_call(
        _mixer_kernel,
        grid=(m // tm,),
        in_specs=[part(OUT_A), part(OUT_SZB), part(OUT_SGA), part(OUT_SGB),
                  pl.BlockSpec((tm, PART_W), lambda i: (i, 0)),
                  pl.BlockSpec((tm, D_MODEL), lambda i: (i, 0)),
                  pl.BlockSpec((1, 1, D_MODEL), lambda i: (i // tpb, 0, 0)),
                  full((1, D_MODEL)),
                  once((PART_W, D_MODEL)), once((PART_W, D_MODEL)), once((D_MODEL, D_MODEL))],
        out_specs=pl.BlockSpec((tm, D_MODEL), lambda i: (i, 0)),
        out_shape=jax.ShapeDtypeStruct((m, D_MODEL), F32),
        scratch_shapes=[pltpu.VMEM((PART_W, D_MODEL), BF16)] * 2 + [pltpu.VMEM((D_MODEL, D_MODEL), BF16)],
        compiler_params=pltpu.CompilerParams(dimension_semantics=("arbitrary",)),
        name="mixer_out",
    )(p5, p5, p5, p5, attn, x2, gt, post_g, woc, woa, wo)


def _rope_tables(seq, tm):
    half = HEAD_DIM // 2
    q4 = HEAD_DIM // 4
    inv = ROPE_BASE ** (-jnp.arange(0, half, 2, dtype=F32) / half)
    lane_freq = np.arange(LANES) % q4

    def tables(n):
        ang = (jnp.arange(n, dtype=F32)[:, None] * inv)[:, lane_freq]
        return jnp.stack([jnp.cos(ang), jnp.sin(ang)])

    return tables(seq // GRID_W), jnp.tile(tables(GRID_W), (1, tm // GRID_W, 1))


def _layer(x, c, ctx, c_ctx, w_mod, b_mod, pre_g, post_g, w_in, conv_w, conv_b, rpb,
           w_out_conv, w_out_attn, w_o):
    batch, seq, _ = x.shape
    n_ctx = ctx.shape[1]
    rows = seq // GRID_W
    assert rows >= 2 * WIN_PAIRS and seq % LANES == 0 and batch <= 7

    cond8 = jnp.zeros((8, D_MODEL), F32).at[:batch].set(c).at[batch].set(c_ctx)
    mod = _modulation(cond8, w_mod, b_mod.reshape(1, -1))
    sh, sc, gt = (mod[:batch, k * D_MODEL:(k + 1) * D_MODEL].reshape(batch, 1, D_MODEL) for k in range(3))
    sh_c, sc_c = (mod[batch:batch + 1, k * D_MODEL:(k + 1) * D_MODEL] for k in range(2))

    pre_g2 = pre_g.reshape(1, D_MODEL)
    perm = jnp.asarray(_split_halves_permutation(), BF16)
    kc, vct = _ctx_kv(ctx.reshape(batch * n_ctx, D_MODEL), sc_c, sh_c, pre_g2, w_in, perm, batch)
    kc = kc.reshape(batch, n_ctx, PART_W)

    x2 = x.reshape(batch * seq, D_MODEL)
    tm_in = min(512, seq)
    rowtab, coltab = _rope_tables(seq, tm_in)
    p5, qt, vt = _inproj(x2, sc, sh, pre_g2, w_in, perm, rowtab, coltab, conv_w,
                         conv_b.reshape(1, -1), batch, seq, tm_in)

    bias = _bias_tables(rpb, rows)
    attn = _attention(p5, qt, vt, kc, vct, bias, batch, seq)

    tm_out = min(512, seq)
    out = _mixer(p5, attn, x2, gt, post_g.reshape(1, -1),
                 w_out_conv, w_out_attn, w_o, seq, tm_out)
    return out.reshape(batch, seq, D_MODEL)


def kernel(x, c, ctx, c_ctx, w_mod, b_mod, pre_g, post_g, w_in, conv_w, conv_b, rpb,
           w_out_conv, w_out_attn, w_o):
    depth = w_mod.shape[0]
    assert depth == 1, "context stream update between layers is not implemented"
    return _layer(x, c, ctx, c_ctx, w_mod[0], b_mod[0], pre_g[0], post_g[0], w_in[0], conv_w[0],
                  conv_b[0], rpb[0], w_out_conv[0], w_out_attn[0], w_o[0])
```
